```python
import jax, jax.numpy as jnp
from jax import lax
import numpy as np

D_MODEL = 1024
BATCH = 16
SEQ = 2048
DEPTH = 4

CHUNK = 64
HEAD_DIM = 64
FOX_HEADS = 8
HGRN_HEADS = 8
FOX_WIDTH = FOX_HEADS * HEAD_DIM
HGRN_WIDTH = HGRN_HEADS * HEAD_DIM
N_BRANCHES = 2
Q_BLOCK = 128
N_EXPERTS = 16
N_GROUPS = 4
EXPERTS_PER_GROUP = N_EXPERTS // N_GROUPS
TOP_K = 2
D_EXPERT = D_MODEL
MOE_BLOCK = 512
LN_EPS = 1e-5
RMS_EPS = 1e-6
MASK_VALUE = -1e30
ALPHA = (2 * DEPTH) ** 0.25
BETA = (8 * DEPTH) ** -0.25
IN_SIZES = (FOX_WIDTH, FOX_WIDTH, FOX_WIDTH, FOX_HEADS, HGRN_WIDTH, HGRN_WIDTH, HGRN_WIDTH, HGRN_WIDTH, N_BRANCHES * D_MODEL)
N_IN = sum(IN_SIZES)
VALUE_SLOTS = (2, 6)
FOX_F_SLOT = 3

kernel_name = 'hybrid_fox_hgrn2_grouped_moe_deepnorm'


def layer_norm(x, g, b):
    xf = x.astype(jnp.float32)
    mu = xf.mean(-1, keepdims=True)
    var = jnp.square(xf - mu).mean(-1, keepdims=True)
    return ((xf - mu) * lax.rsqrt(var + LN_EPS) * g + b).astype(x.dtype)


def split_cols(h, sizes):
    out, off = [], 0
    for s in sizes:
        out.append(h[..., off:off + s])
        off += s
    return out


def to_heads(t, n_heads):
    B, S, W = t.shape
    return t.reshape(B, S, n_heads, W // n_heads).transpose(0, 2, 1, 3)


def from_heads(t):
    B, H, S, d = t.shape
    return t.transpose(0, 2, 1, 3).reshape(B, S, H * d)


def fox_attention(q, k, v, logf):
    B, H, S, d = q.shape
    F = jnp.cumsum(logf, axis=-1)
    scale = d ** -0.5
    outs = []
    for blk in range(S // Q_BLOCK):
        q0, q1 = blk * Q_BLOCK, (blk + 1) * Q_BLOCK
        s = jnp.einsum('bhqd,bhkd->bhqk', q[:, :, q0:q1], k[:, :, :q1]).astype(jnp.float32) * scale
        s = s + F[:, :, q0:q1, None] - F[:, :, None, :q1]
        qpos = jnp.arange(q0, q1)[:, None]
        kpos = jnp.arange(q1)[None, :]
        s = jnp.where(kpos <= qpos, s, MASK_VALUE)
        p = jax.nn.softmax(s, axis=-1)
        outs.append(jnp.einsum('bhqk,bhkd->bhqd', p.astype(v.dtype), v[:, :, :q1]))
    return jnp.concatenate(outs, axis=2)


def hgrn2_chunkwise(q, k, v, logf):
    B, H, S, dk = q.shape
    dv = v.shape[-1]
    n = S // CHUNK

    def to_chunks(t):
        return jnp.moveaxis(t.reshape(B, H, n, CHUNK, t.shape[-1]), 2, 0)

    causal = jnp.tril(jnp.ones((CHUNK, CHUNK), dtype=bool))[:, :, None]

    def step(state, inp):
        qt, kt, vt, gt = inp
        b = jnp.cumsum(gt, axis=-2)
        b_last = b[:, :, -1:, :]
        inter = jnp.einsum('bhtd,bhde->bhte', qt * jnp.exp(b), state)
        diff = b[:, :, :, None, :] - b[:, :, None, :, :]
        decay = jnp.where(causal, jnp.exp(jnp.where(causal, diff, 0.0)), 0.0)
        scores = jnp.einsum('bhtsd,bhsd->bhts', qt[:, :, :, None, :] * decay, kt)
        intra = jnp.einsum('bhts,bhse->bhte', scores, vt)
        state = jnp.exp(b_last[:, :, 0, :, None]) * state + jnp.einsum('bhsd,bhse->bhde', kt * jnp.exp(b_last - b), vt)
        return state, inter + intra

    s0 = jnp.zeros((B, H, dk, dv), jnp.float32)
    _, o = lax.scan(step, s0, (to_chunks(q), to_chunks(k), to_chunks(v), to_chunks(logf)))
    return jnp.moveaxis(o, 0, 2).reshape(B, H, S, dv)


def grouped_top2_moe(x, router_w, router_b, w1, w3, w2):
    B, S, D = x.shape
    T = B * S
    xt = x.reshape(T, D)
    logits = (xt @ router_w).astype(jnp.float32) + router_b.astype(jnp.float32)
    probs = jax.nn.softmax(logits, axis=-1)
    grouped = probs.reshape(T, N_GROUPS, EXPERTS_PER_GROUP)
    g_sel = jnp.argmax(grouped.max(-1), axis=-1)
    in_group = jnp.take_along_axis(grouped, g_sel[:, None, None], axis=1)[:, 0]
    top_p, top_local = lax.top_k(in_group, TOP_K)
    expert_idx = g_sel[:, None] * EXPERTS_PER_GROUP + top_local
    gate = top_p / top_p.sum(-1, keepdims=True)

    A = T * TOP_K
    flat_e = expert_idx.reshape(A)
    flat_tok = jnp.repeat(jnp.arange(T, dtype=jnp.int32), TOP_K)
    flat_gate = gate.reshape(A)
    order = jnp.argsort(flat_e)
    sorted_e = flat_e[order]
    counts = jnp.bincount(flat_e, length=N_EXPERTS)
    padded = (counts + MOE_BLOCK - 1) // MOE_BLOCK * MOE_BLOCK
    start = jnp.cumsum(counts) - counts
    pend = jnp.cumsum(padded)
    pstart = pend - padded
    dest = pstart[sorted_e] + (jnp.arange(A) - start[sorted_e])
    n_blocks = (A + N_EXPERTS * (MOE_BLOCK - 1) + MOE_BLOCK - 1) // MOE_BLOCK
    P = n_blocks * MOE_BLOCK
    row_tok = jnp.full((P,), T, jnp.int32).at[dest].set(flat_tok[order])
    row_gate = jnp.zeros((P,), jnp.float32).at[dest].set(flat_gate[order])
    block_e = jnp.minimum(jnp.searchsorted(pend, jnp.arange(n_blocks) * MOE_BLOCK, side='right'), N_EXPERTS - 1)
    x_pad = jnp.concatenate([xt, jnp.zeros((1, D), xt.dtype)], axis=0)
    xb = x_pad[row_tok].reshape(n_blocks, MOE_BLOCK, D)

    def expert_rows(args):
        xr, e = args
        h = jax.nn.silu(xr @ w1[e]) * (xr @ w3[e])
        return h @ w2[e]

    yb = lax.map(expert_rows, (xb, block_e)).reshape(P, D)
    y = jax.ops.segment_sum(yb * row_gate[:, None].astype(yb.dtype), row_tok, num_segments=T + 1)[:T]
    return y.reshape(B, S, D).astype(x.dtype)


def setup_inputs(seed: int = 0) -> dict:
    key = jax.random.key(seed)
    ks = jax.random.split(key, 24)
    nrm = jax.random.normal
    f32 = jnp.float32
    col_scale = jnp.concatenate([jnp.full((s,), BETA if i in VALUE_SLOTS else 1.0, f32) for i, s in enumerate(IN_SIZES)])
    fox_f_off = sum(IN_SIZES[:FOX_F_SLOT])
    b_in = 0.02 * nrm(ks[4], (DEPTH, N_IN), f32)
    b_in = b_in.at[:, fox_f_off:fox_f_off + FOX_HEADS].add(jnp.linspace(1.0, 4.0, FOX_HEADS))
    return {
        'x': nrm(ks[0], (BATCH, SEQ, D_MODEL), f32),
        'ln_in_g': 1.0 + 0.02 * nrm(ks[1], (D_MODEL,), f32),
        'ln_in_b': 0.02 * nrm(ks[2], (D_MODEL,), f32),
        'w_in': nrm(ks[3], (DEPTH, D_MODEL, N_IN), f32) * D_MODEL ** -0.5 * col_scale,
        'b_in': b_in,
        'w_fox_branch': nrm(ks[5], (DEPTH, FOX_WIDTH, D_MODEL), f32) * FOX_WIDTH ** -0.5 * BETA,
        'hgrn_lb_logits': 0.1 * nrm(ks[6], (DEPTH, HGRN_WIDTH), f32),
        'hgrn_norm_g': 1.0 + 0.02 * nrm(ks[7], (DEPTH, HGRN_WIDTH), f32),
        'w_hgrn_branch': nrm(ks[8], (DEPTH, HGRN_WIDTH, D_MODEL), f32) * HGRN_WIDTH ** -0.5 * BETA,
        'w_mix_out': nrm(ks[9], (DEPTH, D_MODEL, D_MODEL), f32) * D_MODEL ** -0.5 * BETA,
        'b_mix_out': 0.02 * nrm(ks[10], (DEPTH, D_MODEL), f32),
        'ln1_g': 1.0 + 0.02 * nrm(ks[11], (DEPTH, D_MODEL), f32),
        'ln1_b': 0.02 * nrm(ks[12], (DEPTH, D_MODEL), f32),
        'router_w': nrm(ks[13], (D_MODEL, N_EXPERTS), f32) * D_MODEL ** -0.5,
        'router_b': 0.01 * nrm(ks[14], (N_EXPERTS,), f32),
        'expert_w1': nrm(ks[15], (DEPTH, N_EXPERTS, D_MODEL, D_EXPERT), f32) * D_MODEL ** -0.5,
        'expert_w3': nrm(ks[16], (DEPTH, N_EXPERTS, D_MODEL, D_EXPERT), f32) * D_MODEL ** -0.5 * BETA,
        'expert_w2': nrm(ks[17], (DEPTH, N_EXPERTS, D_EXPERT, D_MODEL), f32) * D_EXPERT ** -0.5 * BETA,
        'ln2_g': 1.0 + 0.02 * nrm(ks[18], (DEPTH, D_MODEL), f32),
        'ln2_b': 0.02 * nrm(ks[19], (DEPTH, D_MODEL), f32),
    }


def reference(x, ln_in_g, ln_in_b, w_in, b_in, w_fox_branch, hgrn_lb_logits, hgrn_norm_g, w_hgrn_branch,
              w_mix_out, b_mix_out, ln1_g, ln1_b, router_w, router_b, expert_w1, expert_w3, expert_w2,
              ln2_g, ln2_b):
    f32 = jnp.float32
    lb_p = jax.nn.softmax(hgrn_lb_logits.astype(f32), axis=0)
    lb_all = jnp.cumsum(lb_p, axis=0) - lb_p[0]
    x = layer_norm(x, ln_in_g, ln_in_b)
    for l in range(DEPTH):
        h = x @ w_in[l] + b_in[l]
        fq, fk, fv, ff, hq, hf, hi, hg, gates = split_cols(h, IN_SIZES)
        fox_logf = jax.nn.log_sigmoid(ff.astype(f32)).transpose(0, 2, 1)
        fox_o = fox_attention(to_heads(fq, FOX_HEADS), to_heads(fk, FOX_HEADS), to_heads(fv, FOX_HEADS), fox_logf)
        y_fox = from_heads(fox_o) @ w_fox_branch[l]
        lb = lb_all[l].reshape(HGRN_HEADS, 1, HEAD_DIM)
        z = to_heads(hf, HGRN_HEADS).astype(f32)
        f_gate = lb + (1.0 - lb) * jax.nn.sigmoid(z)
        logf = jnp.log(f_gate)
        k_in = (1.0 - lb) * jax.nn.sigmoid(-z)
        q_h = jax.nn.silu(to_heads(hq, HGRN_HEADS).astype(f32))
        o = hgrn2_chunkwise(q_h, k_in, to_heads(hi, HGRN_HEADS).astype(f32), logf)
        o = o * lax.rsqrt(jnp.mean(jnp.square(o), -1, keepdims=True) + RMS_EPS) * hgrn_norm_g[l].astype(f32).reshape(HGRN_HEADS, 1, HEAD_DIM)
        o = (from_heads(o) * jax.nn.silu(hg.astype(f32))).astype(x.dtype)
        y_hgrn = o @ w_hgrn_branch[l]
        g = jax.nn.sigmoid(gates)
        mixed = (g[..., :D_MODEL] * y_fox + g[..., D_MODEL:] * y_hgrn) @ w_mix_out[l] + b_mix_out[l]
        x = layer_norm(ALPHA * x + mixed, ln1_g[l], ln1_b[l])
        y_moe = grouped_top2_moe(x, router_w, router_b, expert_w1[l], expert_w3[l], expert_w2[l])
        x = layer_norm(ALPHA * x + y_moe, ln2_g[l], ln2_b[l])
    return x
```

```python
import functools

import jax
import jax.numpy as jnp
from jax import lax
from jax.experimental import pallas as pl
from jax.experimental.pallas import tpu as pltpu

F32 = jnp.float32
BF16 = jnp.bfloat16
I32 = jnp.int32

LANES = 128
SUBLANES = 8
HEAD_DIM = 64
FOX_HEADS = 8
HGRN_HEADS = 8
HEADS_PER_VREG = LANES // HEAD_DIM
FOX_PAIRS = FOX_HEADS // HEADS_PER_VREG
HGRN_PAIRS = HGRN_HEADS // HEADS_PER_VREG
FOX_WIDTH = FOX_HEADS * HEAD_DIM
HGRN_WIDTH = HGRN_HEADS * HEAD_DIM
N_EXPERTS = 16
N_GROUPS = 4
EXPERTS_PER_GROUP = N_EXPERTS // N_GROUPS
TOP_K = 2
HGRN_CHUNK = 16
LN_EPS = 1e-5
RMS_EPS = 1e-6
MASK_VALUE = -1e30
VMEM_LIMIT = 56 * 1024 * 1024

_C_FQ = 0
_C_FK = _C_FQ + FOX_WIDTH
_C_FV = _C_FK + FOX_WIDTH
_C_HQ = _C_FV + FOX_WIDTH
_C_HI = _C_HQ + HGRN_WIDTH
_C_HG = _C_HI + HGRN_WIDTH
_C_GATES = _C_HG + HGRN_WIDTH


def _params(sem, vmem=VMEM_LIMIT):
    return pltpu.CompilerParams(dimension_semantics=sem, vmem_limit_bytes=vmem)


def _split3(x):
    hi = x.astype(BF16)
    r1 = x - hi.astype(F32)
    mid = r1.astype(BF16)
    lo = (r1 - mid.astype(F32)).astype(BF16)
    return hi, mid, lo


def _dot(a, b):
    return jnp.dot(a, b, preferred_element_type=F32)


def _dot_nt(a, b):
    return lax.dot_general(a, b, (((1,), (1,)), ((), ())), preferred_element_type=F32)


def _dot_tn(a, b):
    return lax.dot_general(a, b, (((0,), (0,)), ((), ())), preferred_element_type=F32)


def _dot_exact_lhs(m, x):
    hi, mid, lo = _split3(x)
    return _dot(m, hi) + _dot(m, mid) + _dot(m, lo)


def _dot_exact_rhs(x, m):
    hi, mid, lo = _split3(x)
    return _dot(hi, m) + _dot(mid, m) + _dot(lo, m)


def _sigmoid(x):
    return 1.0 / (1.0 + jnp.exp(-x))


def _layer_norm(x, g, b):
    mu = jnp.mean(x, axis=-1, keepdims=True)
    xc = x - mu
    var = jnp.mean(xc * xc, axis=-1, keepdims=True)
    return xc * lax.rsqrt(var + LN_EPS) * g + b


def _row_tile(n, want):
    t = min(n, want)
    assert n % t == 0, (n, t)
    return t


def _ln_kernel(x_ref, g_ref, b_ref, o_ref):
    o_ref[...] = _layer_norm(x_ref[...], g_ref[...], b_ref[...])


def _ln_in(x, g, b):
    T, D = x.shape
    tm = _row_tile(T, 512)
    return pl.pallas_call(
        _ln_kernel,
        grid=(T // tm,),
        in_specs=[pl.BlockSpec((tm, D), lambda i: (i, 0)),
                  pl.BlockSpec((1, D), lambda i: (0, 0)),
                  pl.BlockSpec((1, D), lambda i: (0, 0))],
        out_specs=pl.BlockSpec((tm, D), lambda i: (i, 0)),
        out_shape=jax.ShapeDtypeStruct((T, D), F32),
        compiler_params=_params(("parallel",)),
        name="ln_in",
    )(x, g.reshape(1, D), b.reshape(1, D))


def _inproj_kernel(x_ref, w_ref, b_ref, fq_ref, fk_ref, fv_ref, hq_ref, hi_ref, hg_ref,
                   gates_ref, hf_ref, ff_ref, *, d_model):
    xb = x_ref[...].astype(BF16)

    def proj(c0, n):
        return _dot(xb, w_ref[:, c0:c0 + n]) + b_ref[:, c0:c0 + n]

    scale = HEAD_DIM ** -0.5
    fq_ref[...] = (proj(_C_FQ, FOX_WIDTH) * scale).astype(BF16)
    fk_ref[...] = proj(_C_FK, FOX_WIDTH).astype(BF16)
    fv_ref[...] = proj(_C_FV, FOX_WIDTH).astype(BF16)
    hq_ref[...] = proj(_C_HQ, HGRN_WIDTH).astype(BF16)
    hi_ref[...] = proj(_C_HI, HGRN_WIDTH).astype(BF16)
    hg_ref[...] = proj(_C_HG, HGRN_WIDTH).astype(BF16)
    n_gate_chunks = 2 * d_model // 512
    for c in range(n_gate_chunks):
        gates_ref[:, c * 512:(c + 1) * 512] = proj(_C_GATES + c * 512, 512).astype(BF16)
    c_hf = _C_GATES + 2 * d_model
    hf_ref[...] = proj(c_hf, HGRN_WIDTH)
    ff_ref[...] = proj(c_hf + HGRN_WIDTH, LANES)


def _inproj(x, w, b, layer):
    T, D = x.shape
    n_all = w.shape[-1]
    tm = _row_tile(T, 512)
    row = lambda n: pl.BlockSpec((tm, n), lambda i: (i, 0))
    outs = [(FOX_WIDTH, BF16)] * 3 + [(HGRN_WIDTH, BF16)] * 3 + [(2 * D, BF16), (HGRN_WIDTH, F32), (LANES, F32)]
    return pl.pallas_call(
        functools.partial(_inproj_kernel, d_model=D),
        grid=(T // tm,),
        in_specs=[row(D),
                  pl.BlockSpec((None, D, n_all), lambda i: (layer, 0, 0), pipeline_mode=pl.Buffered(1)),
                  pl.BlockSpec((None, 1, n_all), lambda i: (layer, 0, 0), pipeline_mode=pl.Buffered(1))],
        out_specs=[row(n) for n, _ in outs],
        out_shape=[jax.ShapeDtypeStruct((T, n), dt) for n, dt in outs],
        compiler_params=_params(("parallel",)),
        name="inproj",
    )(x, w, b)


def _fox_kernel(q_ref, k_ref, v_ref, ff_ref, o_ref, kp_ref, vp_ref, f_ref, *, seq, tq, cb):
    i = pl.program_id(1)
    lane = lax.broadcasted_iota(I32, (1, LANES), 1)

    def head_lanes(h):
        hh = h % HEADS_PER_VREG
        own = (lane >= hh * HEAD_DIM) & (lane < (hh + 1) * HEAD_DIM)
        e0 = (1 - hh) * HEAD_DIM
        return own, e0

    @pl.when(i == 0)
    def _prologue():
        rr = lax.broadcasted_iota(I32, (cb, cb), 0)
        cc = lax.broadcasted_iota(I32, (cb, cb), 1)
        tri = (cc <= rr).astype(BF16)

        def cum_block(blk, carry):
            rows = pl.ds(pl.multiple_of(blk * cb, cb), cb)
            x = ff_ref[rows, :]
            lf = jnp.minimum(x, 0.0) - jnp.log(1.0 + jnp.exp(-jnp.abs(x)))
            fb = _dot_exact_lhs(tri, lf) + carry
            f_ref[rows, :] = fb
            for h in range(FOX_HEADS):
                p = h // HEADS_PER_VREG
                own, e0 = head_lanes(h)
                hi, mid, lo = _split3(fb[:, h:h + 1])
                ext = jnp.where((lane >= e0) & (lane < e0 + 3), 1.0, 0.0)
                ext = jnp.where(lane == e0 + 3, -hi.astype(F32), ext)
                ext = jnp.where(lane == e0 + 4, -mid.astype(F32), ext)
                ext = jnp.where(lane == e0 + 5, -lo.astype(F32), ext)
                kpair = k_ref[rows, p * LANES:(p + 1) * LANES]
                vpair = v_ref[rows, p * LANES:(p + 1) * LANES]
                kp_ref[h, rows, :] = jnp.where(own, kpair.astype(F32), ext).astype(BF16)
                vp_ref[h, rows, :] = jnp.where(own, vpair, jnp.zeros_like(vpair))
            return fb[cb - 1:cb, :]

        lax.fori_loop(0, seq // cb, cum_block, jnp.zeros((1, LANES), F32))

    q0 = pl.multiple_of(i * tq, tq)
    rowi = lax.broadcasted_iota(I32, (tq, tq), 0)
    coli = lax.broadcasted_iota(I32, (tq, tq), 1)
    causal = coli <= rowi
    for p in range(FOX_PAIRS):
        qpair = q_ref[:, p * LANES:(p + 1) * LANES].astype(F32)
        out_pair = jnp.zeros((tq, LANES), F32)
        for hh in range(HEADS_PER_VREG):
            h = p * HEADS_PER_VREG + hh
            own, e0 = head_lanes(h)
            hi, mid, lo = _split3(f_ref[pl.ds(q0, tq), h:h + 1])
            ext = jnp.where((lane >= e0 + 3) & (lane < e0 + 6), 1.0, 0.0)
            ext = jnp.where(lane == e0, hi.astype(F32), ext)
            ext = jnp.where(lane == e0 + 1, mid.astype(F32), ext)
            ext = jnp.where(lane == e0 + 2, lo.astype(F32), ext)
            qp = jnp.where(own, qpair, ext).astype(BF16)

            def step(j, carry, masked, qp=qp, h=h):
                m, l, acc = carry
                rows = pl.ds(pl.multiple_of(j * tq, tq), tq)
                s = _dot_nt(qp, kp_ref[h, rows, :])
                if masked:
                    s = jnp.where(causal, s, MASK_VALUE)
                m_new = jnp.maximum(m, jnp.max(s, axis=-1, keepdims=True))
                a = jnp.exp(m - m_new)
                pr = jnp.exp(s - m_new)
                l = a * l + jnp.sum(pr, axis=-1, keepdims=True)
                acc = a * acc + _dot(pr.astype(BF16), vp_ref[h, rows, :])
                return m_new, l, acc

            init = (jnp.full((tq, 1), MASK_VALUE, F32), jnp.zeros((tq, 1), F32),
                    jnp.zeros((tq, LANES), F32))
            carry = lax.fori_loop(0, i, functools.partial(step, masked=False), init)
            _, l, acc = step(i, carry, True)
            out_pair = out_pair + acc / l
        o_ref[:, p * LANES:(p + 1) * LANES] = out_pair.astype(BF16)


def _fox_attention(fq, fk, fv, ffp, batch, seq):
    T = fq.shape[0]
    tq = _row_tile(seq, 256)
    nq = seq // tq
    cb = tq
    return pl.pallas_call(
        functools.partial(_fox_kernel, seq=seq, tq=tq, cb=cb),
        grid=(batch, nq),
        in_specs=[pl.BlockSpec((tq, FOX_WIDTH), lambda b, i: (b * nq + i, 0)),
                  pl.BlockSpec((seq, FOX_WIDTH), lambda b, i: (b, 0)),
                  pl.BlockSpec((seq, FOX_WIDTH), lambda b, i: (b, 0)),
                  pl.BlockSpec((seq, LANES), lambda b, i: (b, 0))],
        out_specs=pl.BlockSpec((tq, FOX_WIDTH), lambda b, i: (b * nq + i, 0)),
        out_shape=jax.ShapeDtypeStruct((T, FOX_WIDTH), BF16),
        scratch_shapes=[pltpu.VMEM((FOX_HEADS, seq, LANES), BF16),
                        pltpu.VMEM((FOX_HEADS, seq, LANES), BF16),
                        pltpu.VMEM((seq, LANES), F32)],
        compiler_params=_params(("parallel", "arbitrary")),
        name="fox_attention",
    )(fq, fk, fv, ffp)


def _hgrn_kernel(hq_ref, hf_ref, hi_ref, hg_ref, lb_ref, ng_ref, o_ref, ss_ref, *, seq, rb):
    ch = HGRN_CHUNK
    rr = lax.broadcasted_iota(I32, (rb, rb), 0)
    cc = lax.broadcasted_iota(I32, (rb, rb), 1)
    same_chunk = (rr // ch) == (cc // ch)
    cum_mat = jnp.concatenate([(same_chunk & (cc <= rr)).astype(BF16), same_chunk.astype(BF16)], axis=0)
    hr = lax.broadcasted_iota(I32, (LANES, LANES), 0) // HEAD_DIM
    hc = lax.broadcasted_iota(I32, (LANES, LANES), 1) // HEAD_DIM
    head_ones = (hr == hc).astype(BF16)
    head_mask = (hr == hc).astype(F32)
    tmod = lax.broadcasted_iota(I32, (rb, LANES), 0) % ch
    lb = lb_ref[...]
    ng = ng_ref[...]
    ss_ref[...] = jnp.zeros((LANES, LANES), F32)

    def block(r, _):
        rows = pl.ds(pl.multiple_of(r * rb, rb), rb)
        z = hf_ref[rows, :]
        g = jnp.log(lb + (1.0 - lb) * _sigmoid(z))
        k = (1.0 - lb) * _sigmoid(-z)
        hq = hq_ref[rows, :].astype(F32)
        qs = hq * _sigmoid(hq)
        v = hi_ref[rows, :].astype(F32)
        cums = _dot_exact_lhs(cum_mat, g)
        b = cums[:rb]
        btot = cums[rb:]
        qd = (qs * jnp.exp(b)).astype(BF16)
        kd = (k * jnp.exp(btot - b)).astype(BF16)
        dec = jnp.exp(btot)

        acc = _dot((qs * k).astype(BF16), head_ones) * v
        for o in range(1, ch):
            bs = pltpu.roll(b, o, 0)
            ks = pltpu.roll(k, o, 0)
            vs = pltpu.roll(v, o, 0)
            term = jnp.where(tmod >= o, qs * jnp.exp(b - bs) * ks, 0.0)
            acc = acc + _dot(term.astype(BF16), head_ones) * vs

        ss = ss_ref[...]
        vb = v.astype(BF16)
        inter = []
        for n in range(rb // ch):
            sl = slice(n * ch, (n + 1) * ch)
            inter.append(_dot_nt(qd[sl], ss.astype(BF16)))
            upd = _dot_tn(vb[sl], kd[sl])
            ss = ss * dec[n * ch:n * ch + 1, :] + upd * head_mask
        ss_ref[...] = ss
        o_blk = acc + jnp.concatenate(inter, axis=0)
        ms = _dot_exact_rhs(o_blk * o_blk, head_ones) * (1.0 / HEAD_DIM)
        hg = hg_ref[rows, :].astype(F32)
        o_blk = o_blk * lax.rsqrt(ms + RMS_EPS) * ng * (hg * _sigmoid(hg))
        o_ref[rows, :] = o_blk.astype(BF16)
        return 0

    lax.fori_loop(0, seq // rb, block, 0)


def _hgrn(hq, hf, hi, hg, lb, ng, layer, batch, seq):
    T = hq.shape[0]
    rb = _row_tile(seq, 128)
    blk = lambda: pl.BlockSpec((seq, LANES), lambda b, p: (b, p))
    par = lambda: pl.BlockSpec((None, 1, LANES), lambda b, p: (layer * HGRN_PAIRS + p, 0, 0))
    return pl.pallas_call(
        functools.partial(_hgrn_kernel, seq=seq, rb=rb),
        grid=(batch, HGRN_PAIRS),
        in_specs=[blk(), blk(), blk(), blk(), par(), par()],
        out_specs=blk(),
        out_shape=jax.ShapeDtypeStruct((T, HGRN_WIDTH), BF16),
        scratch_shapes=[pltpu.VMEM((LANES, LANES), F32)],
        compiler_params=_params(("parallel", "parallel")),
        name="hgrn2",
    )(hq, hf, hi, hg, lb, ng)


def _merge_kernel(fo_ref, ho_ref, gates_ref, x_ref, wf_ref, wh_ref, wm_ref, bm_ref, g_ref, b_ref,
                  wrh_ref, wrl_ref, rb_ref, tri_ref, x1_ref, x1r_ref, route_ref, cnt_ref,
                  *, alpha, d_model):
    i = pl.program_id(0)
    tm = x_ref.shape[0]

    @pl.when(i == 0)
    def _():
        cnt_ref[...] = jnp.zeros(cnt_ref.shape, F32)

    y_fox = _dot(fo_ref[...], wf_ref[...])
    y_hgrn = _dot(ho_ref[...], wh_ref[...])
    g_fox = _sigmoid(gates_ref[:, :d_model].astype(F32))
    g_hgrn = _sigmoid(gates_ref[:, d_model:].astype(F32))
    mixed = _dot((g_fox * y_fox + g_hgrn * y_hgrn).astype(BF16), wm_ref[...]) + bm_ref[...]
    x1 = _layer_norm(alpha * x_ref[...] + mixed, g_ref[...], b_ref[...])
    x1_ref[...] = x1
    nj = d_model // LANES
    for j in range(nj):
        x1r_ref[pl.ds(j, tm, stride=nj), :] = x1[:, j * LANES:(j + 1) * LANES]

    xh = x1.astype(BF16)
    xl = (x1 - xh.astype(F32)).astype(BF16)
    logits = _dot(xh, wrh_ref[...]) + _dot(xl, wrh_ref[...]) + _dot(xh, wrl_ref[...]) + rb_ref[...]
    lane = lax.broadcasted_iota(I32, (tm, LANES), 1)
    lane_f = lane.astype(F32)
    neg = jnp.float32(-jnp.inf)
    lg = jnp.where(lane < N_EXPERTS, logits, neg)
    m1 = jnp.max(lg, axis=-1, keepdims=True)
    idx1 = jnp.min(jnp.where(lg == m1, lane_f, float(LANES)), axis=-1, keepdims=True).astype(I32)
    in_group = (lane // EXPERTS_PER_GROUP == idx1 // EXPERTS_PER_GROUP) & (lane < N_EXPERTS)
    lg2 = jnp.where(in_group & (lane != idx1), logits, neg)
    m2 = jnp.max(lg2, axis=-1, keepdims=True)
    idx2 = jnp.min(jnp.where(lg2 == m2, lane_f, float(LANES)), axis=-1, keepdims=True).astype(I32)
    e21 = jnp.exp(m2 - m1)
    gate1 = 1.0 / (1.0 + e21)
    gate2 = e21 / (1.0 + e21)

    oh1 = lane == idx1
    oh2 = lane == idx2
    oh = (oh1 | oh2).astype(F32)
    before = _dot(tri_ref[...], oh.astype(BF16)) + cnt_ref[0:1, :]
    rank1 = jnp.sum(jnp.where(oh1, before, 0.0), axis=-1, keepdims=True)
    rank2 = jnp.sum(jnp.where(oh2, before, 0.0), axis=-1, keepdims=True)
    cnt_ref[0:1, :] = cnt_ref[0:1, :] + jnp.sum(oh, axis=0, keepdims=True)

    route = jnp.where(lane == 0, idx1.astype(F32), 0.0)
    route = jnp.where(lane == 1, idx2.astype(F32), route)
    route = jnp.where(lane == 2, gate1, route)
    route = jnp.where(lane == 3, gate2, route)
    route = jnp.where(lane == 4, rank1, route)
    route = jnp.where(lane == 5, rank2, route)
    route_ref[...] = route


def _merge(fox_o, hgrn_o, gates, x, wf, wh, wm, bm, g, b, wrh, wrl, rbias, tri, layer, alpha):
    T, D = x.shape
    tm = tri.shape[0]
    nj = D // LANES
    row = lambda n: pl.BlockSpec((tm, n), lambda i: (i, 0))
    lw = lambda r, c: pl.BlockSpec((None, r, c), lambda i: (layer, 0, 0))
    cw = lambda r, c: pl.BlockSpec((r, c), lambda i: (0, 0))
    return pl.pallas_call(
        functools.partial(_merge_kernel, alpha=alpha, d_model=D),
        grid=(T // tm,),
        in_specs=[row(FOX_WIDTH), row(HGRN_WIDTH), row(2 * D), row(D),
                  lw(FOX_WIDTH, D), lw(HGRN_WIDTH, D), lw(D, D), lw(1, D), lw(1, D), lw(1, D),
                  cw(D, LANES), cw(D, LANES), cw(1, LANES), cw(tm, tm)],
        out_specs=[row(D), pl.BlockSpec((tm * nj, LANES), lambda i: (i, 0)), row(LANES),
                   pl.BlockSpec((SUBLANES, LANES), lambda i: (0, 0))],
        out_shape=[jax.ShapeDtypeStruct((T, D), F32), jax.ShapeDtypeStruct((T * nj, LANES), F32),
                   jax.ShapeDtypeStruct((T, LANES), F32), jax.ShapeDtypeStruct((SUBLANES, LANES), F32)],
        compiler_params=_params(("arbitrary",)),
        name="merge_router",
    )(fox_o, hgrn_o, gates, x, wf, wh, wm, bm, g, b, wrh, wrl, rbias, tri)


def _dispatch_kernel(dest_ref, x_hbm, xs_hbm, sem, *, tc, nj):
    i = pl.program_id(0)

    def row_copy(t, d):
        return pltpu.make_async_copy(
            x_hbm.at[pl.ds(pl.multiple_of(t * nj, nj), nj), :],
            xs_hbm.at[pl.ds(pl.multiple_of(d * nj, nj), nj), :], sem)

    def issue(r, _):
        t = i * tc + r
        row_copy(t, dest_ref[0, 0, 2 * r]).start()
        row_copy(t, dest_ref[0, 0, 2 * r + 1]).start()
        return 0

    lax.fori_loop(0, tc, issue, 0)

    def drain(r, _):
        row_copy(0, 0).wait()
        row_copy(0, 0).wait()
        return 0

    lax.fori_loop(0, tc, drain, 0)


def _dispatch(x1r, dest, n_tokens, nj):
    tc = _row_tile(n_tokens, 2048)
    n_steps = n_tokens // tc
    return pl.pallas_call(
        functools.partial(_dispatch_kernel, tc=tc, nj=nj),
        grid=(n_steps,),
        in_specs=[pl.BlockSpec((1, 1, 2 * tc), lambda i: (i, 0, 0), memory_space=pltpu.SMEM),
                  pl.BlockSpec(memory_space=pl.ANY)],
        out_specs=pl.BlockSpec(memory_space=pl.ANY),
        out_shape=jax.ShapeDtypeStruct((2 * n_tokens * nj, LANES), F32),
        scratch_shapes=[pltpu.SemaphoreType.DMA(())],
        compiler_params=pltpu.CompilerParams(dimension_semantics=("arbitrary",), has_side_effects=True),
        name="moe_dispatch",
    )(dest.reshape(n_steps, 1, 2 * tc), x1r)


def _experts_kernel(meta_ref, xs_ref, w1_ref, w3_ref, w2_ref, ys_ref, *, tm, nj):
    w = pl.program_id(0)
    lo = meta_ref[2, w]
    hi = meta_ref[3, w]
    first = meta_ref[4, w]

    @pl.when(hi > lo)
    def _():
        x = jnp.concatenate([xs_ref[pl.ds(j, tm, stride=nj), :] for j in range(nj)], axis=-1).astype(BF16)
        h1 = _dot(x, w1_ref[...].astype(BF16))
        h3 = _dot(x, w3_ref[...].astype(BF16))
        h = (h1 * _sigmoid(h1) * h3).astype(BF16)
        y = _dot(h, w2_ref[...].astype(BF16))
        rows = lax.broadcasted_iota(I32, (tm, 1), 0)
        mine = (rows >= lo) & (rows < hi)

        @pl.when(first == 1)
        def _():
            for j in range(nj):
                ys_ref[pl.ds(j, tm, stride=nj), :] = jnp.where(mine, y[:, j * LANES:(j + 1) * LANES], 0.0)

        @pl.when(first == 0)
        def _():
            for j in range(nj):
                cur = ys_ref[pl.ds(j, tm, stride=nj), :]
                ys_ref[pl.ds(j, tm, stride=nj), :] = jnp.where(mine, y[:, j * LANES:(j + 1) * LANES], cur)


def _experts(meta, xs, w1, w3, w2, layer, tm, nj):
    n_items = meta.shape[1]
    D = nj * LANES
    dh = w1.shape[-1]
    grid_spec = pltpu.PrefetchScalarGridSpec(
        num_scalar_prefetch=1,
        grid=(n_items,),
        in_specs=[pl.BlockSpec((tm * nj, LANES), lambda w, m: (m[0, w], 0)),
                  pl.BlockSpec((None, None, D, dh), lambda w, m: (layer, m[1, w], 0, 0)),
                  pl.BlockSpec((None, None, D, dh), lambda w, m: (layer, m[1, w], 0, 0)),
                  pl.BlockSpec((None, None, dh, D), lambda w, m: (layer, m[1, w], 0, 0))],
        out_specs=pl.BlockSpec((tm * nj, LANES), lambda w, m: (m[0, w], 0)),
    )
    return pl.pallas_call(
        functools.partial(_experts_kernel, tm=tm, nj=nj),
        grid_spec=grid_spec,
        out_shape=jax.ShapeDtypeStruct(xs.shape, F32),
        compiler_params=_params(("arbitrary",)),
        name="moe_experts",
    )(meta, xs, w1, w3, w2)


def _combine_kernel(dest_ref, ys_hbm, route_ref, x1_ref, g_ref, b_ref, o_ref, buf_ref, sem, *, tc, nj, alpha):
    def row_copy(d, slot, r):
        return pltpu.make_async_copy(
            ys_hbm.at[pl.ds(pl.multiple_of(d * nj, nj), nj), :],
            buf_ref.at[slot, pl.ds(pl.multiple_of(r * nj, nj), nj), :], sem)

    def issue(r, _):
        row_copy(dest_ref[0, 0, 2 * r], 0, r).start()
        row_copy(dest_ref[0, 0, 2 * r + 1], 1, r).start()
        return 0

    lax.fori_loop(0, tc, issue, 0)

    def drain(r, _):
        row_copy(0, 0, 0).wait()
        row_copy(0, 1, 0).wait()
        return 0

    lax.fori_loop(0, tc, drain, 0)

    gate1 = route_ref[:, 2:3]
    gate2 = route_ref[:, 3:4]
    y = jnp.concatenate(
        [gate1 * buf_ref[0, pl.ds(j, tc, stride=nj), :] + gate2 * buf_ref[1, pl.ds(j, tc, stride=nj), :]
         for j in range(nj)], axis=-1)
    o_ref[...] = _layer_norm(alpha * x1_ref[...] + y, g_ref[...], b_ref[...])


def _combine(dest, ys, route, x1, g, b, layer, alpha):
    T, D = x1.shape
    nj = D // LANES
    tc = _row_tile(T, 256)
    n_steps = T // tc
    row = lambda n: pl.BlockSpec((tc, n), lambda i: (i, 0))
    lw = lambda r, c: pl.BlockSpec((None, r, c), lambda i: (layer, 0, 0))
    return pl.pallas_call(
        functools.partial(_combine_kernel, tc=tc, nj=nj, alpha=alpha),
        grid=(n_steps,),
        in_specs=[pl.BlockSpec((1, 1, 2 * tc), lambda i: (i, 0, 0), memory_space=pltpu.SMEM),
                  pl.BlockSpec(memory_space=pl.ANY), row(LANES), row(D), lw(1, D), lw(1, D)],
        out_specs=row(D),
        out_shape=jax.ShapeDtypeStruct((T, D), F32),
        scratch_shapes=[pltpu.VMEM((2, tc * nj, LANES), F32), pltpu.SemaphoreType.DMA(())],
        compiler_params=_params(("arbitrary",)),
        name="moe_combine",
    )(dest.reshape(n_steps, 1, 2 * tc), ys, route, x1, g, b)


def _routing_tables(route, counts, tm, n_items):
    e = route[:, 0:2].astype(I32)
    rank = route[:, 4:6].astype(I32)
    cnt = counts[0, :N_EXPERTS].astype(I32)
    ends = jnp.cumsum(cnt)
    starts = ends - cnt
    dest = (starts[e] + rank).reshape(-1)

    first_tile = starts // tm
    last_tile = jnp.maximum(ends - 1, 0) // tm
    n_tiles_e = jnp.where(cnt > 0, last_tile - first_tile + 1, 0)
    item_end = jnp.cumsum(n_tiles_e)
    item_start = item_end - n_tiles_e
    n_real = item_end[-1]
    w = jnp.arange(n_items, dtype=I32)
    wc = jnp.minimum(w, n_real - 1)
    ex = jnp.searchsorted(item_end, wc, side="right").astype(I32)
    tile = first_tile[ex] + (wc - item_start[ex])
    lo = jnp.clip(starts[ex] - tile * tm, 0, tm)
    hi = jnp.clip(ends[ex] - tile * tm, 0, tm)
    real = w < n_real
    hi = jnp.where(real, hi, lo)
    prev_tile = jnp.concatenate([jnp.full((1,), -1, I32), tile[:-1]])
    first = (real & (tile != prev_tile)).astype(I32)
    meta = jnp.stack([tile, ex, lo, hi, first]).astype(I32)
    return dest, meta


def kernel(x, ln_in_g, ln_in_b, w_in, b_in, w_fox_branch, hgrn_lb_logits, hgrn_norm_g, w_hgrn_branch,
           w_mix_out, b_mix_out, ln1_g, ln1_b, router_w, router_b, expert_w1, expert_w3, expert_w2,
           ln2_g, ln2_b):
    batch, seq, D = x.shape
    depth = w_in.shape[0]
    T = batch * seq
    nj = D // LANES
    alpha = float((2 * depth) ** 0.25)
    assert D % 512 == 0 and seq % HGRN_CHUNK == 0

    sizes = (FOX_WIDTH, FOX_WIDTH, FOX_WIDTH, FOX_HEADS, HGRN_WIDTH, HGRN_WIDTH, HGRN_WIDTH, HGRN_WIDTH, 2 * D)
    offs = [0]
    for s in sizes:
        offs.append(offs[-1] + s)
    col = lambda a, i: a[..., offs[i]:offs[i + 1]]
    order = (0, 1, 2, 4, 6, 7, 8, 5)
    pad_ff = lambda a: jnp.pad(col(a, 3), [(0, 0)] * (a.ndim - 1) + [(0, LANES - FOX_HEADS)])
    w_all = jnp.concatenate([col(w_in, i) for i in order] + [pad_ff(w_in)], axis=-1).astype(BF16)
    b_all = jnp.concatenate([col(b_in, i) for i in order] + [pad_ff(b_in)], axis=-1).astype(F32)[:, None, :]

    lb_p = jax.nn.softmax(hgrn_lb_logits.astype(F32), axis=0)
    lb_all = (jnp.cumsum(lb_p, axis=0) - lb_p[0]).reshape(depth * HGRN_PAIRS, 1, LANES)
    ng_all = hgrn_norm_g.astype(F32).reshape(depth * HGRN_PAIRS, 1, LANES)

    wf = w_fox_branch.astype(BF16)
    wh = w_hgrn_branch.astype(BF16)
    wm = w_mix_out.astype(BF16)
    r3 = lambda a: a.astype(F32)[:, None, :]
    bm, g1, b1, g2, b2 = r3(b_mix_out), r3(ln1_g), r3(ln1_b), r3(ln2_g), r3(ln2_b)
    rw = jnp.pad(router_w.astype(F32), ((0, 0), (0, LANES - N_EXPERTS)))
    wrh = rw.astype(BF16)
    wrl = (rw - wrh.astype(F32)).astype(BF16)
    rbias = jnp.pad(router_b.astype(F32), (0, LANES - N_EXPERTS)).reshape(1, LANES)

    tm_merge = _row_tile(T, 512)
    tri = jnp.tril(jnp.ones((tm_merge, tm_merge), BF16), k=-1)
    tm_exp = _row_tile(2 * T, 512)
    n_items = (2 * T) // tm_exp + N_EXPERTS - 1

    xc = _ln_in(x.reshape(T, D), ln_in_g.astype(F32), ln_in_b.astype(F32))
    for l in range(depth):
        fq, fk, fv, hq, hi, hg, gates, hf, ffp = _inproj(xc, w_all, b_all, l)
        fox_o = _fox_attention(fq, fk, fv, ffp, batch, seq)
        hgrn_o = _hgrn(hq, hf, hi, hg, lb_all, ng_all, l, batch, seq)
        x1, x1r, route, counts = _merge(fox_o, hgrn_o, gates, xc, wf, wh, wm, bm, g1, b1,
                                        wrh, wrl, rbias, tri, l, alpha)
        dest, meta = _routing_tables(route, counts, tm_exp, n_items)
        xs = _dispatch(x1r, dest, T, nj)
        ys = _experts(meta, xs, expert_w1, expert_w3, expert_w2, l, tm_exp, nj)
        xc = _combine(dest, ys, route, x1, g2, b2, l, alpha)
    return xc.reshape(batch, seq, D)
```

```python
import functools

import jax
import jax.numpy as jnp
from jax import lax
from jax.experimental import pallas as pl
from jax.experimental.pallas import tpu as pltpu

F32 = jnp.float32
BF16 = jnp.bfloat16
I32 = jnp.int32

LANES = 128
SUBLANES = 8
HEAD_DIM = 64
FOX_HEADS = 8
HGRN_HEADS = 8
HEADS_PER_VREG = LANES // HEAD_DIM
FOX_PAIRS = FOX_HEADS // HEADS_PER_VREG
HGRN_PAIRS = HGRN_HEADS // HEADS_PER_VREG
FOX_WIDTH = FOX_HEADS * HEAD_DIM
HGRN_WIDTH = HGRN_HEADS * HEAD_DIM
N_EXPERTS = 16
N_GROUPS = 4
EXPERTS_PER_GROUP = N_EXPERTS // N_GROUPS
TOP_K = 2
HGRN_CHUNK = 16
LN_EPS = 1e-5
RMS_EPS = 1e-6
MASK_VALUE = -1e30
VMEM_LIMIT = 56 * 1024 * 1024

_C_FQ = 0
_C_FK = _C_FQ + FOX_WIDTH
_C_FV = _C_FK + FOX_WIDTH
_C_HQ = _C_FV + FOX_WIDTH
_C_HI = _C_HQ + HGRN_WIDTH
_C_HG = _C_HI + HGRN_WIDTH
_C_GATES = _C_HG + HGRN_WIDTH


def _params(sem, vmem=VMEM_LIMIT):
    return pltpu.CompilerParams(dimension_semantics=sem, vmem_limit_bytes=vmem)


def _split3(x):
    hi = x.astype(BF16)
    r1 = x - hi.astype(F32)
    mid = r1.astype(BF16)
    lo = (r1 - mid.astype(F32)).astype(BF16)
    return hi, mid, lo


def _dot(a, b):
    return jnp.dot(a, b, preferred_element_type=F32)


def _dot_nt(a, b):
    return lax.dot_general(a, b, (((1,), (1,)), ((), ())), preferred_element_type=F32)


def _dot_tn(a, b):
    return lax.dot_general(a, b, (((0,), (0,)), ((), ())), preferred_element_type=F32)


def _dot_exact_lhs(m, x):
    hi, mid, lo = _split3(x)
    return _dot(m, hi) + _dot(m, mid) + _dot(m, lo)


def _dot_exact_rhs(x, m):
    hi, mid, lo = _split3(x)
    return _dot(hi, m) + _dot(mid, m) + _dot(lo, m)


def _sigmoid(x):
    return 1.0 / (1.0 + jnp.exp(-x))


def _layer_norm(x, g, b):
    mu = jnp.mean(x, axis=-1, keepdims=True)
    xc = x - mu
    var = jnp.mean(xc * xc, axis=-1, keepdims=True)
    return xc * lax.rsqrt(var + LN_EPS) * g + b


def _row_tile(n, want):
    t = min(n, want)
    assert n % t == 0, (n, t)
    return t


def _ln_kernel(x_ref, g_ref, b_ref, o_ref):
    o_ref[...] = _layer_norm(x_ref[...], g_ref[...], b_ref[...])


def _ln_in(x, g, b):
    T, D = x.shape
    tm = _row_tile(T, 512)
    return pl.pallas_call(
        _ln_kernel,
        grid=(T // tm,),
        in_specs=[pl.BlockSpec((tm, D), lambda i: (i, 0)),
                  pl.BlockSpec((1, D), lambda i: (0, 0)),
                  pl.BlockSpec((1, D), lambda i: (0, 0))],
        out_specs=pl.BlockSpec((tm, D), lambda i: (i, 0)),
        out_shape=jax.ShapeDtypeStruct((T, D), F32),
        compiler_params=_params(("parallel",)),
        name="ln_in",
    )(x, g.reshape(1, D), b.reshape(1, D))


def _inproj_kernel(x_ref, w_ref, b_ref, fq_ref, fk_ref, fv_ref, hq_ref, hi_ref, hg_ref,
                   gates_ref, hf_ref, ff_ref, *, d_model):
    xb = x_ref[...].astype(BF16)

    def proj(c0, n):
        return _dot(xb, w_ref[:, c0:c0 + n]) + b_ref[:, c0:c0 + n]

    scale = HEAD_DIM ** -0.5
    fq_ref[...] = (proj(_C_FQ, FOX_WIDTH) * scale).astype(BF16)
    fk_ref[...] = proj(_C_FK, FOX_WIDTH).astype(BF16)
    fv_ref[...] = proj(_C_FV, FOX_WIDTH).astype(BF16)
    hq_ref[...] = proj(_C_HQ, HGRN_WIDTH).astype(BF16)
    hi_ref[...] = proj(_C_HI, HGRN_WIDTH).astype(BF16)
    hg_ref[...] = proj(_C_HG, HGRN_WIDTH).astype(BF16)
    n_gate_chunks = 2 * d_model // 512
    for c in range(n_gate_chunks):
        gates_ref[:, c * 512:(c + 1) * 512] = proj(_C_GATES + c * 512, 512).astype(BF16)
    c_hf = _C_GATES + 2 * d_model
    hf_ref[...] = proj(c_hf, HGRN_WIDTH)
    ff_ref[...] = proj(c_hf + HGRN_WIDTH, LANES)


def _inproj(x, w, b, layer):
    T, D = x.shape
    n_all = w.shape[-1]
    tm = _row_tile(T, 512)
    row = lambda n: pl.BlockSpec((tm, n), lambda i: (i, 0))
    outs = [(FOX_WIDTH, BF16)] * 3 + [(HGRN_WIDTH, BF16)] * 3 + [(2 * D, BF16), (HGRN_WIDTH, F32), (LANES, F32)]
    return pl.pallas_call(
        functools.partial(_inproj_kernel, d_model=D),
        grid=(T // tm,),
        in_specs=[row(D),
                  pl.BlockSpec((None, D, n_all), lambda i: (layer, 0, 0), pipeline_mode=pl.Buffered(1)),
                  pl.BlockSpec((None, 1, n_all), lambda i: (layer, 0, 0), pipeline_mode=pl.Buffered(1))],
        out_specs=[row(n) for n, _ in outs],
        out_shape=[jax.ShapeDtypeStruct((T, n), dt) for n, dt in outs],
        compiler_params=_params(("parallel",)),
        name="inproj",
    )(x, w, b)


def _fox_kernel(q_ref, k_ref, v_ref, ff_ref, o_ref, kp_ref, vp_ref, f_ref, qp_ref, m_ref, l_ref, acc_ref,
                *, seq, tq, cb):
    i = pl.program_id(1)
    lane = lax.broadcasted_iota(I32, (1, LANES), 1)

    def head_lanes(h):
        hh = h % HEADS_PER_VREG
        own = (lane >= hh * HEAD_DIM) & (lane < (hh + 1) * HEAD_DIM)
        e0 = (1 - hh) * HEAD_DIM
        return own, e0

    @pl.when(i == 0)
    def _prologue():
        rr = lax.broadcasted_iota(I32, (cb, cb), 0)
        cc = lax.broadcasted_iota(I32, (cb, cb), 1)
        tri = (cc <= rr).astype(BF16)

        def cum_block(blk, carry):
            rows = pl.ds(pl.multiple_of(blk * cb, cb), cb)
            x = ff_ref[rows, :]
            lf = jnp.minimum(x, 0.0) - jnp.log(1.0 + jnp.exp(-jnp.abs(x)))
            fb = _dot_exact_lhs(tri, lf) + carry
            f_ref[rows, :] = fb
            for h in range(FOX_HEADS):
                p = h // HEADS_PER_VREG
                own, e0 = head_lanes(h)
                hi, mid, lo = _split3(fb[:, h:h + 1])
                ext = jnp.where((lane >= e0) & (lane < e0 + 3), 1.0, 0.0)
                ext = jnp.where(lane == e0 + 3, -hi.astype(F32), ext)
                ext = jnp.where(lane == e0 + 4, -mid.astype(F32), ext)
                ext = jnp.where(lane == e0 + 5, -lo.astype(F32), ext)
                kpair = k_ref[rows, p * LANES:(p + 1) * LANES]
                vpair = v_ref[rows, p * LANES:(p + 1) * LANES]
                kp_ref[h, rows, :] = jnp.where(own, kpair.astype(F32), ext).astype(BF16)
                vp_ref[h, rows, :] = jnp.where(own, vpair, jnp.zeros_like(vpair))
            return fb[cb - 1:cb, :]

        lax.fori_loop(0, seq // cb, cum_block, jnp.zeros((1, LANES), F32))

    q0 = pl.multiple_of(i * tq, tq)
    rowi = lax.broadcasted_iota(I32, (tq, tq), 0)
    coli = lax.broadcasted_iota(I32, (tq, tq), 1)
    causal = coli <= rowi
    for h in range(FOX_HEADS):
        p = h // HEADS_PER_VREG
        own, e0 = head_lanes(h)
        hi, mid, lo = _split3(f_ref[pl.ds(q0, tq), h:h + 1])
        ext = jnp.where((lane >= e0 + 3) & (lane < e0 + 6), 1.0, 0.0)
        ext = jnp.where(lane == e0, hi.astype(F32), ext)
        ext = jnp.where(lane == e0 + 1, mid.astype(F32), ext)
        ext = jnp.where(lane == e0 + 2, lo.astype(F32), ext)
        qpair = q_ref[:, p * LANES:(p + 1) * LANES].astype(F32)
        qp_ref[h] = jnp.where(own, qpair, ext).astype(BF16)
        m_ref[h] = jnp.full((tq, 1), MASK_VALUE, F32)
        l_ref[h] = jnp.zeros((tq, 1), F32)
        acc_ref[h] = jnp.zeros((tq, LANES), F32)

    def step(j, masked):
        rows = pl.ds(pl.multiple_of(j * tq, tq), tq)
        for h in range(FOX_HEADS):
            s = _dot_nt(qp_ref[h], kp_ref[h, rows, :])
            if masked:
                s = jnp.where(causal, s, MASK_VALUE)
            m = m_ref[h]
            m_new = jnp.maximum(m, jnp.max(s, axis=-1, keepdims=True))
            a = jnp.exp(m - m_new)
            pr = jnp.exp(s - m_new)
            m_ref[h] = m_new
            l_ref[h] = a * l_ref[h] + jnp.sum(pr, axis=-1, keepdims=True)
            acc_ref[h] = a * acc_ref[h] + _dot(pr.astype(BF16), vp_ref[h, rows, :])

    def off_diagonal(j, _):
        step(j, False)
        return 0

    lax.fori_loop(0, i, off_diagonal, 0)
    step(i, True)
    for p in range(FOX_PAIRS):
        h0 = p * HEADS_PER_VREG
        out_pair = acc_ref[h0] / l_ref[h0]
        for h in range(h0 + 1, h0 + HEADS_PER_VREG):
            out_pair = out_pair + acc_ref[h] / l_ref[h]
        o_ref[:, p * LANES:(p + 1) * LANES] = out_pair.astype(BF16)


def _fox_attention(fq, fk, fv, ffp, batch, seq):
    T = fq.shape[0]
    tq = _row_tile(seq, 256)
    nq = seq // tq
    cb = tq
    return pl.pallas_call(
        functools.partial(_fox_kernel, seq=seq, tq=tq, cb=cb),
        grid=(batch, nq),
        in_specs=[pl.BlockSpec((tq, FOX_WIDTH), lambda b, i: (b * nq + i, 0)),
                  pl.BlockSpec((seq, FOX_WIDTH), lambda b, i: (b, 0)),
                  pl.BlockSpec((seq, FOX_WIDTH), lambda b, i: (b, 0)),
                  pl.BlockSpec((seq, LANES), lambda b, i: (b, 0))],
        out_specs=pl.BlockSpec((tq, FOX_WIDTH), lambda b, i: (b * nq + i, 0)),
        out_shape=jax.ShapeDtypeStruct((T, FOX_WIDTH), BF16),
        scratch_shapes=[pltpu.VMEM((FOX_HEADS, seq, LANES), BF16),
                        pltpu.VMEM((FOX_HEADS, seq, LANES), BF16),
                        pltpu.VMEM((seq, LANES), F32),
                        pltpu.VMEM((FOX_HEADS, tq, LANES), BF16),
                        pltpu.VMEM((FOX_HEADS, tq, 1), F32),
                        pltpu.VMEM((FOX_HEADS, tq, 1), F32),
                        pltpu.VMEM((FOX_HEADS, tq, LANES), F32)],
        compiler_params=_params(("parallel", "arbitrary")),
        name="fox_attention",
    )(fq, fk, fv, ffp)


def _hgrn_kernel(hq_ref, hf_ref, hi_ref, hg_ref, lb_ref, ng_ref, o_ref, ss_ref, *, seq, rb):
    ch = HGRN_CHUNK
    rr = lax.broadcasted_iota(I32, (rb, rb), 0)
    cc = lax.broadcasted_iota(I32, (rb, rb), 1)
    same_chunk = (rr // ch) == (cc // ch)
    cum_mat = jnp.concatenate([(same_chunk & (cc <= rr)).astype(BF16), same_chunk.astype(BF16)], axis=0)
    hr = lax.broadcasted_iota(I32, (LANES, LANES), 0) // HEAD_DIM
    hc = lax.broadcasted_iota(I32, (LANES, LANES), 1) // HEAD_DIM
    head_ones = (hr == hc).astype(BF16)
    head_mask = (hr == hc).astype(F32)
    tmod = lax.broadcasted_iota(I32, (rb, LANES), 0) % ch
    lb = lb_ref[...]
    ng = ng_ref[...]
    ss_ref[...] = jnp.zeros((LANES, LANES), F32)

    def block(r, _):
        rows = pl.ds(pl.multiple_of(r * rb, rb), rb)
        z = hf_ref[rows, :]
        g = jnp.log(lb + (1.0 - lb) * _sigmoid(z))
        k = (1.0 - lb) * _sigmoid(-z)
        hq = hq_ref[rows, :].astype(F32)
        qs = hq * _sigmoid(hq)
        v = hi_ref[rows, :].astype(F32)
        cums = _dot_exact_lhs(cum_mat, g)
        b = cums[:rb]
        btot = cums[rb:]
        qd = (qs * jnp.exp(b)).astype(BF16)
        kd = (k * jnp.exp(btot - b)).astype(BF16)
        dec = jnp.exp(btot)

        acc = _dot((qs * k).astype(BF16), head_ones) * v
        for o in range(1, ch):
            bs = pltpu.roll(b, o, 0)
            ks = pltpu.roll(k, o, 0)
            vs = pltpu.roll(v, o, 0)
            term = jnp.where(tmod >= o, qs * jnp.exp(b - bs) * ks, 0.0)
            acc = acc + _dot(term.astype(BF16), head_ones) * vs

        vb = v.astype(BF16)
        chunks = [slice(n * ch, (n + 1) * ch) for n in range(rb // ch)]
        upds = [_dot_tn(vb[sl], kd[sl]) * head_mask for sl in chunks]
        ss = ss_ref[...]
        states = []
        for n, sl in enumerate(chunks):
            states.append(ss.astype(BF16))
            ss = ss * dec[n * ch:n * ch + 1, :] + upds[n]
        ss_ref[...] = ss
        inter = [_dot_nt(qd[sl], st) for sl, st in zip(chunks, states)]
        o_blk = acc + jnp.concatenate(inter, axis=0)
        ms = _dot_exact_rhs(o_blk * o_blk, head_ones) * (1.0 / HEAD_DIM)
        hg = hg_ref[rows, :].astype(F32)
        o_blk = o_blk * lax.rsqrt(ms + RMS_EPS) * ng * (hg * _sigmoid(hg))
        o_ref[rows, :] = o_blk.astype(BF16)
        return 0

    lax.fori_loop(0, seq // rb, block, 0)


def _hgrn(hq, hf, hi, hg, lb, ng, layer, batch, seq):
    T = hq.shape[0]
    rb = _row_tile(seq, 128)
    blk = lambda: pl.BlockSpec((seq, LANES), lambda b, p: (b, p))
    par = lambda: pl.BlockSpec((None, 1, LANES), lambda b, p: (layer * HGRN_PAIRS + p, 0, 0))
    return pl.pallas_call(
        functools.partial(_hgrn_kernel, seq=seq, rb=rb),
        grid=(batch, HGRN_PAIRS),
        in_specs=[blk(), blk(), blk(), blk(), par(), par()],
        out_specs=blk(),
        out_shape=jax.ShapeDtypeStruct((T, HGRN_WIDTH), BF16),
        scratch_shapes=[pltpu.VMEM((LANES, LANES), F32)],
        compiler_params=_params(("parallel", "parallel")),
        name="hgrn2",
    )(hq, hf, hi, hg, lb, ng)


def _merge_kernel(fo_ref, ho_ref, gates_ref, x_ref, wf_ref, wh_ref, wm_ref, bm_ref, g_ref, b_ref,
                  wrh_ref, wrl_ref, rb_ref, tri_ref, x1_ref, x1r_ref, route_ref, cnt_ref,
                  *, alpha, d_model):
    i = pl.program_id(0)
    tm = x_ref.shape[0]

    @pl.when(i == 0)
    def _():
        cnt_ref[...] = jnp.zeros(cnt_ref.shape, F32)

    y_fox = _dot(fo_ref[...], wf_ref[...])
    y_hgrn = _dot(ho_ref[...], wh_ref[...])
    g_fox = _sigmoid(gates_ref[:, :d_model].astype(F32))
    g_hgrn = _sigmoid(gates_ref[:, d_model:].astype(F32))
    mixed = _dot((g_fox * y_fox + g_hgrn * y_hgrn).astype(BF16), wm_ref[...]) + bm_ref[...]
    x1 = _layer_norm(alpha * x_ref[...] + mixed, g_ref[...], b_ref[...])
    x1_ref[...] = x1
    nj = d_model // LANES
    for j in range(nj):
        x1r_ref[pl.ds(j, tm, stride=nj), :] = x1[:, j * LANES:(j + 1) * LANES]

    xh = x1.astype(BF16)
    xl = (x1 - xh.astype(F32)).astype(BF16)
    logits = _dot(xh, wrh_ref[...]) + _dot(xl, wrh_ref[...]) + _dot(xh, wrl_ref[...]) + rb_ref[...]
    lane = lax.broadcasted_iota(I32, (tm, LANES), 1)
    lane_f = lane.astype(F32)
    neg = jnp.float32(-jnp.inf)
    lg = jnp.where(lane < N_EXPERTS, logits, neg)
    m1 = jnp.max(lg, axis=-1, keepdims=True)
    idx1 = jnp.min(jnp.where(lg == m1, lane_f, float(LANES)), axis=-1, keepdims=True).astype(I32)
    in_group = (lane // EXPERTS_PER_GROUP == idx1 // EXPERTS_PER_GROUP) & (lane < N_EXPERTS)
    lg2 = jnp.where(in_group & (lane != idx1), logits, neg)
    m2 = jnp.max(lg2, axis=-1, keepdims=True)
    idx2 = jnp.min(jnp.where(lg2 == m2, lane_f, float(LANES)), axis=-1, keepdims=True).astype(I32)
    e21 = jnp.exp(m2 - m1)
    gate1 = 1.0 / (1.0 + e21)
    gate2 = e21 / (1.0 + e21)

    oh1 = lane == idx1
    oh2 = lane == idx2
    oh = (oh1 | oh2).astype(F32)
    before = _dot(tri_ref[...], oh.astype(BF16)) + cnt_ref[0:1, :]
    rank1 = jnp.sum(jnp.where(oh1, before, 0.0), axis=-1, keepdims=True)
    rank2 = jnp.sum(jnp.where(oh2, before, 0.0), axis=-1, keepdims=True)
    cnt_ref[0:1, :] = cnt_ref[0:1, :] + jnp.sum(oh, axis=0, keepdims=True)

    route = jnp.where(lane == 0, idx1.astype(F32), 0.0)
    route = jnp.where(lane == 1, idx2.astype(F32), route)
    route = jnp.where(lane == 2, gate1, route)
    route = jnp.where(lane == 3, gate2, route)
    route = jnp.where(lane == 4, rank1, route)
    route = jnp.where(lane == 5, rank2, route)
    route_ref[...] = route


def _merge(fox_o, hgrn_o, gates, x, wf, wh, wm, bm, g, b, wrh, wrl, rbias, tri, layer, alpha):
    T, D = x.shape
    tm = tri.shape[0]
    nj = D // LANES
    row = lambda n: pl.BlockSpec((tm, n), lambda i: (i, 0))
    lw = lambda r, c: pl.BlockSpec((None, r, c), lambda i: (layer, 0, 0))
    cw = lambda r, c: pl.BlockSpec((r, c), lambda i: (0, 0))
    return pl.pallas_call(
        functools.partial(_merge_kernel, alpha=alpha, d_model=D),
        grid=(T // tm,),
        in_specs=[row(FOX_WIDTH), row(HGRN_WIDTH), row(2 * D), row(D),
                  lw(FOX_WIDTH, D), lw(HGRN_WIDTH, D), lw(D, D), lw(1, D), lw(1, D), lw(1, D),
                  cw(D, LANES), cw(D, LANES), cw(1, LANES), cw(tm, tm)],
        out_specs=[row(D), pl.BlockSpec((tm * nj, LANES), lambda i: (i, 0)), row(LANES),
                   pl.BlockSpec((SUBLANES, LANES), lambda i: (0, 0))],
        out_shape=[jax.ShapeDtypeStruct((T, D), F32), jax.ShapeDtypeStruct((T * nj, LANES), F32),
                   jax.ShapeDtypeStruct((T, LANES), F32), jax.ShapeDtypeStruct((SUBLANES, LANES), F32)],
        compiler_params=_params(("arbitrary",)),
        name="merge_router",
    )(fox_o, hgrn_o, gates, x, wf, wh, wm, bm, g, b, wrh, wrl, rbias, tri)


def _dispatch_kernel(dest_ref, x_ref, xs_hbm, sem, *, tc, nj):
    def row_copy(r, d):
        return pltpu.make_async_copy(
            x_ref.at[pl.ds(pl.multiple_of(r * nj, nj), nj), :],
            xs_hbm.at[pl.ds(pl.multiple_of(d * nj, nj), nj), :], sem)

    def issue(r, _):
        row_copy(r, dest_ref[0, 0, 2 * r]).start()
        row_copy(r, dest_ref[0, 0, 2 * r + 1]).start()
        return 0

    lax.fori_loop(0, tc, issue, 0, unroll=8)

    def drain(r, _):
        row_copy(0, 0).wait()
        row_copy(0, 0).wait()
        return 0

    lax.fori_loop(0, tc, drain, 0, unroll=8)


def _dispatch(x1r, dest, n_tokens, nj):
    tc = _row_tile(n_tokens, 512)
    n_steps = n_tokens // tc
    return pl.pallas_call(
        functools.partial(_dispatch_kernel, tc=tc, nj=nj),
        grid=(n_steps,),
        in_specs=[pl.BlockSpec((1, 1, 2 * tc), lambda i: (i, 0, 0), memory_space=pltpu.SMEM),
                  pl.BlockSpec((tc * nj, LANES), lambda i: (i, 0))],
        out_specs=pl.BlockSpec(memory_space=pl.ANY),
        out_shape=jax.ShapeDtypeStruct((2 * n_tokens * nj, LANES), F32),
        scratch_shapes=[pltpu.SemaphoreType.DMA(())],
        compiler_params=pltpu.CompilerParams(dimension_semantics=("arbitrary",), has_side_effects=True),
        name="moe_dispatch",
    )(dest.reshape(n_steps, 1, 2 * tc), x1r)


def _experts_kernel(meta_ref, xs_ref, w1_ref, w3_ref, w2_ref, ys_ref, *, tm, nj):
    w = pl.program_id(0)
    lo = meta_ref[2, w]
    hi = meta_ref[3, w]
    first = meta_ref[4, w]

    @pl.when(hi > lo)
    def _():
        x = jnp.concatenate([xs_ref[pl.ds(j, tm, stride=nj), :] for j in range(nj)], axis=-1).astype(BF16)
        h1 = _dot(x, w1_ref[...].astype(BF16))
        h3 = _dot(x, w3_ref[...].astype(BF16))
        h = (h1 * _sigmoid(h1) * h3).astype(BF16)
        y = _dot(h, w2_ref[...].astype(BF16))
        rows = lax.broadcasted_iota(I32, (tm, 1), 0)
        mine = (rows >= lo) & (rows < hi)

        @pl.when(first == 1)
        def _():
            for j in range(nj):
                ys_ref[pl.ds(j, tm, stride=nj), :] = jnp.where(mine, y[:, j * LANES:(j + 1) * LANES], 0.0)

        @pl.when(first == 0)
        def _():
            for j in range(nj):
                cur = ys_ref[pl.ds(j, tm, stride=nj), :]
                ys_ref[pl.ds(j, tm, stride=nj), :] = jnp.where(mine, y[:, j * LANES:(j + 1) * LANES], cur)


def _experts(meta, xs, w1, w3, w2, layer, tm, nj):
    n_items = meta.shape[1]
    D = nj * LANES
    dh = w1.shape[-1]
    grid_spec = pltpu.PrefetchScalarGridSpec(
        num_scalar_prefetch=1,
        grid=(n_items,),
        in_specs=[pl.BlockSpec((tm * nj, LANES), lambda w, m: (m[0, w], 0)),
                  pl.BlockSpec((None, None, D, dh), lambda w, m: (layer, m[1, w], 0, 0)),
                  pl.BlockSpec((None, None, D, dh), lambda w, m: (layer, m[1, w], 0, 0)),
                  pl.BlockSpec((None, None, dh, D), lambda w, m: (layer, m[1, w], 0, 0))],
        out_specs=pl.BlockSpec((tm * nj, LANES), lambda w, m: (m[0, w], 0)),
    )
    return pl.pallas_call(
        functools.partial(_experts_kernel, tm=tm, nj=nj),
        grid_spec=grid_spec,
        out_shape=jax.ShapeDtypeStruct(xs.shape, F32),
        compiler_params=_params(("arbitrary",)),
        name="moe_experts",
    )(meta, xs, w1, w3, w2)


def _combine_kernel(dest_ref, ys_hbm, route_ref, x1_ref, g_ref, b_ref, o_ref, buf_ref, sem, *, tc, nj, alpha):
    def row_copy(d, slot, r):
        return pltpu.make_async_copy(
            ys_hbm.at[pl.ds(pl.multiple_of(d * nj, nj), nj), :],
            buf_ref.at[slot, pl.ds(pl.multiple_of(r * nj, nj), nj), :], sem)

    def issue(r, _):
        row_copy(dest_ref[0, 0, 2 * r], 0, r).start()
        row_copy(dest_ref[0, 0, 2 * r + 1], 1, r).start()
        return 0

    lax.fori_loop(0, tc, issue, 0, unroll=8)

    def drain(r, _):
        row_copy(0, 0, 0).wait()
        row_copy(0, 1, 0).wait()
        return 0

    lax.fori_loop(0, tc, drain, 0, unroll=8)

    gate1 = route_ref[:, 2:3]
    gate2 = route_ref[:, 3:4]
    y = jnp.concatenate(
        [gate1 * buf_ref[0, pl.ds(j, tc, stride=nj), :] + gate2 * buf_ref[1, pl.ds(j, tc, stride=nj), :]
         for j in range(nj)], axis=-1)
    o_ref[...] = _layer_norm(alpha * x1_ref[...] + y, g_ref[...], b_ref[...])


def _combine(dest, ys, route, x1, g, b, layer, alpha):
    T, D = x1.shape
    nj = D // LANES
    tc = _row_tile(T, 256)
    n_steps = T // tc
    row = lambda n: pl.BlockSpec((tc, n), lambda i: (i, 0))
    lw = lambda r, c: pl.BlockSpec((None, r, c), lambda i: (layer, 0, 0))
    return pl.pallas_call(
        functools.partial(_combine_kernel, tc=tc, nj=nj, alpha=alpha),
        grid=(n_steps,),
        in_specs=[pl.BlockSpec((1, 1, 2 * tc), lambda i: (i, 0, 0), memory_space=pltpu.SMEM),
                  pl.BlockSpec(memory_space=pl.ANY), row(LANES), row(D), lw(1, D), lw(1, D)],
        out_specs=row(D),
        out_shape=jax.ShapeDtypeStruct((T, D), F32),
        scratch_shapes=[pltpu.VMEM((2, tc * nj, LANES), F32), pltpu.SemaphoreType.DMA(())],
        compiler_params=_params(("arbitrary",)),
        name="moe_combine",
    )(dest.reshape(n_steps, 1, 2 * tc), ys, route, x1, g, b)


def _routing_tables(route, counts, tm, n_items):
    e = route[:, 0:2].astype(I32)
    rank = route[:, 4:6].astype(I32)
    cnt = counts[0, :N_EXPERTS].astype(I32)
    ends = jnp.cumsum(cnt)
    starts = ends - cnt
    dest = (starts[e] + rank).reshape(-1)

    first_tile = starts // tm
    last_tile = jnp.maximum(ends - 1, 0) // tm
    n_tiles_e = jnp.where(cnt > 0, last_tile - first_tile + 1, 0)
    item_end = jnp.cumsum(n_tiles_e)
    item_start = item_end - n_tiles_e
    n_real = item_end[-1]
    w = jnp.arange(n_items, dtype=I32)
    wc = jnp.minimum(w, n_real - 1)
    ex = jnp.sum((item_end[None, :] <= wc[:, None]).astype(I32), axis=1)
    tile = first_tile[ex] + (wc - item_start[ex])
    lo = jnp.clip(starts[ex] - tile * tm, 0, tm)
    hi = jnp.clip(ends[ex] - tile * tm, 0, tm)
    real = w < n_real
    hi = jnp.where(real, hi, lo)
    prev_tile = jnp.concatenate([jnp.full((1,), -1, I32), tile[:-1]])
    first = (real & (tile != prev_tile)).astype(I32)
    meta = jnp.stack([tile, ex, lo, hi, first]).astype(I32)
    return dest, meta


def kernel(x, ln_in_g, ln_in_b, w_in, b_in, w_fox_branch, hgrn_lb_logits, hgrn_norm_g, w_hgrn_branch,
           w_mix_out, b_mix_out, ln1_g, ln1_b, router_w, router_b, expert_w1, expert_w3, expert_w2,
           ln2_g, ln2_b):
    batch, seq, D = x.shape
    depth = w_in.shape[0]
    T = batch * seq
    nj = D // LANES
    alpha = float((2 * depth) ** 0.25)
    assert D % 512 == 0 and seq % HGRN_CHUNK == 0

    sizes = (FOX_WIDTH, FOX_WIDTH, FOX_WIDTH, FOX_HEADS, HGRN_WIDTH, HGRN_WIDTH, HGRN_WIDTH, HGRN_WIDTH, 2 * D)
    offs = [0]
    for s in sizes:
        offs.append(offs[-1] + s)
    col = lambda a, i: a[..., offs[i]:offs[i + 1]]
    order = (0, 1, 2, 4, 6, 7, 8, 5)
    pad_ff = lambda a: jnp.pad(col(a, 3), [(0, 0)] * (a.ndim - 1) + [(0, LANES - FOX_HEADS)])
    w_all = jnp.concatenate([col(w_in, i) for i in order] + [pad_ff(w_in)], axis=-1).astype(BF16)
    b_all = jnp.concatenate([col(b_in, i) for i in order] + [pad_ff(b_in)], axis=-1).astype(F32)[:, None, :]

    lb_p = jax.nn.softmax(hgrn_lb_logits.astype(F32), axis=0)
    lb_all = (jnp.cumsum(lb_p, axis=0) - lb_p[0]).reshape(depth * HGRN_PAIRS, 1, LANES)
    ng_all = hgrn_norm_g.astype(F32).reshape(depth * HGRN_PAIRS, 1, LANES)

    wf = w_fox_branch.astype(BF16)
    wh = w_hgrn_branch.astype(BF16)
    wm = w_mix_out.astype(BF16)
    r3 = lambda a: a.astype(F32)[:, None, :]
    bm, g1, b1, g2, b2 = r3(b_mix_out), r3(ln1_g), r3(ln1_b), r3(ln2_g), r3(ln2_b)
    rw = jnp.pad(router_w.astype(F32), ((0, 0), (0, LANES - N_EXPERTS)))
    wrh = rw.astype(BF16)
    wrl = (rw - wrh.astype(F32)).astype(BF16)
    rbias = jnp.pad(router_b.astype(F32), (0, LANES - N_EXPERTS)).reshape(1, LANES)

    tm_merge = _row_tile(T, 512)
    tri = jnp.tril(jnp.ones((tm_merge, tm_merge), BF16), k=-1)
    tm_exp = _row_tile(2 * T, 512)
    n_items = (2 * T) // tm_exp + N_EXPERTS - 1

    xc = _ln_in(x.reshape(T, D), ln_in_g.astype(F32), ln_in_b.astype(F32))
    for l in range(depth):
        fq, fk, fv, hq, hi, hg, gates, hf, ffp = _inproj(xc, w_all, b_all, l)
        fox_o = _fox_attention(fq, fk, fv, ffp, batch, seq)
        hgrn_o = _hgrn(hq, hf, hi, hg, lb_all, ng_all, l, batch, seq)
        x1, x1r, route, counts = _merge(fox_o, hgrn_o, gates, xc, wf, wh, wm, bm, g1, b1,
                                        wrh, wrl, rbias, tri, l, alpha)
        dest, meta = _routing_tables(route, counts, tm_exp, n_items)
        xs = _dispatch(x1r, dest, T, nj)
        ys = _experts(meta, xs, expert_w1, expert_w3, expert_w2, l, tm_exp, nj)
        xc = _combine(dest, ys, route, x1, g2, b2, l, alpha)
    return xc.reshape(batch, seq, D)
```

```python
import functools

import jax
import jax.numpy as jnp
from jax import lax
from jax.experimental import pallas as pl
from jax.experimental.pallas import tpu as pltpu

F32 = jnp.float32
BF16 = jnp.bfloat16
I32 = jnp.int32

LANES = 128
SUBLANES = 8
HEAD_DIM = 64
FOX_HEADS = 8
HGRN_HEADS = 8
HEADS_PER_VREG = LANES // HEAD_DIM
FOX_PAIRS = FOX_HEADS // HEADS_PER_VREG
HGRN_PAIRS = HGRN_HEADS // HEADS_PER_VREG
FOX_WIDTH = FOX_HEADS * HEAD_DIM
HGRN_WIDTH = HGRN_HEADS * HEAD_DIM
N_EXPERTS = 16
N_GROUPS = 4
EXPERTS_PER_GROUP = N_EXPERTS // N_GROUPS
TOP_K = 2
HGRN_CHUNK = 16
LN_EPS = 1e-5
RMS_EPS = 1e-6
MASK_VALUE = -1e30
LOG2E = 1.4426950408889634
VMEM_LIMIT = 56 * 1024 * 1024

_C_FQ = 0
_C_FK = _C_FQ + FOX_WIDTH
_C_FV = _C_FK + FOX_WIDTH
_C_HQ = _C_FV + FOX_WIDTH
_C_HI = _C_HQ + HGRN_WIDTH
_C_HG = _C_HI + HGRN_WIDTH
_C_GATES = _C_HG + HGRN_WIDTH


def _params(sem, vmem=VMEM_LIMIT):
    return pltpu.CompilerParams(dimension_semantics=sem, vmem_limit_bytes=vmem)


def _split3(x):
    hi = x.astype(BF16)
    r1 = x - hi.astype(F32)
    mid = r1.astype(BF16)
    lo = (r1 - mid.astype(F32)).astype(BF16)
    return hi, mid, lo


def _dot(a, b):
    return jnp.dot(a, b, preferred_element_type=F32)


def _dot_nt(a, b):
    return lax.dot_general(a, b, (((1,), (1,)), ((), ())), preferred_element_type=F32)


def _dot_tn(a, b):
    return lax.dot_general(a, b, (((0,), (0,)), ((), ())), preferred_element_type=F32)


def _dot_exact_lhs(m, x):
    hi, mid, lo = _split3(x)
    return _dot(m, hi) + _dot(m, mid) + _dot(m, lo)


def _dot_exact_rhs(x, m):
    hi, mid, lo = _split3(x)
    return _dot(hi, m) + _dot(mid, m) + _dot(lo, m)


def _sigmoid(x):
    return 1.0 / (1.0 + jnp.exp(-x))


def _layer_norm(x, g, b):
    mu = jnp.mean(x, axis=-1, keepdims=True)
    xc = x - mu
    var = jnp.mean(xc * xc, axis=-1, keepdims=True)
    return xc * lax.rsqrt(var + LN_EPS) * g + b


def _row_tile(n, want):
    t = min(n, want)
    assert n % t == 0, (n, t)
    return t


def _ln_kernel(x_ref, g_ref, b_ref, o_ref):
    o_ref[...] = _layer_norm(x_ref[...], g_ref[...], b_ref[...])


def _ln_in(x, g, b):
    T, D = x.shape
    tm = _row_tile(T, 512)
    return pl.pallas_call(
        _ln_kernel,
        grid=(T // tm,),
        in_specs=[pl.BlockSpec((tm, D), lambda i: (i, 0)),
                  pl.BlockSpec((1, D), lambda i: (0, 0)),
                  pl.BlockSpec((1, D), lambda i: (0, 0))],
        out_specs=pl.BlockSpec((tm, D), lambda i: (i, 0)),
        out_shape=jax.ShapeDtypeStruct((T, D), F32),
        compiler_params=_params(("parallel",)),
        name="ln_in",
    )(x, g.reshape(1, D), b.reshape(1, D))


def _inproj_kernel(x_ref, w_ref, b_ref, fq_ref, fk_ref, fv_ref, hq_ref, hi_ref, hg_ref,
                   gates_ref, hf_ref, ff_ref, *, d_model):
    xb = x_ref[...].astype(BF16)

    def proj(c0, n):
        return _dot(xb, w_ref[:, c0:c0 + n]) + b_ref[:, c0:c0 + n]

    scale = HEAD_DIM ** -0.5
    fq_ref[...] = (proj(_C_FQ, FOX_WIDTH) * scale).astype(BF16)
    fk_ref[...] = proj(_C_FK, FOX_WIDTH).astype(BF16)
    fv_ref[...] = proj(_C_FV, FOX_WIDTH).astype(BF16)
    hq_ref[...] = proj(_C_HQ, HGRN_WIDTH).astype(BF16)
    hi_ref[...] = proj(_C_HI, HGRN_WIDTH).astype(BF16)
    hg_ref[...] = proj(_C_HG, HGRN_WIDTH).astype(BF16)
    n_gate_chunks = 2 * d_model // 512
    for c in range(n_gate_chunks):
        gates_ref[:, c * 512:(c + 1) * 512] = proj(_C_GATES + c * 512, 512).astype(BF16)
    c_hf = _C_GATES + 2 * d_model
    hf_ref[...] = proj(c_hf, HGRN_WIDTH)
    ff_ref[...] = proj(c_hf + HGRN_WIDTH, LANES)


def _inproj(x, w, b, layer):
    T, D = x.shape
    n_all = w.shape[-1]
    tm = _row_tile(T, 512)
    row = lambda n: pl.BlockSpec((tm, n), lambda i: (i, 0))
    outs = [(FOX_WIDTH, BF16)] * 3 + [(HGRN_WIDTH, BF16)] * 3 + [(2 * D, BF16), (HGRN_WIDTH, F32), (LANES, F32)]
    return pl.pallas_call(
        functools.partial(_inproj_kernel, d_model=D),
        grid=(T // tm,),
        in_specs=[row(D),
                  pl.BlockSpec((None, D, n_all), lambda i: (layer, 0, 0), pipeline_mode=pl.Buffered(1)),
                  pl.BlockSpec((None, 1, n_all), lambda i: (layer, 0, 0), pipeline_mode=pl.Buffered(1))],
        out_specs=[row(n) for n, _ in outs],
        out_shape=[jax.ShapeDtypeStruct((T, n), dt) for n, dt in outs],
        compiler_params=_params(("parallel",)),
        name="inproj",
    )(x, w, b)


def _fox_kernel(q_ref, k_ref, v_ref, ff_ref, o_ref, kp_ref, vt_ref, f_ref, qp_ref, m_ref, l_ref, acc_ref,
                s_ref, p_ref, *, seq, tq, cb):
    i = pl.program_id(1)
    nb = seq // cb
    lane = lax.broadcasted_iota(I32, (1, LANES), 1)

    def head_lanes(h):
        hh = h % HEADS_PER_VREG
        own = (lane >= hh * HEAD_DIM) & (lane < (hh + 1) * HEAD_DIM)
        e0 = (1 - hh) * HEAD_DIM
        return own, e0

    @pl.when(i == 0)
    def _prologue():
        rr = lax.broadcasted_iota(I32, (cb, cb), 0)
        cc = lax.broadcasted_iota(I32, (cb, cb), 1)
        tri = (cc <= rr).astype(BF16)

        def cum_block(blk, carry):
            rows = pl.ds(pl.multiple_of(blk * cb, cb), cb)
            x = ff_ref[rows, :]
            lf = jnp.minimum(x, 0.0) - jnp.log(1.0 + jnp.exp(-jnp.abs(x)))
            fb = _dot_exact_lhs(tri, lf * LOG2E) + carry
            f_ref[rows, :] = fb
            for h in range(FOX_HEADS):
                p = h // HEADS_PER_VREG
                own, e0 = head_lanes(h)
                hi, mid, lo = _split3(fb[:, h:h + 1])
                ext = jnp.where((lane >= e0) & (lane < e0 + 3), 1.0, 0.0)
                ext = jnp.where(lane == e0 + 3, -hi.astype(F32), ext)
                ext = jnp.where(lane == e0 + 4, -mid.astype(F32), ext)
                ext = jnp.where(lane == e0 + 5, -lo.astype(F32), ext)
                kpair = k_ref[rows, p * LANES:(p + 1) * LANES].astype(F32)
                vpair = v_ref[rows, p * LANES:(p + 1) * LANES].astype(F32)
                kp_ref[h * nb + blk] = jnp.where(own, kpair, ext).astype(BF16)
                vt_ref[h * nb + blk] = jnp.where(own, vpair, 0.0).T.astype(BF16)
            return fb[cb - 1:cb, :]

        lax.fori_loop(0, nb, cum_block, jnp.zeros((1, LANES), F32))

    q0 = pl.multiple_of(i * tq, tq)
    key_i = lax.broadcasted_iota(I32, (tq, tq), 0)
    qry_i = lax.broadcasted_iota(I32, (tq, tq), 1)
    causal = key_i <= qry_i
    for h in range(FOX_HEADS):
        p = h // HEADS_PER_VREG
        own, e0 = head_lanes(h)
        hi, mid, lo = _split3(f_ref[pl.ds(q0, tq), h:h + 1])
        ext = jnp.where((lane >= e0 + 3) & (lane < e0 + 6), 1.0, 0.0)
        ext = jnp.where(lane == e0, hi.astype(F32), ext)
        ext = jnp.where(lane == e0 + 1, mid.astype(F32), ext)
        ext = jnp.where(lane == e0 + 2, lo.astype(F32), ext)
        qpair = q_ref[:, p * LANES:(p + 1) * LANES].astype(F32) * LOG2E
        qp_ref[h] = jnp.where(own, qpair, ext).astype(BF16)
        m_ref[h] = jnp.full((1, tq), MASK_VALUE, F32)
        l_ref[h] = jnp.zeros((1, tq), F32)
        acc_ref[h] = jnp.zeros((LANES, tq), F32)

    def step(j, masked):
        for h in range(FOX_HEADS):
            s_ref[h] = _dot_nt(kp_ref[h * nb + j], qp_ref[h])
        rescale = []
        for h in range(FOX_HEADS):
            st = s_ref[h]
            if masked:
                st = jnp.where(causal, st, MASK_VALUE)
            m = m_ref[h]
            m_new = jnp.maximum(m, jnp.max(st, axis=0, keepdims=True))
            a = jnp.exp2(m - m_new)
            pt = jnp.exp2(st - m_new)
            m_ref[h] = m_new
            l_ref[h] = a * l_ref[h] + jnp.sum(pt, axis=0, keepdims=True)
            p_ref[h] = pt.astype(BF16)
            rescale.append(a)
        for h in range(FOX_HEADS):
            acc_ref[h] = rescale[h] * acc_ref[h] + _dot(vt_ref[h * nb + j], p_ref[h])

    def off_diagonal(j, _):
        step(j, False)
        return 0

    lax.fori_loop(0, i, off_diagonal, 0)
    step(i, True)
    for p in range(FOX_PAIRS):
        h0 = p * HEADS_PER_VREG
        out_t = acc_ref[h0] / l_ref[h0]
        for h in range(h0 + 1, h0 + HEADS_PER_VREG):
            out_t = out_t + acc_ref[h] / l_ref[h]
        o_ref[:, p * LANES:(p + 1) * LANES] = out_t.T.astype(BF16)


def _fox_attention(fq, fk, fv, ffp, batch, seq):
    T = fq.shape[0]
    tq = _row_tile(seq, 256)
    nq = seq // tq
    cb = tq
    return pl.pallas_call(
        functools.partial(_fox_kernel, seq=seq, tq=tq, cb=cb),
        grid=(batch, nq),
        in_specs=[pl.BlockSpec((tq, FOX_WIDTH), lambda b, i: (b * nq + i, 0)),
                  pl.BlockSpec((seq, FOX_WIDTH), lambda b, i: (b, 0)),
                  pl.BlockSpec((seq, FOX_WIDTH), lambda b, i: (b, 0)),
                  pl.BlockSpec((seq, LANES), lambda b, i: (b, 0))],
        out_specs=pl.BlockSpec((tq, FOX_WIDTH), lambda b, i: (b * nq + i, 0)),
        out_shape=jax.ShapeDtypeStruct((T, FOX_WIDTH), BF16),
        scratch_shapes=[pltpu.VMEM((FOX_HEADS * nq, cb, LANES), BF16),
                        pltpu.VMEM((FOX_HEADS * nq, LANES, cb), BF16),
                        pltpu.VMEM((seq, LANES), F32),
                        pltpu.VMEM((FOX_HEADS, tq, LANES), BF16),
                        pltpu.VMEM((FOX_HEADS, 1, tq), F32),
                        pltpu.VMEM((FOX_HEADS, 1, tq), F32),
                        pltpu.VMEM((FOX_HEADS, LANES, tq), F32),
                        pltpu.VMEM((FOX_HEADS, cb, tq), F32),
                        pltpu.VMEM((FOX_HEADS, cb, tq), BF16)],
        compiler_params=_params(("parallel", "arbitrary")),
        name="fox_attention",
    )(fq, fk, fv, ffp)


def _hgrn_kernel(hq_ref, hf_ref, hi_ref, hg_ref, lb_ref, ng_ref, o_ref, ss_ref, cpad_ref, vpad_ref,
                 term_ref, score_ref, *, seq, rb, wl):
    ch = HGRN_CHUNK
    n_states = wl // LANES
    rr = lax.broadcasted_iota(I32, (rb, rb), 0)
    cc = lax.broadcasted_iota(I32, (rb, rb), 1)
    same_chunk = (rr // ch) == (cc // ch)
    cum_mat = jnp.concatenate([(same_chunk & (cc <= rr)).astype(BF16), same_chunk.astype(BF16)], axis=0)
    hr = lax.broadcasted_iota(I32, (wl, wl), 0) // HEAD_DIM
    hc = lax.broadcasted_iota(I32, (wl, wl), 1) // HEAD_DIM
    head_ones = (hr == hc).astype(BF16)
    head_mask = (hr == hc)[:LANES, :LANES].astype(F32)
    tmod = lax.broadcasted_iota(I32, (rb, wl), 0) % ch
    lb = lb_ref[...]
    ng = ng_ref[...]
    ss_ref[...] = jnp.zeros(ss_ref.shape, F32)
    cpad_ref[0:ch, :] = jnp.zeros((ch, wl), F32)
    vpad_ref[0:ch, :] = jnp.zeros((ch, wl), F32)

    def block(r, _):
        rows = pl.ds(pl.multiple_of(r * rb, rb), rb)
        z = hf_ref[rows, :]
        g = jnp.log(lb + (1.0 - lb) * _sigmoid(z))
        k = (1.0 - lb) * _sigmoid(-z)
        hq = hq_ref[rows, :].astype(F32)
        qs = hq * _sigmoid(hq)
        v = hi_ref[rows, :].astype(F32)
        cums = _dot_exact_lhs(cum_mat, g * LOG2E)
        b = cums[:rb]
        btot = cums[rb:]
        qd = (qs * jnp.exp2(b)).astype(BF16)
        kd = (k * jnp.exp2(btot - b)).astype(BF16)
        dec = jnp.exp2(btot)

        c = b - jnp.log2(k)
        cpad_ref[ch:ch + rb, :] = c
        vpad_ref[ch:ch + rb, :] = v
        term_ref[0:rb, :] = (qs * k).astype(BF16)
        for o in range(1, ch):
            cs = cpad_ref[ch - o:ch - o + rb, :]
            term = jnp.where(tmod >= o, qs * jnp.exp2(b - cs), 0.0)
            term_ref[o * rb:(o + 1) * rb, :] = term.astype(BF16)
        score_ref[...] = _dot(term_ref[...], head_ones)
        acc = score_ref[0:rb, :] * v
        for o in range(1, ch):
            acc = acc + score_ref[o * rb:(o + 1) * rb, :] * vpad_ref[ch - o:ch - o + rb, :]

        vb = v.astype(BF16)
        chunks = [slice(n * ch, (n + 1) * ch) for n in range(rb // ch)]
        inter_cols = []
        for sidx in range(n_states):
            ln = slice(sidx * LANES, (sidx + 1) * LANES)
            upds = [_dot_tn(vb[sl, ln], kd[sl, ln]) * head_mask for sl in chunks]
            ss = ss_ref[sidx]
            states = []
            for n, sl in enumerate(chunks):
                states.append(ss.astype(BF16))
                ss = ss * dec[n * ch:n * ch + 1, ln] + upds[n]
            ss_ref[sidx] = ss
            inter_cols.append(jnp.concatenate(
                [_dot_nt(qd[sl, ln], st) for sl, st in zip(chunks, states)], axis=0))
        o_blk = acc + jnp.concatenate(inter_cols, axis=1)
        ms = _dot_exact_rhs(o_blk * o_blk, head_ones) * (1.0 / HEAD_DIM)
        hg = hg_ref[rows, :].astype(F32)
        o_blk = o_blk * lax.rsqrt(ms + RMS_EPS) * ng * (hg * _sigmoid(hg))
        o_ref[rows, :] = o_blk.astype(BF16)
        return 0

    lax.fori_loop(0, seq // rb, block, 0)


def _hgrn(hq, hf, hi, hg, lb, ng, layer, batch, seq):
    T = hq.shape[0]
    rb = _row_tile(seq, 128)
    wl = 2 * LANES
    n_prog = HGRN_WIDTH // wl
    blk = lambda: pl.BlockSpec((seq, wl), lambda b, p: (b, p))
    par = lambda: pl.BlockSpec((None, 1, wl), lambda b, p: (layer * n_prog + p, 0, 0))
    return pl.pallas_call(
        functools.partial(_hgrn_kernel, seq=seq, rb=rb, wl=wl),
        grid=(batch, n_prog),
        in_specs=[blk(), blk(), blk(), blk(), par(), par()],
        out_specs=blk(),
        out_shape=jax.ShapeDtypeStruct((T, HGRN_WIDTH), BF16),
        scratch_shapes=[pltpu.VMEM((wl // LANES, LANES, LANES), F32),
                        pltpu.VMEM((HGRN_CHUNK + rb, wl), F32),
                        pltpu.VMEM((HGRN_CHUNK + rb, wl), F32),
                        pltpu.VMEM((HGRN_CHUNK * rb, wl), BF16),
                        pltpu.VMEM((HGRN_CHUNK * rb, wl), F32)],
        compiler_params=_params(("parallel", "parallel")),
        name="hgrn2",
    )(hq, hf, hi, hg, lb.reshape(-1, 1, wl), ng.reshape(-1, 1, wl))


def _merge_kernel(fo_ref, ho_ref, gates_ref, x_ref, wf_ref, wh_ref, wm_ref, bm_ref, g_ref, b_ref,
                  wrh_ref, wrl_ref, rb_ref, tri_ref, x1_ref, x1r_ref, route_ref, cnt_ref,
                  *, alpha, d_model):
    i = pl.program_id(0)
    tm = x_ref.shape[0]

    @pl.when(i == 0)
    def _():
        cnt_ref[...] = jnp.zeros(cnt_ref.shape, F32)

    y_fox = _dot(fo_ref[...], wf_ref[...])
    y_hgrn = _dot(ho_ref[...], wh_ref[...])
    g_fox = _sigmoid(gates_ref[:, :d_model].astype(F32))
    g_hgrn = _sigmoid(gates_ref[:, d_model:].astype(F32))
    mixed = _dot((g_fox * y_fox + g_hgrn * y_hgrn).astype(BF16), wm_ref[...]) + bm_ref[...]
    x1 = _layer_norm(alpha * x_ref[...] + mixed, g_ref[...], b_ref[...])
    x1_ref[...] = x1
    nj = d_model // LANES
    for j in range(nj):
        x1r_ref[pl.ds(j, tm, stride=nj), :] = x1[:, j * LANES:(j + 1) * LANES]

    xh = x1.astype(BF16)
    xl = (x1 - xh.astype(F32)).astype(BF16)
    logits = _dot(xh, wrh_ref[...]) + _dot(xl, wrh_ref[...]) + _dot(xh, wrl_ref[...]) + rb_ref[...]
    lane = lax.broadcasted_iota(I32, (tm, LANES), 1)
    lane_f = lane.astype(F32)
    neg = jnp.float32(-jnp.inf)
    lg = jnp.where(lane < N_EXPERTS, logits, neg)
    m1 = jnp.max(lg, axis=-1, keepdims=True)
    idx1 = jnp.min(jnp.where(lg == m1, lane_f, float(LANES)), axis=-1, keepdims=True).astype(I32)
    in_group = (lane // EXPERTS_PER_GROUP == idx1 // EXPERTS_PER_GROUP) & (lane < N_EXPERTS)
    lg2 = jnp.where(in_group & (lane != idx1), logits, neg)
    m2 = jnp.max(lg2, axis=-1, keepdims=True)
    idx2 = jnp.min(jnp.where(lg2 == m2, lane_f, float(LANES)), axis=-1, keepdims=True).astype(I32)
    e21 = jnp.exp(m2 - m1)
    gate1 = 1.0 / (1.0 + e21)
    gate2 = e21 / (1.0 + e21)

    oh1 = lane == idx1
    oh2 = lane == idx2
    oh = (oh1 | oh2).astype(F32)
    before = _dot(tri_ref[...], oh.astype(BF16)) + cnt_ref[0:1, :]
    rank1 = jnp.sum(jnp.where(oh1, before, 0.0), axis=-1, keepdims=True)
    rank2 = jnp.sum(jnp.where(oh2, before, 0.0), axis=-1, keepdims=True)
    cnt_ref[0:1, :] = cnt_ref[0:1, :] + jnp.sum(oh, axis=0, keepdims=True)

    route = jnp.where(lane == 0, idx1.astype(F32), 0.0)
    route = jnp.where(lane == 1, idx2.astype(F32), route)
    route = jnp.where(lane == 2, gate1, route)
    route = jnp.where(lane == 3, gate2, route)
    route = jnp.where(lane == 4, rank1, route)
    route = jnp.where(lane == 5, rank2, route)
    route_ref[...] = route


def _merge(fox_o, hgrn_o, gates, x, wf, wh, wm, bm, g, b, wrh, wrl, rbias, tri, layer, alpha):
    T, D = x.shape
    tm = tri.shape[0]
    nj = D // LANES
    row = lambda n: pl.BlockSpec((tm, n), lambda i: (i, 0))
    lw = lambda r, c: pl.BlockSpec((None, r, c), lambda i: (layer, 0, 0))
    cw = lambda r, c: pl.BlockSpec((r, c), lambda i: (0, 0))
    return pl.pallas_call(
        functools.partial(_merge_kernel, alpha=alpha, d_model=D),
        grid=(T // tm,),
        in_specs=[row(FOX_WIDTH), row(HGRN_WIDTH), row(2 * D), row(D),
                  lw(FOX_WIDTH, D), lw(HGRN_WIDTH, D), lw(D, D), lw(1, D), lw(1, D), lw(1, D),
                  cw(D, LANES), cw(D, LANES), cw(1, LANES), cw(tm, tm)],
        out_specs=[row(D), pl.BlockSpec((tm * nj, LANES), lambda i: (i, 0)), row(LANES),
                   pl.BlockSpec((SUBLANES, LANES), lambda i: (0, 0))],
        out_shape=[jax.ShapeDtypeStruct((T, D), F32), jax.ShapeDtypeStruct((T * nj, LANES), F32),
                   jax.ShapeDtypeStruct((T, LANES), F32), jax.ShapeDtypeStruct((SUBLANES, LANES), F32)],
        compiler_params=_params(("arbitrary",)),
        name="merge_router",
    )(fox_o, hgrn_o, gates, x, wf, wh, wm, bm, g, b, wrh, wrl, rbias, tri)


def _dispatch_kernel(dest_ref, x_ref, xs_hbm, sem, *, tc, nj):
    def row_copy(r, d):
        return pltpu.make_async_copy(
            x_ref.at[pl.ds(pl.multiple_of(r * nj, nj), nj), :],
            xs_hbm.at[pl.ds(pl.multiple_of(d * nj, nj), nj), :], sem)

    def issue(r, _):
        row_copy(r, dest_ref[0, 0, 2 * r]).start()
        row_copy(r, dest_ref[0, 0, 2 * r + 1]).start()
        return 0

    lax.fori_loop(0, tc, issue, 0, unroll=8)

    def drain(r, _):
        row_copy(0, 0).wait()
        row_copy(0, 0).wait()
        return 0

    lax.fori_loop(0, tc, drain, 0, unroll=8)


def _dispatch(x1r, dest, n_tokens, nj):
    tc = _row_tile(n_tokens, 512)
    n_steps = n_tokens // tc
    return pl.pallas_call(
        functools.partial(_dispatch_kernel, tc=tc, nj=nj),
        grid=(n_steps,),
        in_specs=[pl.BlockSpec((1, 1, 2 * tc), lambda i: (i, 0, 0), memory_space=pltpu.SMEM),
                  pl.BlockSpec((tc * nj, LANES), lambda i: (i, 0))],
        out_specs=pl.BlockSpec(memory_space=pl.ANY),
        out_shape=jax.ShapeDtypeStruct((2 * n_tokens * nj, LANES), F32),
        scratch_shapes=[pltpu.SemaphoreType.DMA(())],
        compiler_params=pltpu.CompilerParams(dimension_semantics=("arbitrary",), has_side_effects=True),
        name="moe_dispatch",
    )(dest.reshape(n_steps, 1, 2 * tc), x1r)


def _experts_kernel(meta_ref, xs_ref, w1_ref, w3_ref, w2_ref, ys_ref, *, tm, nj):
    w = pl.program_id(0)
    lo = meta_ref[2, w]
    hi = meta_ref[3, w]
    first = meta_ref[4, w]

    @pl.when(hi > lo)
    def _():
        x = jnp.concatenate([xs_ref[pl.ds(j, tm, stride=nj), :] for j in range(nj)], axis=-1).astype(BF16)
        h1 = _dot(x, w1_ref[...].astype(BF16))
        h3 = _dot(x, w3_ref[...].astype(BF16))
        h = (h1 * _sigmoid(h1) * h3).astype(BF16)
        y = _dot(h, w2_ref[...].astype(BF16))
        rows = lax.broadcasted_iota(I32, (tm, 1), 0)
        mine = (rows >= lo) & (rows < hi)

        @pl.when(first == 1)
        def _():
            for j in range(nj):
                ys_ref[pl.ds(j, tm, stride=nj), :] = jnp.where(mine, y[:, j * LANES:(j + 1) * LANES], 0.0)

        @pl.when(first == 0)
        def _():
            for j in range(nj):
                cur = ys_ref[pl.ds(j, tm, stride=nj), :]
                ys_ref[pl.ds(j, tm, stride=nj), :] = jnp.where(mine, y[:, j * LANES:(j + 1) * LANES], cur)


def _experts(meta, xs, w1, w3, w2, layer, tm, nj):
    n_items = meta.shape[1]
    D = nj * LANES
    dh = w1.shape[-1]
    grid_spec = pltpu.PrefetchScalarGridSpec(
        num_scalar_prefetch=1,
        grid=(n_items,),
        in_specs=[pl.BlockSpec((tm * nj, LANES), lambda w, m: (m[0, w], 0)),
                  pl.BlockSpec((None, None, D, dh), lambda w, m: (layer, m[1, w], 0, 0)),
                  pl.BlockSpec((None, None, D, dh), lambda w, m: (layer, m[1, w], 0, 0)),
                  pl.BlockSpec((None, None, dh, D), lambda w, m: (layer, m[1, w], 0, 0))],
        out_specs=pl.BlockSpec((tm * nj, LANES), lambda w, m: (m[0, w], 0)),
    )
    return pl.pallas_call(
        functools.partial(_experts_kernel, tm=tm, nj=nj),
        grid_spec=grid_spec,
        out_shape=jax.ShapeDtypeStruct(xs.shape, F32),
        compiler_params=_params(("arbitrary",)),
        name="moe_experts",
    )(meta, xs, w1, w3, w2)


def _combine_kernel(dest_ref, ys_hbm, route_ref, x1_ref, g_ref, b_ref, o_ref, buf_ref, sem, *, tc, nj, alpha):
    def row_copy(d, slot, r):
        return pltpu.make_async_copy(
            ys_hbm.at[pl.ds(pl.multiple_of(d * nj, nj), nj), :],
            buf_ref.at[slot, pl.ds(pl.multiple_of(r * nj, nj), nj), :], sem)

    def issue(r, _):
        row_copy(dest_ref[0, 0, 2 * r], 0, r).start()
        row_copy(dest_ref[0, 0, 2 * r + 1], 1, r).start()
        return 0

    lax.fori_loop(0, tc, issue, 0, unroll=8)

    def drain(r, _):
        row_copy(0, 0, 0).wait()
        row_copy(0, 1, 0).wait()
        return 0

    lax.fori_loop(0, tc, drain, 0, unroll=8)

    gate1 = route_ref[:, 2:3]
    gate2 = route_ref[:, 3:4]
    y = jnp.concatenate(
        [gate1 * buf_ref[0, pl.ds(j, tc, stride=nj), :] + gate2 * buf_ref[1, pl.ds(j, tc, stride=nj), :]
         for j in range(nj)], axis=-1)
    o_ref[...] = _layer_norm(alpha * x1_ref[...] + y, g_ref[...], b_ref[...])


def _combine(dest, ys, route, x1, g, b, layer, alpha):
    T, D = x1.shape
    nj = D // LANES
    tc = _row_tile(T, 256)
    n_steps = T // tc
    row = lambda n: pl.BlockSpec((tc, n), lambda i: (i, 0))
    lw = lambda r, c: pl.BlockSpec((None, r, c), lambda i: (layer, 0, 0))
    return pl.pallas_call(
        functools.partial(_combine_kernel, tc=tc, nj=nj, alpha=alpha),
        grid=(n_steps,),
        in_specs=[pl.BlockSpec((1, 1, 2 * tc), lambda i: (i, 0, 0), memory_space=pltpu.SMEM),
                  pl.BlockSpec(memory_space=pl.ANY), row(LANES), row(D), lw(1, D), lw(1, D)],
        out_specs=row(D),
        out_shape=jax.ShapeDtypeStruct((T, D), F32),
        scratch_shapes=[pltpu.VMEM((2, tc * nj, LANES), F32), pltpu.SemaphoreType.DMA(())],
        compiler_params=_params(("arbitrary",)),
        name="moe_combine",
    )(dest.reshape(n_steps, 1, 2 * tc), ys, route, x1, g, b)


def _routing_tables(route, counts, tm, n_items):
    e = route[:, 0:2].astype(I32)
    rank = route[:, 4:6].astype(I32)
    cnt = counts[0, :N_EXPERTS].astype(I32)
    ends = jnp.cumsum(cnt)
    starts = ends - cnt
    dest = (starts[e] + rank).reshape(-1)

    first_tile = starts // tm
    last_tile = jnp.maximum(ends - 1, 0) // tm
    n_tiles_e = jnp.where(cnt > 0, last_tile - first_tile + 1, 0)
    item_end = jnp.cumsum(n_tiles_e)
    item_start = item_end - n_tiles_e
    n_real = item_end[-1]
    w = jnp.arange(n_items, dtype=I32)
    wc = jnp.minimum(w, n_real - 1)
    ex = jnp.sum((item_end[None, :] <= wc[:, None]).astype(I32), axis=1)
    tile = first_tile[ex] + (wc - item_start[ex])
    lo = jnp.clip(starts[ex] - tile * tm, 0, tm)
    hi = jnp.clip(ends[ex] - tile * tm, 0, tm)
    real = w < n_real
    hi = jnp.where(real, hi, lo)
    prev_tile = jnp.concatenate([jnp.full((1,), -1, I32), tile[:-1]])
    first = (real & (tile != prev_tile)).astype(I32)
    meta = jnp.stack([tile, ex, lo, hi, first]).astype(I32)
    return dest, meta


def kernel(x, ln_in_g, ln_in_b, w_in, b_in, w_fox_branch, hgrn_lb_logits, hgrn_norm_g, w_hgrn_branch,
           w_mix_out, b_mix_out, ln1_g, ln1_b, router_w, router_b, expert_w1, expert_w3, expert_w2,
           ln2_g, ln2_b):
    batch, seq, D = x.shape
    depth = w_in.shape[0]
    T = batch * seq
    nj = D // LANES
    alpha = float((2 * depth) ** 0.25)
    assert D % 512 == 0 and seq % HGRN_CHUNK == 0

    sizes = (FOX_WIDTH, FOX_WIDTH, FOX_WIDTH, FOX_HEADS, HGRN_WIDTH, HGRN_WIDTH, HGRN_WIDTH, HGRN_WIDTH, 2 * D)
    offs = [0]
    for s in sizes:
        offs.append(offs[-1] + s)
    col = lambda a, i: a[..., offs[i]:offs[i + 1]]
    order = (0, 1, 2, 4, 6, 7, 8, 5)
    pad_ff = lambda a: jnp.pad(col(a, 3), [(0, 0)] * (a.ndim - 1) + [(0, LANES - FOX_HEADS)])
    w_all = jnp.concatenate([col(w_in, i) for i in order] + [pad_ff(w_in)], axis=-1).astype(BF16)
    b_all = jnp.concatenate([col(b_in, i) for i in order] + [pad_ff(b_in)], axis=-1).astype(F32)[:, None, :]

    lb_p = jax.nn.softmax(hgrn_lb_logits.astype(F32), axis=0)
    lb_all = (jnp.cumsum(lb_p, axis=0) - lb_p[0]).reshape(depth * HGRN_PAIRS, 1, LANES)
    ng_all = hgrn_norm_g.astype(F32).reshape(depth * HGRN_PAIRS, 1, LANES)

    wf = w_fox_branch.astype(BF16)
    wh = w_hgrn_branch.astype(BF16)
    wm = w_mix_out.astype(BF16)
    r3 = lambda a: a.astype(F32)[:, None, :]
    bm, g1, b1, g2, b2 = r3(b_mix_out), r3(ln1_g), r3(ln1_b), r3(ln2_g), r3(ln2_b)
    rw = jnp.pad(router_w.astype(F32), ((0, 0), (0, LANES - N_EXPERTS)))
    wrh = rw.astype(BF16)
    wrl = (rw - wrh.astype(F32)).astype(BF16)
    rbias = jnp.pad(router_b.astype(F32), (0, LANES - N_EXPERTS)).reshape(1, LANES)

    tm_merge = _row_tile(T, 512)
    tri = jnp.tril(jnp.ones((tm_merge, tm_merge), BF16), k=-1)
    tm_exp = _row_tile(2 * T, 512)
    n_items = (2 * T) // tm_exp + N_EXPERTS - 1

    xc = _ln_in(x.reshape(T, D), ln_in_g.astype(F32), ln_in_b.astype(F32))
    for l in range(depth):
        fq, fk, fv, hq, hi, hg, gates, hf, ffp = _inproj(xc, w_all, b_all, l)
        fox_o = _fox_attention(fq, fk, fv, ffp, batch, seq)
        hgrn_o = _hgrn(hq, hf, hi, hg, lb_all, ng_all, l, batch, seq)
        x1, x1r, route, counts = _merge(fox_o, hgrn_o, gates, xc, wf, wh, wm, bm, g1, b1,
                                        wrh, wrl, rbias, tri, l, alpha)
        dest, meta = _routing_tables(route, counts, tm_exp, n_items)
        xs = _dispatch(x1r, dest, T, nj)
        ys = _experts(meta, xs, expert_w1, expert_w3, expert_w2, l, tm_exp, nj)
        xc = _combine(dest, ys, route, x1, g2, b2, l, alpha)
    return xc.reshape(batch, seq, D)
```

```python
import functools

import jax
import jax.numpy as jnp
from jax import lax
from jax.experimental import pallas as pl
from jax.experimental.pallas import tpu as pltpu

F32 = jnp.float32
BF16 = jnp.bfloat16
I32 = jnp.int32

LANES = 128
SUBLANES = 8
HEAD_DIM = 64
FOX_HEADS = 8
HGRN_HEADS = 8
HEADS_PER_VREG = LANES // HEAD_DIM
FOX_PAIRS = FOX_HEADS // HEADS_PER_VREG
HGRN_PAIRS = HGRN_HEADS // HEADS_PER_VREG
FOX_WIDTH = FOX_HEADS * HEAD_DIM
HGRN_WIDTH = HGRN_HEADS * HEAD_DIM
N_EXPERTS = 16
N_GROUPS = 4
EXPERTS_PER_GROUP = N_EXPERTS // N_GROUPS
TOP_K = 2
HGRN_CHUNK = 16
LN_EPS = 1e-5
RMS_EPS = 1e-6
MASK_VALUE = -1e30
LOG2E = 1.4426950408889634
VMEM_LIMIT = 56 * 1024 * 1024

_C_FQ = 0
_C_FK = _C_FQ + FOX_WIDTH
_C_FV = _C_FK + FOX_WIDTH
_C_HQ = _C_FV + FOX_WIDTH
_C_HI = _C_HQ + HGRN_WIDTH
_C_HG = _C_HI + HGRN_WIDTH
_C_GATES = _C_HG + HGRN_WIDTH


def _params(sem, vmem=VMEM_LIMIT):
    return pltpu.CompilerParams(dimension_semantics=sem, vmem_limit_bytes=vmem)


def _split3(x):
    hi = x.astype(BF16)
    r1 = x - hi.astype(F32)
    mid = r1.astype(BF16)
    lo = (r1 - mid.astype(F32)).astype(BF16)
    return hi, mid, lo


def _dot(a, b):
    return jnp.dot(a, b, preferred_element_type=F32)


def _dot_nt(a, b):
    return lax.dot_general(a, b, (((1,), (1,)), ((), ())), preferred_element_type=F32)


def _dot_tn(a, b):
    return lax.dot_general(a, b, (((0,), (0,)), ((), ())), preferred_element_type=F32)


def _dot_exact_lhs(m, x):
    hi, mid, lo = _split3(x)
    return _dot(m, hi) + _dot(m, mid) + _dot(m, lo)


def _dot_exact_rhs(x, m):
    hi, mid, lo = _split3(x)
    return _dot(hi, m) + _dot(mid, m) + _dot(lo, m)


def _sigmoid(x):
    return 1.0 / (1.0 + jnp.exp(-x))


def _layer_norm(x, g, b):
    mu = jnp.mean(x, axis=-1, keepdims=True)
    xc = x - mu
    var = jnp.mean(xc * xc, axis=-1, keepdims=True)
    return xc * lax.rsqrt(var + LN_EPS) * g + b


def _row_tile(n, want):
    t = min(n, want)
    assert n % t == 0, (n, t)
    return t


def _ln_kernel(x_ref, g_ref, b_ref, o_ref):
    o_ref[...] = _layer_norm(x_ref[...], g_ref[...], b_ref[...])


def _ln_in(x, g, b):
    T, D = x.shape
    tm = _row_tile(T, 512)
    return pl.pallas_call(
        _ln_kernel,
        grid=(T // tm,),
        in_specs=[pl.BlockSpec((tm, D), lambda i: (i, 0)),
                  pl.BlockSpec((1, D), lambda i: (0, 0)),
                  pl.BlockSpec((1, D), lambda i: (0, 0))],
        out_specs=pl.BlockSpec((tm, D), lambda i: (i, 0)),
        out_shape=jax.ShapeDtypeStruct((T, D), F32),
        compiler_params=_params(("parallel",)),
        name="ln_in",
    )(x, g.reshape(1, D), b.reshape(1, D))


def _inproj_kernel(x_ref, w_ref, b_ref, fq_ref, fk_ref, fv_ref, hq_ref, hi_ref, hg_ref,
                   gates_ref, hf_ref, ff_ref, *, d_model):
    xb = x_ref[...].astype(BF16)

    def proj(c0, n):
        return _dot(xb, w_ref[:, c0:c0 + n]) + b_ref[:, c0:c0 + n]

    scale = HEAD_DIM ** -0.5
    fq_ref[...] = (proj(_C_FQ, FOX_WIDTH) * scale).astype(BF16)
    fk_ref[...] = proj(_C_FK, FOX_WIDTH).astype(BF16)
    fv_ref[...] = proj(_C_FV, FOX_WIDTH).astype(BF16)
    hq_ref[...] = proj(_C_HQ, HGRN_WIDTH).astype(BF16)
    hi_ref[...] = proj(_C_HI, HGRN_WIDTH).astype(BF16)
    hg_ref[...] = proj(_C_HG, HGRN_WIDTH).astype(BF16)
    n_gate_chunks = 2 * d_model // 512
    for c in range(n_gate_chunks):
        gates_ref[:, c * 512:(c + 1) * 512] = proj(_C_GATES + c * 512, 512).astype(BF16)
    c_hf = _C_GATES + 2 * d_model
    hf_ref[...] = proj(c_hf, HGRN_WIDTH)
    ff_ref[...] = proj(c_hf + HGRN_WIDTH, LANES)


def _inproj(x, w, b, layer):
    T, D = x.shape
    n_all = w.shape[-1]
    tm = _row_tile(T, 512)
    row = lambda n: pl.BlockSpec((tm, n), lambda i: (i, 0))
    outs = [(FOX_WIDTH, BF16)] * 3 + [(HGRN_WIDTH, BF16)] * 3 + [(2 * D, BF16), (HGRN_WIDTH, F32), (LANES, F32)]
    return pl.pallas_call(
        functools.partial(_inproj_kernel, d_model=D),
        grid=(T // tm,),
        in_specs=[row(D),
                  pl.BlockSpec((None, D, n_all), lambda i: (layer, 0, 0), pipeline_mode=pl.Buffered(1)),
                  pl.BlockSpec((None, 1, n_all), lambda i: (layer, 0, 0), pipeline_mode=pl.Buffered(1))],
        out_specs=[row(n) for n, _ in outs],
        out_shape=[jax.ShapeDtypeStruct((T, n), dt) for n, dt in outs],
        compiler_params=_params(("parallel",)),
        name="inproj",
    )(x, w, b)


def _fox_kernel(q_ref, k_ref, v_ref, ff_ref, o_ref, kp_ref, vt_ref, f_ref, qp_ref, m_ref, l_ref, acc_ref,
                s_ref, p_ref, *, seq, tq, cb):
    i = pl.program_id(1)
    nb = seq // cb
    lane = lax.broadcasted_iota(I32, (1, LANES), 1)

    def head_lanes(h):
        hh = h % HEADS_PER_VREG
        own = (lane >= hh * HEAD_DIM) & (lane < (hh + 1) * HEAD_DIM)
        e0 = (1 - hh) * HEAD_DIM
        return own, e0

    @pl.when(i == 0)
    def _prologue():
        rr = lax.broadcasted_iota(I32, (cb, cb), 0)
        cc = lax.broadcasted_iota(I32, (cb, cb), 1)
        tri = (cc <= rr).astype(BF16)

        def cum_block(blk, carry):
            rows = pl.ds(pl.multiple_of(blk * cb, cb), cb)
            x = ff_ref[rows, :]
            lf = jnp.minimum(x, 0.0) - jnp.log(1.0 + jnp.exp(-jnp.abs(x)))
            fb = _dot_exact_lhs(tri, lf * LOG2E) + carry
            f_ref[rows, :] = fb
            for h in range(FOX_HEADS):
                p = h // HEADS_PER_VREG
                own, e0 = head_lanes(h)
                hi, mid, lo = _split3(fb[:, h:h + 1])
                ext = jnp.where((lane >= e0) & (lane < e0 + 3), 1.0, 0.0)
                ext = jnp.where(lane == e0 + 3, -hi.astype(F32), ext)
                ext = jnp.where(lane == e0 + 4, -mid.astype(F32), ext)
                ext = jnp.where(lane == e0 + 5, -lo.astype(F32), ext)
                kpair = k_ref[rows, p * LANES:(p + 1) * LANES].astype(F32)
                vpair = v_ref[rows, p * LANES:(p + 1) * LANES].astype(F32)
                kp_ref[h * nb + blk] = jnp.where(own, kpair, ext).astype(BF16)
                vt_ref[h * nb + blk] = jnp.where(own, vpair, 0.0).T.astype(BF16)
            return fb[cb - 1:cb, :]

        lax.fori_loop(0, nb, cum_block, jnp.zeros((1, LANES), F32))

    q0 = pl.multiple_of(i * tq, tq)
    key_i = lax.broadcasted_iota(I32, (tq, tq), 0)
    qry_i = lax.broadcasted_iota(I32, (tq, tq), 1)
    causal = key_i <= qry_i
    for h in range(FOX_HEADS):
        p = h // HEADS_PER_VREG
        own, e0 = head_lanes(h)
        hi, mid, lo = _split3(f_ref[pl.ds(q0, tq), h:h + 1])
        ext = jnp.where((lane >= e0 + 3) & (lane < e0 + 6), 1.0, 0.0)
        ext = jnp.where(lane == e0, hi.astype(F32), ext)
        ext = jnp.where(lane == e0 + 1, mid.astype(F32), ext)
        ext = jnp.where(lane == e0 + 2, lo.astype(F32), ext)
        qpair = q_ref[:, p * LANES:(p + 1) * LANES].astype(F32) * LOG2E
        qp_ref[h] = jnp.where(own, qpair, ext).astype(BF16)
        m_ref[h] = jnp.full((1, tq), MASK_VALUE, F32)
        l_ref[h] = jnp.zeros((1, tq), F32)
        acc_ref[h] = jnp.zeros((LANES, tq), F32)

    def step(j, masked):
        for h in range(FOX_HEADS):
            s_ref[h] = _dot_nt(kp_ref[h * nb + j], qp_ref[h])
        rescale = []
        for h in range(FOX_HEADS):
            def scores(h=h):
                st = s_ref[h]
                return jnp.where(causal, st, MASK_VALUE) if masked else st
            m = m_ref[h]
            m_new = jnp.maximum(m, jnp.max(scores(), axis=0, keepdims=True))
            a = jnp.exp2(m - m_new)
            pt = jnp.exp2(scores() - m_new)
            m_ref[h] = m_new
            l_ref[h] = a * l_ref[h] + jnp.sum(pt, axis=0, keepdims=True)
            p_ref[h] = pt.astype(BF16)
            rescale.append(a)
        for h in range(FOX_HEADS):
            acc_ref[h] = rescale[h] * acc_ref[h] + _dot(vt_ref[h * nb + j], p_ref[h])

    def off_diagonal(j, _):
        step(j, False)
        return 0

    lax.fori_loop(0, i, off_diagonal, 0)
    step(i, True)
    for p in range(FOX_PAIRS):
        h0 = p * HEADS_PER_VREG
        out_t = acc_ref[h0] / l_ref[h0]
        for h in range(h0 + 1, h0 + HEADS_PER_VREG):
            out_t = out_t + acc_ref[h] / l_ref[h]
        o_ref[:, p * LANES:(p + 1) * LANES] = out_t.T.astype(BF16)


def _fox_attention(fq, fk, fv, ffp, batch, seq):
    T = fq.shape[0]
    tq = _row_tile(seq, 256)
    nq = seq // tq
    cb = tq
    return pl.pallas_call(
        functools.partial(_fox_kernel, seq=seq, tq=tq, cb=cb),
        grid=(batch, nq),
        in_specs=[pl.BlockSpec((tq, FOX_WIDTH), lambda b, i: (b * nq + i, 0)),
                  pl.BlockSpec((seq, FOX_WIDTH), lambda b, i: (b, 0)),
                  pl.BlockSpec((seq, FOX_WIDTH), lambda b, i: (b, 0)),
                  pl.BlockSpec((seq, LANES), lambda b, i: (b, 0))],
        out_specs=pl.BlockSpec((tq, FOX_WIDTH), lambda b, i: (b * nq + i, 0)),
        out_shape=jax.ShapeDtypeStruct((T, FOX_WIDTH), BF16),
        scratch_shapes=[pltpu.VMEM((FOX_HEADS * nq, cb, LANES), BF16),
                        pltpu.VMEM((FOX_HEADS * nq, LANES, cb), BF16),
                        pltpu.VMEM((seq, LANES), F32),
                        pltpu.VMEM((FOX_HEADS, tq, LANES), BF16),
                        pltpu.VMEM((FOX_HEADS, 1, tq), F32),
                        pltpu.VMEM((FOX_HEADS, 1, tq), F32),
                        pltpu.VMEM((FOX_HEADS, LANES, tq), F32),
                        pltpu.VMEM((FOX_HEADS, cb, tq), F32),
                        pltpu.VMEM((FOX_HEADS, cb, tq), BF16)],
        compiler_params=_params(("parallel", "arbitrary")),
        name="fox_attention",
    )(fq, fk, fv, ffp)


def _hgrn_kernel(hq_ref, hf_ref, hi_ref, hg_ref, lb_ref, ng_ref, o_ref, ss_ref, cpad_ref, vpad_ref,
                 term_ref, score_ref, *, seq, rb, wl):
    ch = HGRN_CHUNK
    n_states = wl // LANES
    rr = lax.broadcasted_iota(I32, (rb, rb), 0)
    cc = lax.broadcasted_iota(I32, (rb, rb), 1)
    same_chunk = (rr // ch) == (cc // ch)
    cum_mat = jnp.concatenate([(same_chunk & (cc <= rr)).astype(BF16), same_chunk.astype(BF16)], axis=0)
    hr = lax.broadcasted_iota(I32, (wl, wl), 0) // HEAD_DIM
    hc = lax.broadcasted_iota(I32, (wl, wl), 1) // HEAD_DIM
    head_ones = (hr == hc).astype(BF16)
    head_mask = (hr == hc)[:LANES, :LANES].astype(F32)
    tmod = lax.broadcasted_iota(I32, (rb, wl), 0) % ch
    lb = lb_ref[...]
    ng = ng_ref[...]
    ss_ref[...] = jnp.zeros(ss_ref.shape, F32)
    cpad_ref[0:ch, :] = jnp.zeros((ch, wl), F32)
    vpad_ref[0:ch, :] = jnp.zeros((ch, wl), F32)

    def block(r, _):
        rows = pl.ds(pl.multiple_of(r * rb, rb), rb)
        z = hf_ref[rows, :]
        g = jnp.log(lb + (1.0 - lb) * _sigmoid(z))
        k = (1.0 - lb) * _sigmoid(-z)
        hq = hq_ref[rows, :].astype(F32)
        qs = hq * _sigmoid(hq)
        v = hi_ref[rows, :].astype(F32)
        cums = _dot_exact_lhs(cum_mat, g * LOG2E)
        b = cums[:rb]
        btot = cums[rb:]
        qd = (qs * jnp.exp2(b)).astype(BF16)
        kd = (k * jnp.exp2(btot - b)).astype(BF16)
        dec = jnp.exp2(btot)

        c = b - jnp.log2(k)
        cpad_ref[ch:ch + rb, :] = c
        vpad_ref[ch:ch + rb, :] = v
        term_ref[0:rb, :] = (qs * k).astype(BF16)
        for o in range(1, ch):
            cs = cpad_ref[ch - o:ch - o + rb, :]
            term = jnp.where(tmod >= o, qs * jnp.exp2(b - cs), 0.0)
            term_ref[o * rb:(o + 1) * rb, :] = term.astype(BF16)
        score_ref[...] = _dot(term_ref[...], head_ones)
        acc = score_ref[0:rb, :] * v
        for o in range(1, ch):
            acc = acc + score_ref[o * rb:(o + 1) * rb, :] * vpad_ref[ch - o:ch - o + rb, :]

        vb = v.astype(BF16)
        chunks = [slice(n * ch, (n + 1) * ch) for n in range(rb // ch)]
        inter_cols = []
        for sidx in range(n_states):
            ln = slice(sidx * LANES, (sidx + 1) * LANES)
            upds = [_dot_tn(vb[sl, ln], kd[sl, ln]) * head_mask for sl in chunks]
            ss = ss_ref[sidx]
            states = []
            for n, sl in enumerate(chunks):
                states.append(ss.astype(BF16))
                ss = ss * dec[n * ch:n * ch + 1, ln] + upds[n]
            ss_ref[sidx] = ss
            inter_cols.append(jnp.concatenate(
                [_dot_nt(qd[sl, ln], st) for sl, st in zip(chunks, states)], axis=0))
        o_blk = acc + jnp.concatenate(inter_cols, axis=1)
        ms = _dot_exact_rhs(o_blk * o_blk, head_ones) * (1.0 / HEAD_DIM)
        hg = hg_ref[rows, :].astype(F32)
        o_blk = o_blk * lax.rsqrt(ms + RMS_EPS) * ng * (hg * _sigmoid(hg))
        o_ref[rows, :] = o_blk.astype(BF16)
        return 0

    lax.fori_loop(0, seq // rb, block, 0)


def _hgrn(hq, hf, hi, hg, lb, ng, layer, batch, seq):
    T = hq.shape[0]
    rb = _row_tile(seq, 128)
    wl = 2 * LANES
    n_prog = HGRN_WIDTH // wl
    blk = lambda: pl.BlockSpec((seq, wl), lambda b, p: (b, p))
    par = lambda: pl.BlockSpec((None, 1, wl), lambda b, p: (layer * n_prog + p, 0, 0))
    return pl.pallas_call(
        functools.partial(_hgrn_kernel, seq=seq, rb=rb, wl=wl),
        grid=(batch, n_prog),
        in_specs=[blk(), blk(), blk(), blk(), par(), par()],
        out_specs=blk(),
        out_shape=jax.ShapeDtypeStruct((T, HGRN_WIDTH), BF16),
        scratch_shapes=[pltpu.VMEM((wl // LANES, LANES, LANES), F32),
                        pltpu.VMEM((HGRN_CHUNK + rb, wl), F32),
                        pltpu.VMEM((HGRN_CHUNK + rb, wl), F32),
                        pltpu.VMEM((HGRN_CHUNK * rb, wl), BF16),
                        pltpu.VMEM((HGRN_CHUNK * rb, wl), F32)],
        compiler_params=_params(("parallel", "parallel")),
        name="hgrn2",
    )(hq, hf, hi, hg, lb.reshape(-1, 1, wl), ng.reshape(-1, 1, wl))


def _merge_kernel(fo_ref, ho_ref, gates_ref, x_ref, wf_ref, wh_ref, wm_ref, bm_ref, g_ref, b_ref,
                  wrh_ref, wrl_ref, rb_ref, tri_ref, x1_ref, x1r_ref, route_ref, cnt_ref,
                  *, alpha, d_model):
    i = pl.program_id(0)
    tm = x_ref.shape[0]

    @pl.when(i == 0)
    def _():
        cnt_ref[...] = jnp.zeros(cnt_ref.shape, F32)

    y_fox = _dot(fo_ref[...], wf_ref[...])
    y_hgrn = _dot(ho_ref[...], wh_ref[...])
    g_fox = _sigmoid(gates_ref[:, :d_model].astype(F32))
    g_hgrn = _sigmoid(gates_ref[:, d_model:].astype(F32))
    mixed = _dot((g_fox * y_fox + g_hgrn * y_hgrn).astype(BF16), wm_ref[...]) + bm_ref[...]
    x1 = _layer_norm(alpha * x_ref[...] + mixed, g_ref[...], b_ref[...])
    x1_ref[...] = x1
    nj = d_model // LANES
    for j in range(nj):
        x1r_ref[pl.ds(j, tm, stride=nj), :] = x1[:, j * LANES:(j + 1) * LANES]

    xh = x1.astype(BF16)
    xl = (x1 - xh.astype(F32)).astype(BF16)
    logits = _dot(xh, wrh_ref[...]) + _dot(xl, wrh_ref[...]) + _dot(xh, wrl_ref[...]) + rb_ref[...]
    lane = lax.broadcasted_iota(I32, (tm, LANES), 1)
    lane_f = lane.astype(F32)
    neg = jnp.float32(-jnp.inf)
    lg = jnp.where(lane < N_EXPERTS, logits, neg)
    m1 = jnp.max(lg, axis=-1, keepdims=True)
    idx1 = jnp.min(jnp.where(lg == m1, lane_f, float(LANES)), axis=-1, keepdims=True).astype(I32)
    in_group = (lane // EXPERTS_PER_GROUP == idx1 // EXPERTS_PER_GROUP) & (lane < N_EXPERTS)
    lg2 = jnp.where(in_group & (lane != idx1), logits, neg)
    m2 = jnp.max(lg2, axis=-1, keepdims=True)
    idx2 = jnp.min(jnp.where(lg2 == m2, lane_f, float(LANES)), axis=-1, keepdims=True).astype(I32)
    e21 = jnp.exp(m2 - m1)
    gate1 = 1.0 / (1.0 + e21)
    gate2 = e21 / (1.0 + e21)

    oh1 = lane == idx1
    oh2 = lane == idx2
    oh = (oh1 | oh2).astype(F32)
    before = _dot(tri_ref[...], oh.astype(BF16)) + cnt_ref[0:1, :]
    rank1 = jnp.sum(jnp.where(oh1, before, 0.0), axis=-1, keepdims=True)
    rank2 = jnp.sum(jnp.where(oh2, before, 0.0), axis=-1, keepdims=True)
    cnt_ref[0:1, :] = cnt_ref[0:1, :] + jnp.sum(oh, axis=0, keepdims=True)

    route = jnp.where(lane == 0, idx1.astype(F32), 0.0)
    route = jnp.where(lane == 1, idx2.astype(F32), route)
    route = jnp.where(lane == 2, gate1, route)
    route = jnp.where(lane == 3, gate2, route)
    route = jnp.where(lane == 4, rank1, route)
    route = jnp.where(lane == 5, rank2, route)
    route_ref[...] = route


def _merge(fox_o, hgrn_o, gates, x, wf, wh, wm, bm, g, b, wrh, wrl, rbias, tri, layer, alpha):
    T, D = x.shape
    tm = tri.shape[0]
    nj = D // LANES
    row = lambda n: pl.BlockSpec((tm, n), lambda i: (i, 0))
    lw = lambda r, c: pl.BlockSpec((None, r, c), lambda i: (layer, 0, 0))
    cw = lambda r, c: pl.BlockSpec((r, c), lambda i: (0, 0))
    return pl.pallas_call(
        functools.partial(_merge_kernel, alpha=alpha, d_model=D),
        grid=(T // tm,),
        in_specs=[row(FOX_WIDTH), row(HGRN_WIDTH), row(2 * D), row(D),
                  lw(FOX_WIDTH, D), lw(HGRN_WIDTH, D), lw(D, D), lw(1, D), lw(1, D), lw(1, D),
                  cw(D, LANES), cw(D, LANES), cw(1, LANES), cw(tm, tm)],
        out_specs=[row(D), pl.BlockSpec((tm * nj, LANES), lambda i: (i, 0)), row(LANES),
                   pl.BlockSpec((SUBLANES, LANES), lambda i: (0, 0))],
        out_shape=[jax.ShapeDtypeStruct((T, D), F32), jax.ShapeDtypeStruct((T * nj, LANES), F32),
                   jax.ShapeDtypeStruct((T, LANES), F32), jax.ShapeDtypeStruct((SUBLANES, LANES), F32)],
        compiler_params=_params(("arbitrary",)),
        name="merge_router",
    )(fox_o, hgrn_o, gates, x, wf, wh, wm, bm, g, b, wrh, wrl, rbias, tri)


def _dispatch_kernel(dest_ref, x_ref, xs_hbm, sem, *, tc, nj):
    def row_copy(r, d):
        return pltpu.make_async_copy(
            x_ref.at[pl.ds(pl.multiple_of(r * nj, nj), nj), :],
            xs_hbm.at[pl.ds(pl.multiple_of(d * nj, nj), nj), :], sem)

    def issue(r, _):
        row_copy(r, dest_ref[0, 0, 2 * r]).start()
        row_copy(r, dest_ref[0, 0, 2 * r + 1]).start()
        return 0

    lax.fori_loop(0, tc, issue, 0, unroll=8)

    def drain(r, _):
        row_copy(0, 0).wait()
        row_copy(0, 0).wait()
        return 0

    lax.fori_loop(0, tc, drain, 0, unroll=8)


def _dispatch(x1r, dest, n_tokens, nj):
    tc = _row_tile(n_tokens, 512)
    n_steps = n_tokens // tc
    return pl.pallas_call(
        functools.partial(_dispatch_kernel, tc=tc, nj=nj),
        grid=(n_steps,),
        in_specs=[pl.BlockSpec((1, 1, 2 * tc), lambda i: (i, 0, 0), memory_space=pltpu.SMEM),
                  pl.BlockSpec((tc * nj, LANES), lambda i: (i, 0))],
        out_specs=pl.BlockSpec(memory_space=pl.ANY),
        out_shape=jax.ShapeDtypeStruct((2 * n_tokens * nj, LANES), F32),
        scratch_shapes=[pltpu.SemaphoreType.DMA(())],
        compiler_params=pltpu.CompilerParams(dimension_semantics=("arbitrary",), has_side_effects=True),
        name="moe_dispatch",
    )(dest.reshape(n_steps, 1, 2 * tc), x1r)


def _experts_kernel(meta_ref, inv_ref, xs_ref, w1_ref, w3_ref, w2_ref, y2_hbm,
                    w1b_ref, w3b_ref, w2b_ref, ybuf_ref, pend_ref, sem, *, tm, nj):
    w = pl.program_id(0)
    lo = meta_ref[2, w]
    hi = meta_ref[3, w]

    def row_copy(r, a):
        return pltpu.make_async_copy(
            ybuf_ref.at[pl.ds(pl.multiple_of(r * nj, nj), nj), :],
            y2_hbm.at[pl.ds(pl.multiple_of(a * nj, nj), nj), :], sem)

    def wait_rows(_, carry):
        row_copy(0, 0).wait()
        return carry

    def drain():
        n = pend_ref[0]

        @pl.when(n == tm)
        def _():
            lax.fori_loop(0, tm, wait_rows, 0, unroll=8)

        @pl.when(n != tm)
        def _():
            lax.fori_loop(0, n, wait_rows, 0)

        pend_ref[0] = 0

    def start_row(r, carry):
        row_copy(r, inv_ref[0, 0, r]).start()
        return carry

    @pl.when(w == 0)
    def _():
        pend_ref[0] = 0

    @pl.when(hi > lo)
    def _():
        @pl.when(meta_ref[4, w] == 1)
        def _():
            w1b_ref[...] = w1_ref[...].astype(BF16)
            w3b_ref[...] = w3_ref[...].astype(BF16)
            w2b_ref[...] = w2_ref[...].astype(BF16)

        x = jnp.concatenate([xs_ref[pl.ds(j, tm, stride=nj), :] for j in range(nj)], axis=-1).astype(BF16)
        h1 = _dot(x, w1b_ref[...])
        h3 = _dot(x, w3b_ref[...])
        h = (h1 * _sigmoid(h1) * h3).astype(BF16)
        y = _dot(h, w2b_ref[...])
        drain()
        for j in range(nj):
            ybuf_ref[pl.ds(j, tm, stride=nj), :] = y[:, j * LANES:(j + 1) * LANES]

        @pl.when((lo == 0) & (hi == tm))
        def _():
            lax.fori_loop(0, tm, start_row, 0, unroll=8)

        @pl.when((lo != 0) | (hi != tm))
        def _():
            lax.fori_loop(lo, hi, start_row, 0)

        pend_ref[0] = hi - lo

    @pl.when(w == pl.num_programs(0) - 1)
    def _():
        drain()


def _experts(meta, inv, xs, w1, w3, w2, layer, tm, nj):
    n_items = meta.shape[1]
    D = nj * LANES
    dh = w1.shape[-1]
    n_tiles = xs.shape[0] // (tm * nj)
    grid_spec = pltpu.PrefetchScalarGridSpec(
        num_scalar_prefetch=1,
        grid=(n_items,),
        in_specs=[pl.BlockSpec((1, 1, tm), lambda w, m: (m[0, w], 0, 0), memory_space=pltpu.SMEM),
                  pl.BlockSpec((tm * nj, LANES), lambda w, m: (m[0, w], 0)),
                  pl.BlockSpec((None, None, D, dh), lambda w, m: (layer, m[1, w], 0, 0)),
                  pl.BlockSpec((None, None, D, dh), lambda w, m: (layer, m[1, w], 0, 0)),
                  pl.BlockSpec((None, None, dh, D), lambda w, m: (layer, m[1, w], 0, 0))],
        out_specs=pl.BlockSpec(memory_space=pl.ANY),
        scratch_shapes=[pltpu.VMEM((D, dh), BF16), pltpu.VMEM((D, dh), BF16), pltpu.VMEM((dh, D), BF16),
                        pltpu.VMEM((tm * nj, LANES), F32), pltpu.SMEM((1,), I32),
                        pltpu.SemaphoreType.DMA(())],
    )
    return pl.pallas_call(
        functools.partial(_experts_kernel, tm=tm, nj=nj),
        grid_spec=grid_spec,
        out_shape=jax.ShapeDtypeStruct(xs.shape, F32),
        compiler_params=pltpu.CompilerParams(dimension_semantics=("arbitrary",), vmem_limit_bytes=VMEM_LIMIT,
                                             has_side_effects=True),
        name="moe_experts",
    )(meta, inv.reshape(n_tiles, 1, tm), xs, w1, w3, w2)


def _combine_kernel(y2_ref, route_ref, x1_ref, g_ref, b_ref, o_ref, *, tc, nj, alpha):
    gate1 = route_ref[:, 2:3]
    gate2 = route_ref[:, 3:4]
    y = jnp.concatenate(
        [gate1 * y2_ref[pl.ds(j, tc, stride=TOP_K * nj), :] + gate2 * y2_ref[pl.ds(nj + j, tc, stride=TOP_K * nj), :]
         for j in range(nj)], axis=-1)
    o_ref[...] = _layer_norm(alpha * x1_ref[...] + y, g_ref[...], b_ref[...])


def _combine(y2, route, x1, g, b, layer, alpha):
    T, D = x1.shape
    nj = D // LANES
    tc = _row_tile(T, 256)
    row = lambda n: pl.BlockSpec((tc, n), lambda i: (i, 0))
    lw = lambda r, c: pl.BlockSpec((None, r, c), lambda i: (layer, 0, 0))
    return pl.pallas_call(
        functools.partial(_combine_kernel, tc=tc, nj=nj, alpha=alpha),
        grid=(T // tc,),
        in_specs=[pl.BlockSpec((tc * TOP_K * nj, LANES), lambda i: (i, 0)), row(LANES), row(D), lw(1, D), lw(1, D)],
        out_specs=row(D),
        out_shape=jax.ShapeDtypeStruct((T, D), F32),
        compiler_params=_params(("parallel",)),
        name="moe_combine",
    )(y2, route, x1, g, b)


def _routing_tables(route, counts, tm, n_items):
    e = route[:, 0:2].astype(I32)
    rank = route[:, 4:6].astype(I32)
    cnt = counts[0, :N_EXPERTS].astype(I32)
    ends = jnp.cumsum(cnt)
    starts = ends - cnt
    dest = (starts[e] + rank).reshape(-1)

    first_tile = starts // tm
    last_tile = jnp.maximum(ends - 1, 0) // tm
    n_tiles_e = jnp.where(cnt > 0, last_tile - first_tile + 1, 0)
    item_end = jnp.cumsum(n_tiles_e)
    item_start = item_end - n_tiles_e
    n_real = item_end[-1]
    w = jnp.arange(n_items, dtype=I32)
    wc = jnp.minimum(w, n_real - 1)
    ex = jnp.sum((item_end[None, :] <= wc[:, None]).astype(I32), axis=1)
    tile = first_tile[ex] + (wc - item_start[ex])
    lo = jnp.clip(starts[ex] - tile * tm, 0, tm)
    hi = jnp.clip(ends[ex] - tile * tm, 0, tm)
    real = w < n_real
    hi = jnp.where(real, hi, lo)
    prev_ex = jnp.concatenate([jnp.full((1,), -1, I32), ex[:-1]])
    changed = (real & (ex != prev_ex)).astype(I32)
    meta = jnp.stack([tile, ex, lo, hi, changed]).astype(I32)
    n_assign = dest.shape[0]
    inv = jnp.zeros((n_assign,), I32).at[dest].set(jnp.arange(n_assign, dtype=I32))
    return dest, inv, meta


def kernel(x, ln_in_g, ln_in_b, w_in, b_in, w_fox_branch, hgrn_lb_logits, hgrn_norm_g, w_hgrn_branch,
           w_mix_out, b_mix_out, ln1_g, ln1_b, router_w, router_b, expert_w1, expert_w3, expert_w2,
           ln2_g, ln2_b):
    batch, seq, D = x.shape
    depth = w_in.shape[0]
    T = batch * seq
    nj = D // LANES
    alpha = float((2 * depth) ** 0.25)
    assert D % 512 == 0 and seq % HGRN_CHUNK == 0

    sizes = (FOX_WIDTH, FOX_WIDTH, FOX_WIDTH, FOX_HEADS, HGRN_WIDTH, HGRN_WIDTH, HGRN_WIDTH, HGRN_WIDTH, 2 * D)
    offs = [0]
    for s in sizes:
        offs.append(offs[-1] + s)
    col = lambda a, i: a[..., offs[i]:offs[i + 1]]
    order = (0, 1, 2, 4, 6, 7, 8, 5)
    pad_ff = lambda a: jnp.pad(col(a, 3), [(0, 0)] * (a.ndim - 1) + [(0, LANES - FOX_HEADS)])
    w_all = jnp.concatenate([col(w_in, i) for i in order] + [pad_ff(w_in)], axis=-1).astype(BF16)
    b_all = jnp.concatenate([col(b_in, i) for i in order] + [pad_ff(b_in)], axis=-1).astype(F32)[:, None, :]

    lb_p = jax.nn.softmax(hgrn_lb_logits.astype(F32), axis=0)
    lb_all = (jnp.cumsum(lb_p, axis=0) - lb_p[0]).reshape(depth * HGRN_PAIRS, 1, LANES)
    ng_all = hgrn_norm_g.astype(F32).reshape(depth * HGRN_PAIRS, 1, LANES)

    wf = w_fox_branch.astype(BF16)
    wh = w_hgrn_branch.astype(BF16)
    wm = w_mix_out.astype(BF16)
    r3 = lambda a: a.astype(F32)[:, None, :]
    bm, g1, b1, g2, b2 = r3(b_mix_out), r3(ln1_g), r3(ln1_b), r3(ln2_g), r3(ln2_b)
    rw = jnp.pad(router_w.astype(F32), ((0, 0), (0, LANES - N_EXPERTS)))
    wrh = rw.astype(BF16)
    wrl = (rw - wrh.astype(F32)).astype(BF16)
    rbias = jnp.pad(router_b.astype(F32), (0, LANES - N_EXPERTS)).reshape(1, LANES)

    tm_merge = _row_tile(T, 512)
    tri = jnp.tril(jnp.ones((tm_merge, tm_merge), BF16), k=-1)
    tm_exp = _row_tile(2 * T, 512)
    n_items = (2 * T) // tm_exp + N_EXPERTS - 1

    xc = _ln_in(x.reshape(T, D), ln_in_g.astype(F32), ln_in_b.astype(F32))
    for l in range(depth):
        fq, fk, fv, hq, hi, hg, gates, hf, ffp = _inproj(xc, w_all, b_all, l)
        fox_o = _fox_attention(fq, fk, fv, ffp, batch, seq)
        hgrn_o = _hgrn(hq, hf, hi, hg, lb_all, ng_all, l, batch, seq)
        x1, x1r, route, counts = _merge(fox_o, hgrn_o, gates, xc, wf, wh, wm, bm, g1, b1,
                                        wrh, wrl, rbias, tri, l, alpha)
        dest, inv, meta = _routing_tables(route, counts, tm_exp, n_items)
        xs = _dispatch(x1r, dest, T, nj)
        y2 = _experts(meta, inv, xs, expert_w1, expert_w3, expert_w2, l, tm_exp, nj)
        xc = _combine(y2, route, x1, g2, b2, l, alpha)
    return xc.reshape(batch, seq, D)
```

```python
import functools

import jax
import jax.numpy as jnp
from jax import lax
from jax.experimental import pallas as pl
from jax.experimental.pallas import tpu as pltpu

F32 = jnp.float32
BF16 = jnp.bfloat16
I32 = jnp.int32

LANES = 128
SUBLANES = 8
HEAD_DIM = 64
FOX_HEADS = 8
HGRN_HEADS = 8
HEADS_PER_VREG = LANES // HEAD_DIM
FOX_PAIRS = FOX_HEADS // HEADS_PER_VREG
HGRN_PAIRS = HGRN_HEADS // HEADS_PER_VREG
FOX_WIDTH = FOX_HEADS * HEAD_DIM
HGRN_WIDTH = HGRN_HEADS * HEAD_DIM
N_EXPERTS = 16
N_GROUPS = 4
EXPERTS_PER_GROUP = N_EXPERTS // N_GROUPS
TOP_K = 2
HGRN_CHUNK = 16
LN_EPS = 1e-5
RMS_EPS = 1e-6
MASK_VALUE = -1e30
LOG2E = 1.4426950408889634
VMEM_LIMIT = 56 * 1024 * 1024

_C_FQ = 0
_C_FK = _C_FQ + FOX_WIDTH
_C_FV = _C_FK + FOX_WIDTH
_C_HQ = _C_FV + FOX_WIDTH
_C_HI = _C_HQ + HGRN_WIDTH
_C_HG = _C_HI + HGRN_WIDTH
_C_GATES = _C_HG + HGRN_WIDTH


def _params(sem, vmem=VMEM_LIMIT):
    return pltpu.CompilerParams(dimension_semantics=sem, vmem_limit_bytes=vmem)


def _split3(x):
    hi = x.astype(BF16)
    r1 = x - hi.astype(F32)
    mid = r1.astype(BF16)
    lo = (r1 - mid.astype(F32)).astype(BF16)
    return hi, mid, lo


def _dot(a, b):
    return jnp.dot(a, b, preferred_element_type=F32)


def _dot_nt(a, b):
    return lax.dot_general(a, b, (((1,), (1,)), ((), ())), preferred_element_type=F32)


def _dot_tn(a, b):
    return lax.dot_general(a, b, (((0,), (0,)), ((), ())), preferred_element_type=F32)


def _dot_exact_lhs(m, x):
    hi, mid, lo = _split3(x)
    return _dot(m, hi) + _dot(m, mid) + _dot(m, lo)


def _dot_exact_rhs(x, m):
    hi, mid, lo = _split3(x)
    return _dot(hi, m) + _dot(mid, m) + _dot(lo, m)


def _sigmoid(x):
    return 1.0 / (1.0 + jnp.exp(-x))


def _layer_norm(x, g, b):
    mu = jnp.mean(x, axis=-1, keepdims=True)
    xc = x - mu
    var = jnp.mean(xc * xc, axis=-1, keepdims=True)
    return xc * lax.rsqrt(var + LN_EPS) * g + b


def _row_tile(n, want):
    t = min(n, want)
    assert n % t == 0, (n, t)
    return t


def _ln_kernel(x_ref, g_ref, b_ref, o_ref):
    o_ref[...] = _layer_norm(x_ref[...], g_ref[...], b_ref[...])


def _ln_in(x, g, b):
    T, D = x.shape
    tm = _row_tile(T, 512)
    return pl.pallas_call(
        _ln_kernel,
        grid=(T // tm,),
        in_specs=[pl.BlockSpec((tm, D), lambda i: (i, 0)),
                  pl.BlockSpec((1, D), lambda i: (0, 0)),
                  pl.BlockSpec((1, D), lambda i: (0, 0))],
        out_specs=pl.BlockSpec((tm, D), lambda i: (i, 0)),
        out_shape=jax.ShapeDtypeStruct((T, D), F32),
        compiler_params=_params(("parallel",)),
        name="ln_in",
    )(x, g.reshape(1, D), b.reshape(1, D))


def _inproj_kernel(x_ref, w_ref, b_ref, fq_ref, fk_ref, fv_ref, hq_ref, hi_ref, hg_ref,
                   gates_ref, hf_ref, ff_ref, *, d_model):
    xb = x_ref[...].astype(BF16)

    def proj(c0, n):
        return _dot(xb, w_ref[:, c0:c0 + n]) + b_ref[:, c0:c0 + n]

    scale = HEAD_DIM ** -0.5
    fq_ref[...] = (proj(_C_FQ, FOX_WIDTH) * scale).astype(BF16)
    fk_ref[...] = proj(_C_FK, FOX_WIDTH).astype(BF16)
    fv_ref[...] = proj(_C_FV, FOX_WIDTH).astype(BF16)
    hq_ref[...] = proj(_C_HQ, HGRN_WIDTH).astype(BF16)
    hi_ref[...] = proj(_C_HI, HGRN_WIDTH).astype(BF16)
    hg_ref[...] = proj(_C_HG, HGRN_WIDTH).astype(BF16)
    n_gate_chunks = 2 * d_model // 512
    for c in range(n_gate_chunks):
        gates_ref[:, c * 512:(c + 1) * 512] = proj(_C_GATES + c * 512, 512).astype(BF16)
    c_hf = _C_GATES + 2 * d_model
    hf_ref[...] = proj(c_hf, HGRN_WIDTH)
    ff_ref[...] = proj(c_hf + HGRN_WIDTH, LANES)


def _inproj(x, w, b, layer):
    T, D = x.shape
    n_all = w.shape[-1]
    tm = _row_tile(T, 512)
    row = lambda n: pl.BlockSpec((tm, n), lambda i: (i, 0))
    outs = [(FOX_WIDTH, BF16)] * 3 + [(HGRN_WIDTH, BF16)] * 3 + [(2 * D, BF16), (HGRN_WIDTH, F32), (LANES, F32)]
    return pl.pallas_call(
        functools.partial(_inproj_kernel, d_model=D),
        grid=(T // tm,),
        in_specs=[row(D),
                  pl.BlockSpec((None, D, n_all), lambda i: (layer, 0, 0), pipeline_mode=pl.Buffered(1)),
                  pl.BlockSpec((None, 1, n_all), lambda i: (layer, 0, 0), pipeline_mode=pl.Buffered(1))],
        out_specs=[row(n) for n, _ in outs],
        out_shape=[jax.ShapeDtypeStruct((T, n), dt) for n, dt in outs],
        compiler_params=_params(("parallel",)),
        name="inproj",
    )(x, w, b)


def _fox_kernel(q_ref, k_ref, v_ref, ff_ref, o_ref, kp_ref, vt_ref, f_ref, qp_ref, m_ref, l_ref, acc_ref,
                s_ref, p_ref, a_ref, *, seq, tq, cb):
    i = pl.program_id(1)
    nb = seq // cb
    lane = lax.broadcasted_iota(I32, (1, LANES), 1)

    def head_lanes(h):
        hh = h % HEADS_PER_VREG
        own = (lane >= hh * HEAD_DIM) & (lane < (hh + 1) * HEAD_DIM)
        e0 = (1 - hh) * HEAD_DIM
        return own, e0

    @pl.when(i == 0)
    def _prologue():
        rr = lax.broadcasted_iota(I32, (cb, cb), 0)
        cc = lax.broadcasted_iota(I32, (cb, cb), 1)
        tri = (cc <= rr).astype(BF16)

        def cum_block(blk, carry):
            rows = pl.ds(pl.multiple_of(blk * cb, cb), cb)
            x = ff_ref[rows, :]
            lf = jnp.minimum(x, 0.0) - jnp.log(1.0 + jnp.exp(-jnp.abs(x)))
            fb = _dot_exact_lhs(tri, lf * LOG2E) + carry
            f_ref[rows, :] = fb
            for h in range(FOX_HEADS):
                p = h // HEADS_PER_VREG
                own, e0 = head_lanes(h)
                hi, mid, lo = _split3(fb[:, h:h + 1])
                ext = jnp.where((lane >= e0) & (lane < e0 + 3), 1.0, 0.0)
                ext = jnp.where(lane == e0 + 3, -hi.astype(F32), ext)
                ext = jnp.where(lane == e0 + 4, -mid.astype(F32), ext)
                ext = jnp.where(lane == e0 + 5, -lo.astype(F32), ext)
                kpair = k_ref[rows, p * LANES:(p + 1) * LANES].astype(F32)
                vpair = v_ref[rows, p * LANES:(p + 1) * LANES].astype(F32)
                kp_ref[h * nb + blk] = jnp.where(own, kpair, ext).astype(BF16)
                vt_ref[h * nb + blk] = jnp.where(own, vpair, 0.0).T.astype(BF16)
            return fb[cb - 1:cb, :]

        lax.fori_loop(0, nb, cum_block, jnp.zeros((1, LANES), F32))

    q0 = pl.multiple_of(i * tq, tq)
    key_i = lax.broadcasted_iota(I32, (tq, tq), 0)
    qry_i = lax.broadcasted_iota(I32, (tq, tq), 1)
    causal = key_i <= qry_i
    for h in range(FOX_HEADS):
        p = h // HEADS_PER_VREG
        own, e0 = head_lanes(h)
        hi, mid, lo = _split3(f_ref[pl.ds(q0, tq), h:h + 1])
        ext = jnp.where((lane >= e0 + 3) & (lane < e0 + 6), 1.0, 0.0)
        ext = jnp.where(lane == e0, hi.astype(F32), ext)
        ext = jnp.where(lane == e0 + 1, mid.astype(F32), ext)
        ext = jnp.where(lane == e0 + 2, lo.astype(F32), ext)
        qpair = q_ref[:, p * LANES:(p + 1) * LANES].astype(F32) * LOG2E
        qp_ref[h] = jnp.where(own, qpair, ext).astype(BF16)
        m_ref[h] = jnp.full((1, tq), MASK_VALUE, F32)
        l_ref[h] = jnp.zeros((1, tq), F32)
        acc_ref[h] = jnp.zeros((LANES, tq), F32)

    def scores(j, par):
        for h in range(FOX_HEADS):
            s_ref[par * FOX_HEADS + h] = _dot_nt(kp_ref[h * nb + j], qp_ref[h])

    def softmax(par, masked):
        for h in range(FOX_HEADS):
            st = s_ref[par * FOX_HEADS + h]
            if masked:
                st = jnp.where(causal, st, MASK_VALUE)
            m = m_ref[h]
            m_new = jnp.maximum(m, jnp.max(st, axis=0, keepdims=True))
            a = jnp.exp2(m - m_new)
            pt = jnp.exp2(st - m_new)
            m_ref[h] = m_new
            l_ref[h] = a * l_ref[h] + jnp.sum(pt, axis=0, keepdims=True)
            p_ref[par * FOX_HEADS + h] = pt.astype(BF16)
            a_ref[par * FOX_HEADS + h] = a

    def values(j, par):
        for h in range(FOX_HEADS):
            acc_ref[h] = a_ref[par * FOX_HEADS + h] * acc_ref[h] + _dot(vt_ref[h * nb + j],
                                                                     p_ref[par * FOX_HEADS + h])

    for h in range(FOX_HEADS):
        p_ref[FOX_HEADS + h] = jnp.zeros((cb, tq), BF16)
        a_ref[FOX_HEADS + h] = jnp.ones((1, tq), F32)
    scores(0, 0)

    def trip(t, par):
        scores(t + 1, 1 - par)
        softmax(par, False)
        values(jnp.maximum(t - 1, 0), 1 - par)

    def two_trips(u, _):
        trip(2 * u, 0)
        trip(2 * u + 1, 1)
        return 0

    lax.fori_loop(0, i // 2, two_trips, 0)

    @pl.when(i % 2 == 1)
    def _():
        trip(i - 1, 0)
        softmax(1, True)
        values(i - 1, 0)
        values(i, 1)

    @pl.when(i % 2 == 0)
    def _():
        softmax(0, True)
        values(jnp.maximum(i - 1, 0), 1)
        values(i, 0)
    for p in range(FOX_PAIRS):
        h0 = p * HEADS_PER_VREG
        out_t = acc_ref[h0] / l_ref[h0]
        for h in range(h0 + 1, h0 + HEADS_PER_VREG):
            out_t = out_t + acc_ref[h] / l_ref[h]
        o_ref[:, p * LANES:(p + 1) * LANES] = out_t.T.astype(BF16)


def _fox_attention(fq, fk, fv, ffp, batch, seq):
    T = fq.shape[0]
    tq = _row_tile(seq, 256)
    nq = seq // tq
    cb = tq
    return pl.pallas_call(
        functools.partial(_fox_kernel, seq=seq, tq=tq, cb=cb),
        grid=(batch, nq),
        in_specs=[pl.BlockSpec((tq, FOX_WIDTH), lambda b, i: (b * nq + i, 0)),
                  pl.BlockSpec((seq, FOX_WIDTH), lambda b, i: (b, 0)),
                  pl.BlockSpec((seq, FOX_WIDTH), lambda b, i: (b, 0)),
                  pl.BlockSpec((seq, LANES), lambda b, i: (b, 0))],
        out_specs=pl.BlockSpec((tq, FOX_WIDTH), lambda b, i: (b * nq + i, 0)),
        out_shape=jax.ShapeDtypeStruct((T, FOX_WIDTH), BF16),
        scratch_shapes=[pltpu.VMEM((FOX_HEADS * nq, cb, LANES), BF16),
                        pltpu.VMEM((FOX_HEADS * nq, LANES, cb), BF16),
                        pltpu.VMEM((seq, LANES), F32),
                        pltpu.VMEM((FOX_HEADS, tq, LANES), BF16),
                        pltpu.VMEM((FOX_HEADS, 1, tq), F32),
                        pltpu.VMEM((FOX_HEADS, 1, tq), F32),
                        pltpu.VMEM((FOX_HEADS, LANES, tq), F32),
                        pltpu.VMEM((2 * FOX_HEADS, cb, tq), F32),
                        pltpu.VMEM((2 * FOX_HEADS, cb, tq), BF16),
                        pltpu.VMEM((2 * FOX_HEADS, 1, tq), F32)],
        compiler_params=_params(("parallel", "arbitrary")),
        name="fox_attention",
    )(fq, fk, fv, ffp)


def _hgrn_kernel(hq_ref, hf_ref, hi_ref, hg_ref, lb_ref, ng_ref, o_ref, ss_ref, cpad_ref, vpad_ref,
                 term_ref, score_ref, *, seq, rb, wl):
    ch = HGRN_CHUNK
    n_states = wl // LANES
    rr = lax.broadcasted_iota(I32, (rb, rb), 0)
    cc = lax.broadcasted_iota(I32, (rb, rb), 1)
    same_chunk = (rr // ch) == (cc // ch)
    cum_mat = jnp.concatenate([(same_chunk & (cc <= rr)).astype(BF16), same_chunk.astype(BF16)], axis=0)
    hr = lax.broadcasted_iota(I32, (wl, wl), 0) // HEAD_DIM
    hc = lax.broadcasted_iota(I32, (wl, wl), 1) // HEAD_DIM
    head_ones = (hr == hc).astype(BF16)
    head_mask = (hr == hc)[:LANES, :LANES].astype(F32)
    tmod = lax.broadcasted_iota(I32, (rb, wl), 0) % ch
    lb = lb_ref[...]
    ng = ng_ref[...]
    ss_ref[...] = jnp.zeros(ss_ref.shape, F32)
    cpad_ref[0:ch, :] = jnp.zeros((ch, wl), F32)
    vpad_ref[0:ch, :] = jnp.zeros((ch, wl), F32)

    def block(r, _):
        rows = pl.ds(pl.multiple_of(r * rb, rb), rb)
        z = hf_ref[rows, :]
        g = jnp.log(lb + (1.0 - lb) * _sigmoid(z))
        k = (1.0 - lb) * _sigmoid(-z)
        hq = hq_ref[rows, :].astype(F32)
        qs = hq * _sigmoid(hq)
        v = hi_ref[rows, :].astype(F32)
        cums = _dot_exact_lhs(cum_mat, g * LOG2E)
        b = cums[:rb]
        btot = cums[rb:]
        qd = (qs * jnp.exp2(b)).astype(BF16)
        kd = (k * jnp.exp2(btot - b)).astype(BF16)
        dec = jnp.exp2(btot)

        c = b - jnp.log2(k)
        cpad_ref[ch:ch + rb, :] = c
        vpad_ref[ch:ch + rb, :] = v
        term_ref[0:rb, :] = (qs * k).astype(BF16)
        for o in range(1, ch):
            cs = cpad_ref[ch - o:ch - o + rb, :]
            term = jnp.where(tmod >= o, qs * jnp.exp2(b - cs), 0.0)
            term_ref[o * rb:(o + 1) * rb, :] = term.astype(BF16)
        score_ref[...] = _dot(term_ref[...], head_ones)
        acc = score_ref[0:rb, :] * v
        for o in range(1, ch):
            acc = acc + score_ref[o * rb:(o + 1) * rb, :] * vpad_ref[ch - o:ch - o + rb, :]

        vb = v.astype(BF16)
        chunks = [slice(n * ch, (n + 1) * ch) for n in range(rb // ch)]
        inter_cols = []
        for sidx in range(n_states):
            ln = slice(sidx * LANES, (sidx + 1) * LANES)
            upds = [_dot_tn(vb[sl, ln], kd[sl, ln]) * head_mask for sl in chunks]
            ss = ss_ref[sidx]
            states = []
            for n, sl in enumerate(chunks):
                states.append(ss.astype(BF16))
                ss = ss * dec[n * ch:n * ch + 1, ln] + upds[n]
            ss_ref[sidx] = ss
            inter_cols.append(jnp.concatenate(
                [_dot_nt(qd[sl, ln], st) for sl, st in zip(chunks, states)], axis=0))
        o_blk = acc + jnp.concatenate(inter_cols, axis=1)
        ms = _dot_exact_rhs(o_blk * o_blk, head_ones) * (1.0 / HEAD_DIM)
        hg = hg_ref[rows, :].astype(F32)
        o_blk = o_blk * lax.rsqrt(ms + RMS_EPS) * ng * (hg * _sigmoid(hg))
        o_ref[rows, :] = o_blk.astype(BF16)
        return 0

    lax.fori_loop(0, seq // rb, block, 0, unroll=2 if (seq // rb) % 2 == 0 else 1)


def _hgrn(hq, hf, hi, hg, lb, ng, layer, batch, seq):
    T = hq.shape[0]
    rb = _row_tile(seq, 128)
    wl = 2 * LANES
    n_prog = HGRN_WIDTH // wl
    blk = lambda: pl.BlockSpec((seq, wl), lambda b, p: (b, p))
    par = lambda: pl.BlockSpec((None, 1, wl), lambda b, p: (layer * n_prog + p, 0, 0))
    return pl.pallas_call(
        functools.partial(_hgrn_kernel, seq=seq, rb=rb, wl=wl),
        grid=(batch, n_prog),
        in_specs=[blk(), blk(), blk(), blk(), par(), par()],
        out_specs=blk(),
        out_shape=jax.ShapeDtypeStruct((T, HGRN_WIDTH), BF16),
        scratch_shapes=[pltpu.VMEM((wl // LANES, LANES, LANES), F32),
                        pltpu.VMEM((HGRN_CHUNK + rb, wl), F32),
                        pltpu.VMEM((HGRN_CHUNK + rb, wl), F32),
                        pltpu.VMEM((HGRN_CHUNK * rb, wl), BF16),
                        pltpu.VMEM((HGRN_CHUNK * rb, wl), F32)],
        compiler_params=_params(("parallel", "parallel")),
        name="hgrn2",
    )(hq, hf, hi, hg, lb.reshape(-1, 1, wl), ng.reshape(-1, 1, wl))


def _merge_kernel(fo_ref, ho_ref, gates_ref, x_ref, wf_ref, wh_ref, wm_ref, bm_ref, g_ref, b_ref,
                  wrh_ref, wrl_ref, rb_ref, tri_ref, x1_ref, x1r_ref, route_ref, cnt_ref,
                  *, alpha, d_model):
    i = pl.program_id(0)
    tm = x_ref.shape[0]

    @pl.when(i == 0)
    def _():
        cnt_ref[...] = jnp.zeros(cnt_ref.shape, F32)

    y_fox = _dot(fo_ref[...], wf_ref[...])
    y_hgrn = _dot(ho_ref[...], wh_ref[...])
    g_fox = _sigmoid(gates_ref[:, :d_model].astype(F32))
    g_hgrn = _sigmoid(gates_ref[:, d_model:].astype(F32))
    mixed = _dot((g_fox * y_fox + g_hgrn * y_hgrn).astype(BF16), wm_ref[...]) + bm_ref[...]
    x1 = _layer_norm(alpha * x_ref[...] + mixed, g_ref[...], b_ref[...])
    x1_ref[...] = x1
    nj = d_model // LANES
    for j in range(nj):
        x1r_ref[pl.ds(j, tm, stride=nj), :] = x1[:, j * LANES:(j + 1) * LANES]

    xh = x1.astype(BF16)
    xl = (x1 - xh.astype(F32)).astype(BF16)
    logits = _dot(xh, wrh_ref[...]) + _dot(xl, wrh_ref[...]) + _dot(xh, wrl_ref[...]) + rb_ref[...]
    lane = lax.broadcasted_iota(I32, (tm, LANES), 1)
    lane_f = lane.astype(F32)
    neg = jnp.float32(-jnp.inf)
    lg = jnp.where(lane < N_EXPERTS, logits, neg)
    m1 = jnp.max(lg, axis=-1, keepdims=True)
    idx1 = jnp.min(jnp.where(lg == m1, lane_f, float(LANES)), axis=-1, keepdims=True).astype(I32)
    in_group = (lane // EXPERTS_PER_GROUP == idx1 // EXPERTS_PER_GROUP) & (lane < N_EXPERTS)
    lg2 = jnp.where(in_group & (lane != idx1), logits, neg)
    m2 = jnp.max(lg2, axis=-1, keepdims=True)
    idx2 = jnp.min(jnp.where(lg2 == m2, lane_f, float(LANES)), axis=-1, keepdims=True).astype(I32)
    e21 = jnp.exp(m2 - m1)
    gate1 = 1.0 / (1.0 + e21)
    gate2 = e21 / (1.0 + e21)

    oh1 = lane == idx1
    oh2 = lane == idx2
    oh = (oh1 | oh2).astype(F32)
    before = _dot(tri_ref[...], oh.astype(BF16)) + cnt_ref[0:1, :]
    rank1 = jnp.sum(jnp.where(oh1, before, 0.0), axis=-1, keepdims=True)
    rank2 = jnp.sum(jnp.where(oh2, before, 0.0), axis=-1, keepdims=True)
    cnt_ref[0:1, :] = cnt_ref[0:1, :] + jnp.sum(oh, axis=0, keepdims=True)

    route = jnp.where(lane == 0, idx1.astype(F32), 0.0)
    route = jnp.where(lane == 1, idx2.astype(F32), route)
    route = jnp.where(lane == 2, gate1, route)
    route = jnp.where(lane == 3, gate2, route)
    route = jnp.where(lane == 4, rank1, route)
    route = jnp.where(lane == 5, rank2, route)
    route_ref[...] = route


def _merge(fox_o, hgrn_o, gates, x, wf, wh, wm, bm, g, b, wrh, wrl, rbias, tri, layer, alpha):
    T, D = x.shape
    tm = tri.shape[0]
    nj = D // LANES
    row = lambda n: pl.BlockSpec((tm, n), lambda i: (i, 0))
    lw = lambda r, c: pl.BlockSpec((None, r, c), lambda i: (layer, 0, 0))
    cw = lambda r, c: pl.BlockSpec((r, c), lambda i: (0, 0))
    return pl.pallas_call(
        functools.partial(_merge_kernel, alpha=alpha, d_model=D),
        grid=(T // tm,),
        in_specs=[row(FOX_WIDTH), row(HGRN_WIDTH), row(2 * D), row(D),
                  lw(FOX_WIDTH, D), lw(HGRN_WIDTH, D), lw(D, D), lw(1, D), lw(1, D), lw(1, D),
                  cw(D, LANES), cw(D, LANES), cw(1, LANES), cw(tm, tm)],
        out_specs=[row(D), pl.BlockSpec((tm * nj, LANES), lambda i: (i, 0)), row(LANES),
                   pl.BlockSpec((SUBLANES, LANES), lambda i: (0, 0))],
        out_shape=[jax.ShapeDtypeStruct((T, D), F32), jax.ShapeDtypeStruct((T * nj, LANES), F32),
                   jax.ShapeDtypeStruct((T, LANES), F32), jax.ShapeDtypeStruct((SUBLANES, LANES), F32)],
        compiler_params=_params(("arbitrary",)),
        name="merge_router",
    )(fox_o, hgrn_o, gates, x, wf, wh, wm, bm, g, b, wrh, wrl, rbias, tri)


def _dispatch_kernel(dest_ref, x_ref, xs_hbm, sem, *, tc, nj):
    def row_copy(r, d):
        return pltpu.make_async_copy(
            x_ref.at[pl.ds(pl.multiple_of(r * nj, nj), nj), :],
            xs_hbm.at[pl.ds(pl.multiple_of(d * nj, nj), nj), :], sem)

    def issue(r, _):
        row_copy(r, dest_ref[0, 0, 2 * r]).start()
        row_copy(r, dest_ref[0, 0, 2 * r + 1]).start()
        return 0

    lax.fori_loop(0, tc, issue, 0, unroll=8)

    def drain(r, _):
        row_copy(0, 0).wait()
        row_copy(0, 0).wait()
        return 0

    lax.fori_loop(0, tc, drain, 0, unroll=8)


def _dispatch(x1r, dest, n_tokens, nj):
    tc = _row_tile(n_tokens, 512)
    n_steps = n_tokens // tc
    return pl.pallas_call(
        functools.partial(_dispatch_kernel, tc=tc, nj=nj),
        grid=(n_steps,),
        in_specs=[pl.BlockSpec((1, 1, 2 * tc), lambda i: (i, 0, 0), memory_space=pltpu.SMEM),
                  pl.BlockSpec((tc * nj, LANES), lambda i: (i, 0))],
        out_specs=pl.BlockSpec(memory_space=pl.ANY),
        out_shape=jax.ShapeDtypeStruct((2 * n_tokens * nj, LANES), F32),
        scratch_shapes=[pltpu.SemaphoreType.DMA(())],
        compiler_params=pltpu.CompilerParams(dimension_semantics=("arbitrary",), has_side_effects=True),
        name="moe_dispatch",
    )(dest.reshape(n_steps, 1, 2 * tc), x1r)


def _experts_kernel(meta_ref, xs_ref, w1_ref, w3_ref, w2_ref, ys_ref, w1b_ref, w3b_ref, w2b_ref, *, tm, nj):
    w = pl.program_id(0)
    lo = meta_ref[2, w]
    hi = meta_ref[3, w]
    first = meta_ref[4, w]

    @pl.when(hi > lo)
    def _():
        @pl.when(meta_ref[5, w] == 1)
        def _():
            w1b_ref[...] = w1_ref[...].astype(BF16)
            w3b_ref[...] = w3_ref[...].astype(BF16)
            w2b_ref[...] = w2_ref[...].astype(BF16)

        x = jnp.concatenate([xs_ref[pl.ds(j, tm, stride=nj), :] for j in range(nj)], axis=-1).astype(BF16)
        h1 = _dot(x, w1b_ref[...])
        h3 = _dot(x, w3b_ref[...])
        h = (h1 * _sigmoid(h1) * h3).astype(BF16)
        y = _dot(h, w2b_ref[...])
        rows = lax.broadcasted_iota(I32, (tm, 1), 0)
        mine = (rows >= lo) & (rows < hi)

        @pl.when(first == 1)
        def _():
            for j in range(nj):
                ys_ref[pl.ds(j, tm, stride=nj), :] = jnp.where(mine, y[:, j * LANES:(j + 1) * LANES], 0.0)

        @pl.when(first == 0)
        def _():
            for j in range(nj):
                cur = ys_ref[pl.ds(j, tm, stride=nj), :]
                ys_ref[pl.ds(j, tm, stride=nj), :] = jnp.where(mine, y[:, j * LANES:(j + 1) * LANES], cur)


def _experts(meta, xs, w1, w3, w2, layer, tm, nj):
    n_items = meta.shape[1]
    D = nj * LANES
    dh = w1.shape[-1]
    grid_spec = pltpu.PrefetchScalarGridSpec(
        num_scalar_prefetch=1,
        grid=(n_items,),
        in_specs=[pl.BlockSpec((tm * nj, LANES), lambda w, m: (m[0, w], 0)),
                  pl.BlockSpec((None, None, D, dh), lambda w, m: (layer, m[1, w], 0, 0)),
                  pl.BlockSpec((None, None, D, dh), lambda w, m: (layer, m[1, w], 0, 0)),
                  pl.BlockSpec((None, None, dh, D), lambda w, m: (layer, m[1, w], 0, 0))],
        out_specs=pl.BlockSpec((tm * nj, LANES), lambda w, m: (m[0, w], 0)),
        scratch_shapes=[pltpu.VMEM((D, dh), BF16), pltpu.VMEM((D, dh), BF16), pltpu.VMEM((dh, D), BF16)],
    )
    return pl.pallas_call(
        functools.partial(_experts_kernel, tm=tm, nj=nj),
        grid_spec=grid_spec,
        out_shape=jax.ShapeDtypeStruct(xs.shape, F32),
        compiler_params=_params(("arbitrary",)),
        name="moe_experts",
    )(meta, xs, w1, w3, w2)


def _combine_kernel(dest_ref, ys_hbm, route_ref, x1_ref, g_ref, b_ref, o_ref, buf_ref, sem, *, tc, nj, alpha):
    def row_copy(d, slot, r):
        return pltpu.make_async_copy(
            ys_hbm.at[pl.ds(pl.multiple_of(d * nj, nj), nj), :],
            buf_ref.at[slot, pl.ds(pl.multiple_of(r * nj, nj), nj), :], sem)

    def issue(r, _):
        row_copy(dest_ref[0, 0, 2 * r], 0, r).start()
        row_copy(dest_ref[0, 0, 2 * r + 1], 1, r).start()
        return 0

    lax.fori_loop(0, tc, issue, 0, unroll=8)

    def drain(r, _):
        row_copy(0, 0, 0).wait()
        row_copy(0, 1, 0).wait()
        return 0

    lax.fori_loop(0, tc, drain, 0, unroll=8)

    gate1 = route_ref[:, 2:3]
    gate2 = route_ref[:, 3:4]
    y = jnp.concatenate(
        [gate1 * buf_ref[0, pl.ds(j, tc, stride=nj), :] + gate2 * buf_ref[1, pl.ds(j, tc, stride=nj), :]
         for j in range(nj)], axis=-1)
    o_ref[...] = _layer_norm(alpha * x1_ref[...] + y, g_ref[...], b_ref[...])


def _combine(dest, ys, route, x1, g, b, layer, alpha):
    T, D = x1.shape
    nj = D // LANES
    tc = _row_tile(T, 256)
    n_steps = T // tc
    row = lambda n: pl.BlockSpec((tc, n), lambda i: (i, 0))
    lw = lambda r, c: pl.BlockSpec((None, r, c), lambda i: (layer, 0, 0))
    return pl.pallas_call(
        functools.partial(_combine_kernel, tc=tc, nj=nj, alpha=alpha),
        grid=(n_steps,),
        in_specs=[pl.BlockSpec((1, 1, 2 * tc), lambda i: (i, 0, 0), memory_space=pltpu.SMEM),
                  pl.BlockSpec(memory_space=pl.ANY), row(LANES), row(D), lw(1, D), lw(1, D)],
        out_specs=row(D),
        out_shape=jax.ShapeDtypeStruct((T, D), F32),
        scratch_shapes=[pltpu.VMEM((2, tc * nj, LANES), F32), pltpu.SemaphoreType.DMA(())],
        compiler_params=_params(("arbitrary",)),
        name="moe_combine",
    )(dest.reshape(n_steps, 1, 2 * tc), ys, route, x1, g, b)


def _routing_tables(route, counts, tm, n_items):
    e = route[:, 0:2].astype(I32)
    rank = route[:, 4:6].astype(I32)
    cnt = counts[0, :N_EXPERTS].astype(I32)
    ends = jnp.cumsum(cnt)
    starts = ends - cnt
    dest = (starts[e] + rank).reshape(-1)

    first_tile = starts // tm
    last_tile = jnp.maximum(ends - 1, 0) // tm
    n_tiles_e = jnp.where(cnt > 0, last_tile - first_tile + 1, 0)
    item_end = jnp.cumsum(n_tiles_e)
    item_start = item_end - n_tiles_e
    n_real = item_end[-1]
    w = jnp.arange(n_items, dtype=I32)
    wc = jnp.minimum(w, n_real - 1)
    ex = jnp.sum((item_end[None, :] <= wc[:, None]).astype(I32), axis=1)
    tile = first_tile[ex] + (wc - item_start[ex])
    lo = jnp.clip(starts[ex] - tile * tm, 0, tm)
    hi = jnp.clip(ends[ex] - tile * tm, 0, tm)
    real = w < n_real
    hi = jnp.where(real, hi, lo)
    prev_tile = jnp.concatenate([jnp.full((1,), -1, I32), tile[:-1]])
    first = (real & (tile != prev_tile)).astype(I32)
    prev_ex = jnp.concatenate([jnp.full((1,), -1, I32), ex[:-1]])
    changed = (real & (ex != prev_ex)).astype(I32)
    meta = jnp.stack([tile, ex, lo, hi, first, changed]).astype(I32)
    return dest, meta


def kernel(x, ln_in_g, ln_in_b, w_in, b_in, w_fox_branch, hgrn_lb_logits, hgrn_norm_g, w_hgrn_branch,
           w_mix_out, b_mix_out, ln1_g, ln1_b, router_w, router_b, expert_w1, expert_w3, expert_w2,
           ln2_g, ln2_b):
    batch, seq, D = x.shape
    depth = w_in.shape[0]
    T = batch * seq
    nj = D // LANES
    alpha = float((2 * depth) ** 0.25)
    assert D % 512 == 0 and seq % HGRN_CHUNK == 0

    sizes = (FOX_WIDTH, FOX_WIDTH, FOX_WIDTH, FOX_HEADS, HGRN_WIDTH, HGRN_WIDTH, HGRN_WIDTH, HGRN_WIDTH, 2 * D)
    offs = [0]
    for s in sizes:
        offs.append(offs[-1] + s)
    col = lambda a, i: a[..., offs[i]:offs[i + 1]]
    order = (0, 1, 2, 4, 6, 7, 8, 5)
    pad_ff = lambda a: jnp.pad(col(a, 3), [(0, 0)] * (a.ndim - 1) + [(0, LANES - FOX_HEADS)])
    w_all = jnp.concatenate([col(w_in, i) for i in order] + [pad_ff(w_in)], axis=-1).astype(BF16)
    b_all = jnp.concatenate([col(b_in, i) for i in order] + [pad_ff(b_in)], axis=-1).astype(F32)[:, None, :]

    lb_p = jax.nn.softmax(hgrn_lb_logits.astype(F32), axis=0)
    lb_all = (jnp.cumsum(lb_p, axis=0) - lb_p[0]).reshape(depth * HGRN_PAIRS, 1, LANES)
    ng_all = hgrn_norm_g.astype(F32).reshape(depth * HGRN_PAIRS, 1, LANES)

    wf = w_fox_branch.astype(BF16)
    wh = w_hgrn_branch.astype(BF16)
    wm = w_mix_out.astype(BF16)
    r3 = lambda a: a.astype(F32)[:, None, :]
    bm, g1, b1, g2, b2 = r3(b_mix_out), r3(ln1_g), r3(ln1_b), r3(ln2_g), r3(ln2_b)
    rw = jnp.pad(router_w.astype(F32), ((0, 0), (0, LANES - N_EXPERTS)))
    wrh = rw.astype(BF16)
    wrl = (rw - wrh.astype(F32)).astype(BF16)
    rbias = jnp.pad(router_b.astype(F32), (0, LANES - N_EXPERTS)).reshape(1, LANES)

    tm_merge = _row_tile(T, 512)
    tri = jnp.tril(jnp.ones((tm_merge, tm_merge), BF16), k=-1)
    tm_exp = _row_tile(2 * T, 512)
    n_items = (2 * T) // tm_exp + N_EXPERTS - 1

    xc = _ln_in(x.reshape(T, D), ln_in_g.astype(F32), ln_in_b.astype(F32))
    for l in range(depth):
        fq, fk, fv, hq, hi, hg, gates, hf, ffp = _inproj(xc, w_all, b_all, l)
        fox_o = _fox_attention(fq, fk, fv, ffp, batch, seq)
        hgrn_o = _hgrn(hq, hf, hi, hg, lb_all, ng_all, l, batch, seq)
        x1, x1r, route, counts = _merge(fox_o, hgrn_o, gates, xc, wf, wh, wm, bm, g1, b1,
                                        wrh, wrl, rbias, tri, l, alpha)
        dest, meta = _routing_tables(route, counts, tm_exp, n_items)
        xs = _dispatch(x1r, dest, T, nj)
        ys = _experts(meta, xs, expert_w1, expert_w3, expert_w2, l, tm_exp, nj)
        xc = _combine(dest, ys, route, x1, g2, b2, l, alpha)
    return xc.reshape(batch, seq, D)
```

```python
import functools

import jax
import jax.numpy as jnp
from jax import lax
from jax.experimental import pallas as pl
from jax.experimental.pallas import tpu as pltpu

F32 = jnp.float32
BF16 = jnp.bfloat16
I32 = jnp.int32

LANES = 128
SUBLANES = 8
HEAD_DIM = 64
FOX_HEADS = 8
HGRN_HEADS = 8
HEADS_PER_VREG = LANES // HEAD_DIM
FOX_PAIRS = FOX_HEADS // HEADS_PER_VREG
HGRN_PAIRS = HGRN_HEADS // HEADS_PER_VREG
FOX_WIDTH = FOX_HEADS * HEAD_DIM
HGRN_WIDTH = HGRN_HEADS * HEAD_DIM
N_EXPERTS = 16
N_GROUPS = 4
EXPERTS_PER_GROUP = N_EXPERTS // N_GROUPS
TOP_K = 2
HGRN_CHUNK = 16
LN_EPS = 1e-5
RMS_EPS = 1e-6
MASK_VALUE = -1e30
LOG2E = 1.4426950408889634
VMEM_LIMIT = 56 * 1024 * 1024

_C_FQ = 0
_C_FK = _C_FQ + FOX_WIDTH
_C_FV = _C_FK + FOX_WIDTH
_C_HQ = _C_FV + FOX_WIDTH
_C_HI = _C_HQ + HGRN_WIDTH
_C_HG = _C_HI + HGRN_WIDTH
_C_GATES = _C_HG + HGRN_WIDTH


def _params(sem, vmem=VMEM_LIMIT):
    return pltpu.CompilerParams(dimension_semantics=sem, vmem_limit_bytes=vmem)


def _split3(x):
    hi = x.astype(BF16)
    r1 = x - hi.astype(F32)
    mid = r1.astype(BF16)
    lo = (r1 - mid.astype(F32)).astype(BF16)
    return hi, mid, lo


def _dot(a, b):
    return jnp.dot(a, b, preferred_element_type=F32)


def _dot_nt(a, b):
    return lax.dot_general(a, b, (((1,), (1,)), ((), ())), preferred_element_type=F32)


def _dot_tn(a, b):
    return lax.dot_general(a, b, (((0,), (0,)), ((), ())), preferred_element_type=F32)


def _dot_exact_lhs(m, x):
    hi, mid, lo = _split3(x)
    return _dot(m, hi) + _dot(m, mid) + _dot(m, lo)


def _dot_exact_rhs(x, m):
    hi, mid, lo = _split3(x)
    return _dot(hi, m) + _dot(mid, m) + _dot(lo, m)


def _sigmoid(x):
    return 1.0 / (1.0 + jnp.exp(-x))


def _layer_norm(x, g, b):
    mu = jnp.mean(x, axis=-1, keepdims=True)
    xc = x - mu
    var = jnp.mean(xc * xc, axis=-1, keepdims=True)
    return xc * lax.rsqrt(var + LN_EPS) * g + b


def _row_tile(n, want):
    t = min(n, want)
    assert n % t == 0, (n, t)
    return t


def _ln_kernel(x_ref, g_ref, b_ref, o_ref):
    o_ref[...] = _layer_norm(x_ref[...], g_ref[...], b_ref[...])


def _ln_in(x, g, b):
    T, D = x.shape
    tm = _row_tile(T, 512)
    return pl.pallas_call(
        _ln_kernel,
        grid=(T // tm,),
        in_specs=[pl.BlockSpec((tm, D), lambda i: (i, 0)),
                  pl.BlockSpec((1, D), lambda i: (0, 0)),
                  pl.BlockSpec((1, D), lambda i: (0, 0))],
        out_specs=pl.BlockSpec((tm, D), lambda i: (i, 0)),
        out_shape=jax.ShapeDtypeStruct((T, D), F32),
        compiler_params=_params(("parallel",)),
        name="ln_in",
    )(x, g.reshape(1, D), b.reshape(1, D))


def _fox_head_lanes(h):
    lane = lax.broadcasted_iota(I32, (1, LANES), 1)
    hh = h % HEADS_PER_VREG
    own = (lane >= hh * HEAD_DIM) & (lane < (hh + 1) * HEAD_DIM)
    e0 = (1 - hh) * HEAD_DIM
    return lane, own, e0


def _inproj_kernel(x_ref, w_ref, b_ref, tri_ref, qp_ref, kp_ref, vt_ref, hq_ref, hi_ref, hg_ref,
                   gates_ref, hf_ref, fc_ref, *, d_model, tiles_per_seq, cb):
    i = pl.program_id(0)
    tm = x_ref.shape[0]
    xb = x_ref[...].astype(BF16)

    def proj(c0, n):
        return _dot(xb, w_ref[:, c0:c0 + n]) + b_ref[:, c0:c0 + n]

    hq_ref[...] = proj(_C_HQ, HGRN_WIDTH).astype(BF16)
    hi_ref[...] = proj(_C_HI, HGRN_WIDTH).astype(BF16)
    hg_ref[...] = proj(_C_HG, HGRN_WIDTH).astype(BF16)
    n_gate_chunks = 2 * d_model // 512
    for c in range(n_gate_chunks):
        gates_ref[:, c * 512:(c + 1) * 512] = proj(_C_GATES + c * 512, 512).astype(BF16)
    c_hf = _C_GATES + 2 * d_model
    hf_ref[...] = proj(c_hf, HGRN_WIDTH)

    @pl.when(i % tiles_per_seq == 0)
    def _():
        fc_ref[...] = jnp.zeros(fc_ref.shape, F32)

    ff = proj(c_hf + HGRN_WIDTH, LANES)
    log_f = jnp.minimum(ff, 0.0) - jnp.log(1.0 + jnp.exp(-jnp.abs(ff)))
    f_cum = _dot_exact_lhs(tri_ref[...], log_f * LOG2E) + fc_ref[...]
    fc_ref[...] = f_cum[tm - 1:tm, :]

    q_all = proj(_C_FQ, FOX_WIDTH) * (HEAD_DIM ** -0.5 * LOG2E)
    k_all = proj(_C_FK, FOX_WIDTH)
    v_all = proj(_C_FV, FOX_WIDTH)
    for h in range(FOX_HEADS):
        p = h // HEADS_PER_VREG
        cols = slice(p * LANES, (p + 1) * LANES)
        lane, own, e0 = _fox_head_lanes(h)
        hi, mid, lo = (t.astype(F32) for t in _split3(f_cum[:, h:h + 1]))
        ext_q = jnp.where((lane >= e0 + 3) & (lane < e0 + 6), 1.0, 0.0)
        ext_q = jnp.where(lane == e0, hi, ext_q)
        ext_q = jnp.where(lane == e0 + 1, mid, ext_q)
        ext_q = jnp.where(lane == e0 + 2, lo, ext_q)
        ext_k = jnp.where((lane >= e0) & (lane < e0 + 3), 1.0, 0.0)
        ext_k = jnp.where(lane == e0 + 3, -hi, ext_k)
        ext_k = jnp.where(lane == e0 + 4, -mid, ext_k)
        ext_k = jnp.where(lane == e0 + 5, -lo, ext_k)
        qp_ref[h] = jnp.where(own, q_all[:, cols], ext_q).astype(BF16)
        kp = jnp.where(own, k_all[:, cols], ext_k).astype(BF16)
        vt = jnp.where(own, v_all[:, cols], 0.0).T
        for blk in range(tm // cb):
            kp_ref[h, blk] = kp[blk * cb:(blk + 1) * cb]
            vt_ref[h, blk] = vt[:, blk * cb:(blk + 1) * cb].astype(BF16)


def _inproj(x, w, b, tri, layer, seq, cb):
    T, D = x.shape
    n_all = w.shape[-1]
    tm = tri.shape[0]
    H = FOX_HEADS
    row = lambda n: pl.BlockSpec((tm, n), lambda i: (i, 0))
    outs = [(HGRN_WIDTH, BF16)] * 3 + [(2 * D, BF16), (HGRN_WIDTH, F32)]
    return pl.pallas_call(
        functools.partial(_inproj_kernel, d_model=D, tiles_per_seq=seq // tm, cb=cb),
        grid=(T // tm,),
        in_specs=[row(D),
                  pl.BlockSpec((None, D, n_all), lambda i: (layer, 0, 0), pipeline_mode=pl.Buffered(1)),
                  pl.BlockSpec((None, 1, n_all), lambda i: (layer, 0, 0), pipeline_mode=pl.Buffered(1)),
                  pl.BlockSpec((tm, tm), lambda i: (0, 0), pipeline_mode=pl.Buffered(1))],
        out_specs=[pl.BlockSpec((H, tm, LANES), lambda i: (0, i, 0)),
                   pl.BlockSpec((H, tm // cb, cb, LANES), lambda i: (0, i, 0, 0)),
                   pl.BlockSpec((H, tm // cb, LANES, cb), lambda i: (0, i, 0, 0))]
                  + [row(n) for n, _ in outs],
        out_shape=[jax.ShapeDtypeStruct((H, T, LANES), BF16),
                   jax.ShapeDtypeStruct((H, T // cb, cb, LANES), BF16),
                   jax.ShapeDtypeStruct((H, T // cb, LANES, cb), BF16)]
                  + [jax.ShapeDtypeStruct((T, n), dt) for n, dt in outs],
        scratch_shapes=[pltpu.VMEM((1, LANES), F32)],
        compiler_params=_params(("arbitrary",)),
        name="inproj",
    )(x, w, b, tri)


def _fox_kernel(qp_ref, kp_ref, vt_ref, o_ref, m_ref, l_ref, acc_ref, s_ref, p_ref, a_ref, *, tq, cb):
    i = pl.program_id(1)
    key_i = lax.broadcasted_iota(I32, (tq, tq), 0)
    qry_i = lax.broadcasted_iota(I32, (tq, tq), 1)
    causal = key_i <= qry_i
    for h in range(FOX_HEADS):
        m_ref[h] = jnp.full((1, tq), MASK_VALUE, F32)
        l_ref[h] = jnp.zeros((1, tq), F32)
        acc_ref[h] = jnp.zeros((LANES, tq), F32)

    def scores(j, par):
        for h in range(FOX_HEADS):
            s_ref[par * FOX_HEADS + h] = _dot_nt(kp_ref[h, j], qp_ref[h])

    def softmax(par, masked):
        for h in range(FOX_HEADS):
            st = s_ref[par * FOX_HEADS + h]
            if masked:
                st = jnp.where(causal, st, MASK_VALUE)
            m = m_ref[h]
            m_new = jnp.maximum(m, jnp.max(st, axis=0, keepdims=True))
            a = jnp.exp2(m - m_new)
            pt = jnp.exp2(st - m_new)
            m_ref[h] = m_new
            l_ref[h] = a * l_ref[h] + jnp.sum(pt, axis=0, keepdims=True)
            p_ref[par * FOX_HEADS + h] = pt.astype(BF16)
            a_ref[par * FOX_HEADS + h] = a

    def values(j, par):
        for h in range(FOX_HEADS):
            acc_ref[h] = a_ref[par * FOX_HEADS + h] * acc_ref[h] + _dot(vt_ref[h, j],
                                                                     p_ref[par * FOX_HEADS + h])

    for h in range(FOX_HEADS):
        p_ref[FOX_HEADS + h] = jnp.zeros((cb, tq), BF16)
        a_ref[FOX_HEADS + h] = jnp.ones((1, tq), F32)
    scores(0, 0)

    def trip(t, par):
        scores(t + 1, 1 - par)
        softmax(par, False)
        values(jnp.maximum(t - 1, 0), 1 - par)

    def two_trips(u, _):
        trip(2 * u, 0)
        trip(2 * u + 1, 1)
        return 0

    lax.fori_loop(0, i // 2, two_trips, 0)

    @pl.when(i % 2 == 1)
    def _():
        trip(i - 1, 0)
        softmax(1, True)
        values(i - 1, 0)
        values(i, 1)

    @pl.when(i % 2 == 0)
    def _():
        softmax(0, True)
        values(jnp.maximum(i - 1, 0), 1)
        values(i, 0)
    for p in range(FOX_PAIRS):
        h0 = p * HEADS_PER_VREG
        out_t = acc_ref[h0] / l_ref[h0]
        for h in range(h0 + 1, h0 + HEADS_PER_VREG):
            out_t = out_t + acc_ref[h] / l_ref[h]
        o_ref[:, p * LANES:(p + 1) * LANES] = out_t.T.astype(BF16)


def _fox_attention(qp, kp, vt, batch, seq, cb):
    H, T, _ = qp.shape
    tq = cb
    nq = seq // tq
    return pl.pallas_call(
        functools.partial(_fox_kernel, tq=tq, cb=cb),
        grid=(batch, nq),
        in_specs=[pl.BlockSpec((H, tq, LANES), lambda b, i: (0, b * nq + i, 0)),
                  pl.BlockSpec((H, nq, cb, LANES), lambda b, i: (0, b, 0, 0)),
                  pl.BlockSpec((H, nq, LANES, cb), lambda b, i: (0, b, 0, 0))],
        out_specs=pl.BlockSpec((tq, FOX_WIDTH), lambda b, i: (b * nq + i, 0)),
        out_shape=jax.ShapeDtypeStruct((T, FOX_WIDTH), BF16),
        scratch_shapes=[pltpu.VMEM((FOX_HEADS, 1, tq), F32),
                        pltpu.VMEM((FOX_HEADS, 1, tq), F32),
                        pltpu.VMEM((FOX_HEADS, LANES, tq), F32),
                        pltpu.VMEM((2 * FOX_HEADS, cb, tq), F32),
                        pltpu.VMEM((2 * FOX_HEADS, cb, tq), BF16),
                        pltpu.VMEM((2 * FOX_HEADS, 1, tq), F32)],
        compiler_params=_params(("parallel", "arbitrary")),
        name="fox_attention",
    )(qp, kp, vt)


def _hgrn_kernel(hq_ref, hf_ref, hi_ref, hg_ref, lb_ref, ng_ref, o_ref, ss_ref, cpad_ref, vpad_ref,
                 term_ref, score_ref, *, seq, rb, wl):
    ch = HGRN_CHUNK
    n_states = wl // LANES
    rr = lax.broadcasted_iota(I32, (rb, rb), 0)
    cc = lax.broadcasted_iota(I32, (rb, rb), 1)
    same_chunk = (rr // ch) == (cc // ch)
    cum_mat = jnp.concatenate([(same_chunk & (cc <= rr)).astype(BF16), same_chunk.astype(BF16)], axis=0)
    hr = lax.broadcasted_iota(I32, (wl, wl), 0) // HEAD_DIM
    hc = lax.broadcasted_iota(I32, (wl, wl), 1) // HEAD_DIM
    head_ones = (hr == hc).astype(BF16)
    head_mask = (hr == hc)[:LANES, :LANES].astype(F32)
    tmod = lax.broadcasted_iota(I32, (rb, wl), 0) % ch
    lb = lb_ref[...]
    ng = ng_ref[...]
    ss_ref[...] = jnp.zeros(ss_ref.shape, F32)
    cpad_ref[0:ch, :] = jnp.zeros((ch, wl), F32)
    vpad_ref[0:ch, :] = jnp.zeros((ch, wl), F32)

    def block(r, _):
        rows = pl.ds(pl.multiple_of(r * rb, rb), rb)
        z = hf_ref[rows, :]
        g = jnp.log(lb + (1.0 - lb) * _sigmoid(z))
        k = (1.0 - lb) * _sigmoid(-z)
        hq = hq_ref[rows, :].astype(F32)
        qs = hq * _sigmoid(hq)
        v = hi_ref[rows, :].astype(F32)
        cums = _dot_exact_lhs(cum_mat, g * LOG2E)
        b = cums[:rb]
        btot = cums[rb:]
        qd = (qs * jnp.exp2(b)).astype(BF16)
        kd = (k * jnp.exp2(btot - b)).astype(BF16)
        dec = jnp.exp2(btot)

        c = b - jnp.log2(k)
        cpad_ref[ch:ch + rb, :] = c
        vpad_ref[ch:ch + rb, :] = v
        term_ref[0:rb, :] = (qs * k).astype(BF16)
        for o in range(1, ch):
            cs = cpad_ref[ch - o:ch - o + rb, :]
            term = jnp.where(tmod >= o, qs * jnp.exp2(b - cs), 0.0)
            term_ref[o * rb:(o + 1) * rb, :] = term.astype(BF16)
        score_ref[...] = _dot(term_ref[...], head_ones)
        acc = score_ref[0:rb, :] * v
        for o in range(1, ch):
            acc = acc + score_ref[o * rb:(o + 1) * rb, :] * vpad_ref[ch - o:ch - o + rb, :]

        vb = v.astype(BF16)
        chunks = [slice(n * ch, (n + 1) * ch) for n in range(rb // ch)]
        inter_cols = []
        for sidx in range(n_states):
            ln = slice(sidx * LANES, (sidx + 1) * LANES)
            upds = [_dot_tn(vb[sl, ln], kd[sl, ln]) * head_mask for sl in chunks]
            ss = ss_ref[sidx]
            states = []
            for n, sl in enumerate(chunks):
                states.append(ss.astype(BF16))
                ss = ss * dec[n * ch:n * ch + 1, ln] + upds[n]
            ss_ref[sidx] = ss
            inter_cols.append(jnp.concatenate(
                [_dot_nt(qd[sl, ln], st) for sl, st in zip(chunks, states)], axis=0))
        o_blk = acc + jnp.concatenate(inter_cols, axis=1)
        ms = _dot_exact_rhs(o_blk * o_blk, head_ones) * (1.0 / HEAD_DIM)
        hg = hg_ref[rows, :].astype(F32)
        o_blk = o_blk * lax.rsqrt(ms + RMS_EPS) * ng * (hg * _sigmoid(hg))
        o_ref[rows, :] = o_blk.astype(BF16)
        return 0

    lax.fori_loop(0, seq // rb, block, 0, unroll=2 if (seq // rb) % 2 == 0 else 1)


def _hgrn(hq, hf, hi, hg, lb, ng, layer, batch, seq):
    T = hq.shape[0]
    rb = _row_tile(seq, 128)
    wl = 2 * LANES
    n_prog = HGRN_WIDTH // wl
    blk = lambda: pl.BlockSpec((seq, wl), lambda b, p: (b, p))
    par = lambda: pl.BlockSpec((None, 1, wl), lambda b, p: (layer * n_prog + p, 0, 0))
    return pl.pallas_call(
        functools.partial(_hgrn_kernel, seq=seq, rb=rb, wl=wl),
        grid=(batch, n_prog),
        in_specs=[blk(), blk(), blk(), blk(), par(), par()],
        out_specs=blk(),
        out_shape=jax.ShapeDtypeStruct((T, HGRN_WIDTH), BF16),
        scratch_shapes=[pltpu.VMEM((wl // LANES, LANES, LANES), F32),
                        pltpu.VMEM((HGRN_CHUNK + rb, wl), F32),
                        pltpu.VMEM((HGRN_CHUNK + rb, wl), F32),
                        pltpu.VMEM((HGRN_CHUNK * rb, wl), BF16),
                        pltpu.VMEM((HGRN_CHUNK * rb, wl), F32)],
        compiler_params=_params(("parallel", "parallel")),
        name="hgrn2",
    )(hq, hf, hi, hg, lb.reshape(-1, 1, wl), ng.reshape(-1, 1, wl))


def _merge_kernel(fo_ref, ho_ref, gates_ref, x_ref, wf_ref, wh_ref, wm_ref, bm_ref, g_ref, b_ref,
                  wrh_ref, wrl_ref, rb_ref, tri_ref, x1_ref, x1r_ref, route_ref, cnt_ref,
                  *, alpha, d_model):
    i = pl.program_id(0)
    tm = x_ref.shape[0]

    @pl.when(i == 0)
    def _():
        cnt_ref[...] = jnp.zeros(cnt_ref.shape, F32)

    y_fox = _dot(fo_ref[...], wf_ref[...])
    y_hgrn = _dot(ho_ref[...], wh_ref[...])
    g_fox = _sigmoid(gates_ref[:, :d_model].astype(F32))
    g_hgrn = _sigmoid(gates_ref[:, d_model:].astype(F32))
    mixed = _dot((g_fox * y_fox + g_hgrn * y_hgrn).astype(BF16), wm_ref[...]) + bm_ref[...]
    x1 = _layer_norm(alpha * x_ref[...] + mixed, g_ref[...], b_ref[...])
    x1_ref[...] = x1
    nj = d_model // LANES
    for j in range(nj):
        x1r_ref[pl.ds(j, tm, stride=nj), :] = x1[:, j * LANES:(j + 1) * LANES]

    xh = x1.astype(BF16)
    xl = (x1 - xh.astype(F32)).astype(BF16)
    logits = _dot(xh, wrh_ref[...]) + _dot(xl, wrh_ref[...]) + _dot(xh, wrl_ref[...]) + rb_ref[...]
    lane = lax.broadcasted_iota(I32, (tm, LANES), 1)
    lane_f = lane.astype(F32)
    neg = jnp.float32(-jnp.inf)
    lg = jnp.where(lane < N_EXPERTS, logits, neg)
    m1 = jnp.max(lg, axis=-1, keepdims=True)
    idx1 = jnp.min(jnp.where(lg == m1, lane_f, float(LANES)), axis=-1, keepdims=True).astype(I32)
    in_group = (lane // EXPERTS_PER_GROUP == idx1 // EXPERTS_PER_GROUP) & (lane < N_EXPERTS)
    lg2 = jnp.where(in_group & (lane != idx1), logits, neg)
    m2 = jnp.max(lg2, axis=-1, keepdims=True)
    idx2 = jnp.min(jnp.where(lg2 == m2, lane_f, float(LANES)), axis=-1, keepdims=True).astype(I32)
    e21 = jnp.exp(m2 - m1)
    gate1 = 1.0 / (1.0 + e21)
    gate2 = e21 / (1.0 + e21)

    oh1 = lane == idx1
    oh2 = lane == idx2
    oh = (oh1 | oh2).astype(F32)
    before = _dot(tri_ref[...], oh.astype(BF16)) + cnt_ref[0:1, :]
    rank1 = jnp.sum(jnp.where(oh1, before, 0.0), axis=-1, keepdims=True)
    rank2 = jnp.sum(jnp.where(oh2, before, 0.0), axis=-1, keepdims=True)
    cnt_ref[0:1, :] = cnt_ref[0:1, :] + jnp.sum(oh, axis=0, keepdims=True)

    route = jnp.where(lane == 0, idx1.astype(F32), 0.0)
    route = jnp.where(lane == 1, idx2.astype(F32), route)
    route = jnp.where(lane == 2, gate1, route)
    route = jnp.where(lane == 3, gate2, route)
    route = jnp.where(lane == 4, rank1, route)
    route = jnp.where(lane == 5, rank2, route)
    route_ref[...] = route


def _merge(fox_o, hgrn_o, gates, x, wf, wh, wm, bm, g, b, wrh, wrl, rbias, tri, layer, alpha):
    T, D = x.shape
    tm = tri.shape[0]
    nj = D // LANES
    row = lambda n: pl.BlockSpec((tm, n), lambda i: (i, 0))
    lw = lambda r, c: pl.BlockSpec((None, r, c), lambda i: (layer, 0, 0))
    cw = lambda r, c: pl.BlockSpec((r, c), lambda i: (0, 0))
    return pl.pallas_call(
        functools.partial(_merge_kernel, alpha=alpha, d_model=D),
        grid=(T // tm,),
        in_specs=[row(FOX_WIDTH), row(HGRN_WIDTH), row(2 * D), row(D),
                  lw(FOX_WIDTH, D), lw(HGRN_WIDTH, D), lw(D, D), lw(1, D), lw(1, D), lw(1, D),
                  cw(D, LANES), cw(D, LANES), cw(1, LANES), cw(tm, tm)],
        out_specs=[row(D), pl.BlockSpec((tm * nj, LANES), lambda i: (i, 0)), row(LANES),
                   pl.BlockSpec((SUBLANES, LANES), lambda i: (0, 0))],
        out_shape=[jax.ShapeDtypeStruct((T, D), F32), jax.ShapeDtypeStruct((T * nj, LANES), F32),
                   jax.ShapeDtypeStruct((T, LANES), F32), jax.ShapeDtypeStruct((SUBLANES, LANES), F32)],
        compiler_params=_params(("arbitrary",)),
        name="merge_router",
    )(fox_o, hgrn_o, gates, x, wf, wh, wm, bm, g, b, wrh, wrl, rbias, tri)


def _dispatch_kernel(dest_ref, x_ref, xs_hbm, sem, *, tc, nj):
    def row_copy(r, d):
        return pltpu.make_async_copy(
            x_ref.at[pl.ds(pl.multiple_of(r * nj, nj), nj), :],
            xs_hbm.at[pl.ds(pl.multiple_of(d * nj, nj), nj), :], sem)

    def issue(r, _):
        row_copy(r, dest_ref[0, 0, 2 * r]).start()
        row_copy(r, dest_ref[0, 0, 2 * r + 1]).start()
        return 0

    lax.fori_loop(0, tc, issue, 0, unroll=8)

    def drain(r, _):
        row_copy(0, 0).wait()
        row_copy(0, 0).wait()
        return 0

    lax.fori_loop(0, tc, drain, 0, unroll=8)


def _dispatch(x1r, dest, n_tokens, nj):
    tc = _row_tile(n_tokens, 512)
    n_steps = n_tokens // tc
    return pl.pallas_call(
        functools.partial(_dispatch_kernel, tc=tc, nj=nj),
        grid=(n_steps,),
        in_specs=[pl.BlockSpec((1, 1, 2 * tc), lambda i: (i, 0, 0), memory_space=pltpu.SMEM),
                  pl.BlockSpec((tc * nj, LANES), lambda i: (i, 0))],
        out_specs=pl.BlockSpec(memory_space=pl.ANY),
        out_shape=jax.ShapeDtypeStruct((2 * n_tokens * nj, LANES), F32),
        scratch_shapes=[pltpu.SemaphoreType.DMA(())],
        compiler_params=pltpu.CompilerParams(dimension_semantics=("arbitrary",), has_side_effects=True),
        name="moe_dispatch",
    )(dest.reshape(n_steps, 1, 2 * tc), x1r)


def _experts_kernel(meta_ref, xs_ref, w1_ref, w3_ref, w2_ref, ys_ref, w1b_ref, w3b_ref, w2b_ref, *, tm, nj):
    w = pl.program_id(0)
    lo = meta_ref[2, w]
    hi = meta_ref[3, w]
    first = meta_ref[4, w]

    @pl.when(hi > lo)
    def _():
        @pl.when(meta_ref[5, w] == 1)
        def _():
            w1b_ref[...] = w1_ref[...].astype(BF16)
            w3b_ref[...] = w3_ref[...].astype(BF16)
            w2b_ref[...] = w2_ref[...].astype(BF16)

        x = jnp.concatenate([xs_ref[pl.ds(j, tm, stride=nj), :] for j in range(nj)], axis=-1).astype(BF16)
        h1 = _dot(x, w1b_ref[...])
        h3 = _dot(x, w3b_ref[...])
        h = (h1 * _sigmoid(h1) * h3).astype(BF16)
        y = _dot(h, w2b_ref[...])
        rows = lax.broadcasted_iota(I32, (tm, 1), 0)
        mine = (rows >= lo) & (rows < hi)

        @pl.when(first == 1)
        def _():
            for j in range(nj):
                ys_ref[pl.ds(j, tm, stride=nj), :] = jnp.where(mine, y[:, j * LANES:(j + 1) * LANES], 0.0)

        @pl.when(first == 0)
        def _():
            for j in range(nj):
                cur = ys_ref[pl.ds(j, tm, stride=nj), :]
                ys_ref[pl.ds(j, tm, stride=nj), :] = jnp.where(mine, y[:, j * LANES:(j + 1) * LANES], cur)


def _experts(meta, xs, w1, w3, w2, layer, tm, nj):
    n_items = meta.shape[1]
    D = nj * LANES
    dh = w1.shape[-1]
    grid_spec = pltpu.PrefetchScalarGridSpec(
        num_scalar_prefetch=1,
        grid=(n_items,),
        in_specs=[pl.BlockSpec((tm * nj, LANES), lambda w, m: (m[0, w], 0)),
                  pl.BlockSpec((None, None, D, dh), lambda w, m: (layer, m[1, w], 0, 0)),
                  pl.BlockSpec((None, None, D, dh), lambda w, m: (layer, m[1, w], 0, 0)),
                  pl.BlockSpec((None, None, dh, D), lambda w, m: (layer, m[1, w], 0, 0))],
        out_specs=pl.BlockSpec((tm * nj, LANES), lambda w, m: (m[0, w], 0)),
        scratch_shapes=[pltpu.VMEM((D, dh), BF16), pltpu.VMEM((D, dh), BF16), pltpu.VMEM((dh, D), BF16)],
    )
    return pl.pallas_call(
        functools.partial(_experts_kernel, tm=tm, nj=nj),
        grid_spec=grid_spec,
        out_shape=jax.ShapeDtypeStruct(xs.shape, F32),
        compiler_params=_params(("arbitrary",)),
        name="moe_experts",
    )(meta, xs, w1, w3, w2)


def _combine_kernel(dest_ref, ys_hbm, route_ref, x1_ref, g_ref, b_ref, o_ref, buf_ref, sem, *, tc, nj, alpha):
    def row_copy(d, slot, r):
        return pltpu.make_async_copy(
            ys_hbm.at[pl.ds(pl.multiple_of(d * nj, nj), nj), :],
            buf_ref.at[slot, pl.ds(pl.multiple_of(r * nj, nj), nj), :], sem)

    def issue(r, _):
        row_copy(dest_ref[0, 0, 2 * r], 0, r).start()
        row_copy(dest_ref[0, 0, 2 * r + 1], 1, r).start()
        return 0

    lax.fori_loop(0, tc, issue, 0, unroll=8)

    def drain(r, _):
        row_copy(0, 0, 0).wait()
        row_copy(0, 1, 0).wait()
        return 0

    lax.fori_loop(0, tc, drain, 0, unroll=8)

    gate1 = route_ref[:, 2:3]
    gate2 = route_ref[:, 3:4]
    y = jnp.concatenate(
        [gate1 * buf_ref[0, pl.ds(j, tc, stride=nj), :] + gate2 * buf_ref[1, pl.ds(j, tc, stride=nj), :]
         for j in range(nj)], axis=-1)
    o_ref[...] = _layer_norm(alpha * x1_ref[...] + y, g_ref[...], b_ref[...])


def _combine(dest, ys, route, x1, g, b, layer, alpha):
    T, D = x1.shape
    nj = D // LANES
    tc = _row_tile(T, 256)
    n_steps = T // tc
    row = lambda n: pl.BlockSpec((tc, n), lambda i: (i, 0))
    lw = lambda r, c: pl.BlockSpec((None, r, c), lambda i: (layer, 0, 0))
    return pl.pallas_call(
        functools.partial(_combine_kernel, tc=tc, nj=nj, alpha=alpha),
        grid=(n_steps,),
        in_specs=[pl.BlockSpec((1, 1, 2 * tc), lambda i: (i, 0, 0), memory_space=pltpu.SMEM),
                  pl.BlockSpec(memory_space=pl.ANY), row(LANES), row(D), lw(1, D), lw(1, D)],
        out_specs=row(D),
        out_shape=jax.ShapeDtypeStruct((T, D), F32),
        scratch_shapes=[pltpu.VMEM((2, tc * nj, LANES), F32), pltpu.SemaphoreType.DMA(())],
        compiler_params=_params(("arbitrary",)),
        name="moe_combine",
    )(dest.reshape(n_steps, 1, 2 * tc), ys, route, x1, g, b)


def _routing_tables(route, counts, tm, n_items):
    e = route[:, 0:2].astype(I32)
    rank = route[:, 4:6].astype(I32)
    cnt = counts[0, :N_EXPERTS].astype(I32)
    ends = jnp.cumsum(cnt)
    starts = ends - cnt
    dest = (starts[e] + rank).reshape(-1)

    first_tile = starts // tm
    last_tile = jnp.maximum(ends - 1, 0) // tm
    n_tiles_e = jnp.where(cnt > 0, last_tile - first_tile + 1, 0)
    item_end = jnp.cumsum(n_tiles_e)
    item_start = item_end - n_tiles_e
    n_real = item_end[-1]
    w = jnp.arange(n_items, dtype=I32)
    wc = jnp.minimum(w, n_real - 1)
    ex = jnp.sum((item_end[None, :] <= wc[:, None]).astype(I32), axis=1)
    tile = first_tile[ex] + (wc - item_start[ex])
    lo = jnp.clip(starts[ex] - tile * tm, 0, tm)
    hi = jnp.clip(ends[ex] - tile * tm, 0, tm)
    real = w < n_real
    hi = jnp.where(real, hi, lo)
    prev_tile = jnp.concatenate([jnp.full((1,), -1, I32), tile[:-1]])
    first = (real & (tile != prev_tile)).astype(I32)
    prev_ex = jnp.concatenate([jnp.full((1,), -1, I32), ex[:-1]])
    changed = (real & (ex != prev_ex)).astype(I32)
    meta = jnp.stack([tile, ex, lo, hi, first, changed]).astype(I32)
    return dest, meta


def kernel(x, ln_in_g, ln_in_b, w_in, b_in, w_fox_branch, hgrn_lb_logits, hgrn_norm_g, w_hgrn_branch,
           w_mix_out, b_mix_out, ln1_g, ln1_b, router_w, router_b, expert_w1, expert_w3, expert_w2,
           ln2_g, ln2_b):
    batch, seq, D = x.shape
    depth = w_in.shape[0]
    T = batch * seq
    nj = D // LANES
    alpha = float((2 * depth) ** 0.25)
    assert D % 512 == 0 and seq % HGRN_CHUNK == 0

    sizes = (FOX_WIDTH, FOX_WIDTH, FOX_WIDTH, FOX_HEADS, HGRN_WIDTH, HGRN_WIDTH, HGRN_WIDTH, HGRN_WIDTH, 2 * D)
    offs = [0]
    for s in sizes:
        offs.append(offs[-1] + s)
    col = lambda a, i: a[..., offs[i]:offs[i + 1]]
    order = (0, 1, 2, 4, 6, 7, 8, 5)
    pad_ff = lambda a: jnp.pad(col(a, 3), [(0, 0)] * (a.ndim - 1) + [(0, LANES - FOX_HEADS)])
    w_all = jnp.concatenate([col(w_in, i) for i in order] + [pad_ff(w_in)], axis=-1).astype(BF16)
    b_all = jnp.concatenate([col(b_in, i) for i in order] + [pad_ff(b_in)], axis=-1).astype(F32)[:, None, :]

    lb_p = jax.nn.softmax(hgrn_lb_logits.astype(F32), axis=0)
    lb_all = (jnp.cumsum(lb_p, axis=0) - lb_p[0]).reshape(depth * HGRN_PAIRS, 1, LANES)
    ng_all = hgrn_norm_g.astype(F32).reshape(depth * HGRN_PAIRS, 1, LANES)

    wf = w_fox_branch.astype(BF16)
    wh = w_hgrn_branch.astype(BF16)
    wm = w_mix_out.astype(BF16)
    r3 = lambda a: a.astype(F32)[:, None, :]
    bm, g1, b1, g2, b2 = r3(b_mix_out), r3(ln1_g), r3(ln1_b), r3(ln2_g), r3(ln2_b)
    rw = jnp.pad(router_w.astype(F32), ((0, 0), (0, LANES - N_EXPERTS)))
    wrh = rw.astype(BF16)
    wrl = (rw - wrh.astype(F32)).astype(BF16)
    rbias = jnp.pad(router_b.astype(F32), (0, LANES - N_EXPERTS)).reshape(1, LANES)

    tm_merge = _row_tile(T, 512)
    tri = jnp.tril(jnp.ones((tm_merge, tm_merge), BF16), k=-1)
    tm_exp = _row_tile(2 * T, 512)
    n_items = (2 * T) // tm_exp + N_EXPERTS - 1
    tm_in = _row_tile(seq, 512)
    fox_block = _row_tile(tm_in, 256)
    tri_in = jnp.tril(jnp.ones((tm_in, tm_in), BF16))

    xc = _ln_in(x.reshape(T, D), ln_in_g.astype(F32), ln_in_b.astype(F32))
    for l in range(depth):
        qp, kp, vt, hq, hi, hg, gates, hf = _inproj(xc, w_all, b_all, tri_in, l, seq, fox_block)
        fox_o = _fox_attention(qp, kp, vt, batch, seq, fox_block)
        hgrn_o = _hgrn(hq, hf, hi, hg, lb_all, ng_all, l, batch, seq)
        x1, x1r, route, counts = _merge(fox_o, hgrn_o, gates, xc, wf, wh, wm, bm, g1, b1,
                                        wrh, wrl, rbias, tri, l, alpha)
        dest, meta = _routing_tables(route, counts, tm_exp, n_items)
        xs = _dispatch(x1r, dest, T, nj)
        ys = _experts(meta, xs, expert_w1, expert_w3, expert_w2, l, tm_exp, nj)
        xc = _combine(dest, ys, route, x1, g2, b2, l, alpha)
    return xc.reshape(batch, seq, D)
```

```python
import functools

import jax
import jax.numpy as jnp
from jax import lax
from jax.experimental import pallas as pl
from jax.experimental.pallas import tpu as pltpu

F32 = jnp.float32
BF16 = jnp.bfloat16
I32 = jnp.int32

LANES = 128
SUBLANES = 8
HEAD_DIM = 64
FOX_HEADS = 8
HGRN_HEADS = 8
HEADS_PER_VREG = LANES // HEAD_DIM
FOX_PAIRS = FOX_HEADS // HEADS_PER_VREG
HGRN_PAIRS = HGRN_HEADS // HEADS_PER_VREG
FOX_WIDTH = FOX_HEADS * HEAD_DIM
HGRN_WIDTH = HGRN_HEADS * HEAD_DIM
N_EXPERTS = 16
N_GROUPS = 4
EXPERTS_PER_GROUP = N_EXPERTS // N_GROUPS
TOP_K = 2
HGRN_CHUNK = 16
LN_EPS = 1e-5
RMS_EPS = 1e-6
MASK_VALUE = -1e30
LOG2E = 1.4426950408889634
VMEM_LIMIT = 56 * 1024 * 1024

_C_FQ = 0
_C_FK = _C_FQ + FOX_WIDTH
_C_FV = _C_FK + FOX_WIDTH
_C_HQ = _C_FV + FOX_WIDTH
_C_HI = _C_HQ + HGRN_WIDTH
_C_HG = _C_HI + HGRN_WIDTH
_C_GATES = _C_HG + HGRN_WIDTH


def _params(sem, vmem=VMEM_LIMIT):
    return pltpu.CompilerParams(dimension_semantics=sem, vmem_limit_bytes=vmem)


def _split3(x):
    hi = x.astype(BF16)
    r1 = x - hi.astype(F32)
    mid = r1.astype(BF16)
    lo = (r1 - mid.astype(F32)).astype(BF16)
    return hi, mid, lo


def _dot(a, b):
    return jnp.dot(a, b, preferred_element_type=F32)


def _dot_nt(a, b):
    return lax.dot_general(a, b, (((1,), (1,)), ((), ())), preferred_element_type=F32)


def _dot_tn(a, b):
    return lax.dot_general(a, b, (((0,), (0,)), ((), ())), preferred_element_type=F32)


def _dot_exact_lhs(m, x):
    hi, mid, lo = _split3(x)
    return _dot(m, hi) + _dot(m, mid) + _dot(m, lo)


def _dot_exact_rhs(x, m):
    hi, mid, lo = _split3(x)
    return _dot(hi, m) + _dot(mid, m) + _dot(lo, m)


def _sigmoid(x):
    return 1.0 / (1.0 + jnp.exp(-x))


def _layer_norm(x, g, b):
    mu = jnp.mean(x, axis=-1, keepdims=True)
    xc = x - mu
    var = jnp.mean(xc * xc, axis=-1, keepdims=True)
    return xc * lax.rsqrt(var + LN_EPS) * g + b


def _row_tile(n, want):
    t = min(n, want)
    assert n % t == 0, (n, t)
    return t


def _ln_kernel(x_ref, g_ref, b_ref, o_ref):
    o_ref[...] = _layer_norm(x_ref[...], g_ref[...], b_ref[...])


def _ln_in(x, g, b):
    T, D = x.shape
    tm = _row_tile(T, 512)
    return pl.pallas_call(
        _ln_kernel,
        grid=(T // tm,),
        in_specs=[pl.BlockSpec((tm, D), lambda i: (i, 0)),
                  pl.BlockSpec((1, D), lambda i: (0, 0)),
                  pl.BlockSpec((1, D), lambda i: (0, 0))],
        out_specs=pl.BlockSpec((tm, D), lambda i: (i, 0)),
        out_shape=jax.ShapeDtypeStruct((T, D), F32),
        compiler_params=_params(("parallel",)),
        name="ln_in",
    )(x, g.reshape(1, D), b.reshape(1, D))


def _fox_head_lanes(h):
    lane = lax.broadcasted_iota(I32, (1, LANES), 1)
    hh = h % HEADS_PER_VREG
    own = (lane >= hh * HEAD_DIM) & (lane < (hh + 1) * HEAD_DIM)
    e0 = (1 - hh) * HEAD_DIM
    return lane, own, e0


def _inproj_kernel(x_ref, w_ref, b_ref, tri_ref, qp_ref, kp_ref, vt_ref, hq_ref, hi_ref, hg_ref,
                   gates_ref, hf_ref, fc_ref, *, d_model, tiles_per_seq, cb):
    i = pl.program_id(0)
    tm = x_ref.shape[0]
    xb = x_ref[...].astype(BF16)

    def proj(c0, n):
        return _dot(xb, w_ref[:, c0:c0 + n]) + b_ref[:, c0:c0 + n]

    c_hf = _C_GATES + 2 * d_model

    @pl.when(i % tiles_per_seq == 0)
    def _():
        fc_ref[...] = jnp.zeros(fc_ref.shape, F32)

    ff = proj(c_hf + HGRN_WIDTH, LANES)
    log_f = jnp.minimum(ff, 0.0) - jnp.log(1.0 + jnp.exp(-jnp.abs(ff)))
    f_cum = _dot_exact_lhs(tri_ref[...], log_f * LOG2E) + fc_ref[...]
    fc_ref[...] = f_cum[tm - 1:tm, :]

    q_all = proj(_C_FQ, FOX_WIDTH) * (HEAD_DIM ** -0.5 * LOG2E)
    k_all = proj(_C_FK, FOX_WIDTH)
    v_all = proj(_C_FV, FOX_WIDTH)
    for h in range(FOX_HEADS):
        p = h // HEADS_PER_VREG
        cols = slice(p * LANES, (p + 1) * LANES)
        lane, own, e0 = _fox_head_lanes(h)
        hi, mid, lo = (t.astype(F32) for t in _split3(f_cum[:, h:h + 1]))
        ext_q = jnp.where((lane >= e0 + 3) & (lane < e0 + 6), 1.0, 0.0)
        ext_q = jnp.where(lane == e0, hi, ext_q)
        ext_q = jnp.where(lane == e0 + 1, mid, ext_q)
        ext_q = jnp.where(lane == e0 + 2, lo, ext_q)
        ext_k = jnp.where((lane >= e0) & (lane < e0 + 3), 1.0, 0.0)
        ext_k = jnp.where(lane == e0 + 3, -hi, ext_k)
        ext_k = jnp.where(lane == e0 + 4, -mid, ext_k)
        ext_k = jnp.where(lane == e0 + 5, -lo, ext_k)
        qp_ref[h] = jnp.where(own, q_all[:, cols], ext_q).astype(BF16)
        kp = jnp.where(own, k_all[:, cols], ext_k).astype(BF16)
        vt = jnp.where(own, v_all[:, cols], 0.0).T
        for blk in range(tm // cb):
            kp_ref[h, blk] = kp[blk * cb:(blk + 1) * cb]
            vt_ref[h, blk] = vt[:, blk * cb:(blk + 1) * cb].astype(BF16)

    hq_ref[...] = proj(_C_HQ, HGRN_WIDTH).astype(BF16)
    hi_ref[...] = proj(_C_HI, HGRN_WIDTH).astype(BF16)
    hg_ref[...] = proj(_C_HG, HGRN_WIDTH).astype(BF16)
    n_gate_chunks = 2 * d_model // 512
    for c in range(n_gate_chunks):
        gates_ref[:, c * 512:(c + 1) * 512] = proj(_C_GATES + c * 512, 512).astype(BF16)
    hf_ref[...] = proj(c_hf, HGRN_WIDTH)


def _inproj(x, w, b, tri, layer, seq, cb):
    T, D = x.shape
    n_all = w.shape[-1]
    tm = tri.shape[0]
    H = FOX_HEADS
    row = lambda n: pl.BlockSpec((tm, n), lambda i: (i, 0))
    outs = [(HGRN_WIDTH, BF16)] * 3 + [(2 * D, BF16), (HGRN_WIDTH, F32)]
    return pl.pallas_call(
        functools.partial(_inproj_kernel, d_model=D, tiles_per_seq=seq // tm, cb=cb),
        grid=(T // tm,),
        in_specs=[row(D),
                  pl.BlockSpec((None, D, n_all), lambda i: (layer, 0, 0), pipeline_mode=pl.Buffered(1)),
                  pl.BlockSpec((None, 1, n_all), lambda i: (layer, 0, 0), pipeline_mode=pl.Buffered(1)),
                  pl.BlockSpec((tm, tm), lambda i: (0, 0), pipeline_mode=pl.Buffered(1))],
        out_specs=[pl.BlockSpec((H, tm, LANES), lambda i: (0, i, 0)),
                   pl.BlockSpec((H, tm // cb, cb, LANES), lambda i: (0, i, 0, 0)),
                   pl.BlockSpec((H, tm // cb, LANES, cb), lambda i: (0, i, 0, 0))]
                  + [row(n) for n, _ in outs],
        out_shape=[jax.ShapeDtypeStruct((H, T, LANES), BF16),
                   jax.ShapeDtypeStruct((H, T // cb, cb, LANES), BF16),
                   jax.ShapeDtypeStruct((H, T // cb, LANES, cb), BF16)]
                  + [jax.ShapeDtypeStruct((T, n), dt) for n, dt in outs],
        scratch_shapes=[pltpu.VMEM((1, LANES), F32)],
        compiler_params=_params(("arbitrary",)),
        name="inproj",
    )(x, w, b, tri)


def _fox_kernel(qp_ref, kp_ref, vt_ref, o_ref, m_ref, l_ref, acc_ref, s_ref, p_ref, a_ref, *, tq, cb):
    i = pl.program_id(1)
    key_i = lax.broadcasted_iota(I32, (tq, tq), 0)
    qry_i = lax.broadcasted_iota(I32, (tq, tq), 1)
    causal = key_i <= qry_i
    for h in range(FOX_HEADS):
        m_ref[h] = jnp.full((1, tq), MASK_VALUE, F32)
        l_ref[h] = jnp.zeros((1, tq), F32)
        acc_ref[h] = jnp.zeros((LANES, tq), F32)

    def scores(j, par):
        for h in range(FOX_HEADS):
            s_ref[par * FOX_HEADS + h] = _dot_nt(kp_ref[h, j], qp_ref[h])

    def softmax(par, masked):
        for h in range(FOX_HEADS):
            st = s_ref[par * FOX_HEADS + h]
            if masked:
                st = jnp.where(causal, st, MASK_VALUE)
            m = m_ref[h]
            m_new = jnp.maximum(m, jnp.max(st, axis=0, keepdims=True))
            a = jnp.exp2(m - m_new)
            pt = jnp.exp2(st - m_new)
            m_ref[h] = m_new
            l_ref[h] = a * l_ref[h] + jnp.sum(pt, axis=0, keepdims=True)
            p_ref[par * FOX_HEADS + h] = pt.astype(BF16)
            a_ref[par * FOX_HEADS + h] = a

    def values(j, par):
        for h in range(FOX_HEADS):
            acc_ref[h] = a_ref[par * FOX_HEADS + h] * acc_ref[h] + _dot(vt_ref[h, j],
                                                                     p_ref[par * FOX_HEADS + h])

    for h in range(FOX_HEADS):
        p_ref[FOX_HEADS + h] = jnp.zeros((cb, tq), BF16)
        a_ref[FOX_HEADS + h] = jnp.ones((1, tq), F32)
    scores(0, 0)

    def trip(t, par):
        scores(t + 1, 1 - par)
        softmax(par, False)
        values(jnp.maximum(t - 1, 0), 1 - par)

    def two_trips(u, _):
        trip(2 * u, 0)
        trip(2 * u + 1, 1)
        return 0

    lax.fori_loop(0, i // 2, two_trips, 0)

    @pl.when(i % 2 == 1)
    def _():
        trip(i - 1, 0)
        softmax(1, True)
        values(i - 1, 0)
        values(i, 1)

    @pl.when(i % 2 == 0)
    def _():
        softmax(0, True)
        values(jnp.maximum(i - 1, 0), 1)
        values(i, 0)
    for p in range(FOX_PAIRS):
        h0 = p * HEADS_PER_VREG
        out_t = acc_ref[h0] / l_ref[h0]
        for h in range(h0 + 1, h0 + HEADS_PER_VREG):
            out_t = out_t + acc_ref[h] / l_ref[h]
        o_ref[:, p * LANES:(p + 1) * LANES] = out_t.T.astype(BF16)


def _fox_attention(qp, kp, vt, batch, seq, cb):
    H, T, _ = qp.shape
    tq = cb
    nq = seq // tq
    return pl.pallas_call(
        functools.partial(_fox_kernel, tq=tq, cb=cb),
        grid=(batch, nq),
        in_specs=[pl.BlockSpec((H, tq, LANES), lambda b, i: (0, b * nq + i, 0)),
                  pl.BlockSpec((H, nq, cb, LANES), lambda b, i: (0, b, 0, 0)),
                  pl.BlockSpec((H, nq, LANES, cb), lambda b, i: (0, b, 0, 0))],
        out_specs=pl.BlockSpec((tq, FOX_WIDTH), lambda b, i: (b * nq + i, 0)),
        out_shape=jax.ShapeDtypeStruct((T, FOX_WIDTH), BF16),
        scratch_shapes=[pltpu.VMEM((FOX_HEADS, 1, tq), F32),
                        pltpu.VMEM((FOX_HEADS, 1, tq), F32),
                        pltpu.VMEM((FOX_HEADS, LANES, tq), F32),
                        pltpu.VMEM((2 * FOX_HEADS, cb, tq), F32),
                        pltpu.VMEM((2 * FOX_HEADS, cb, tq), BF16),
                        pltpu.VMEM((2 * FOX_HEADS, 1, tq), F32)],
        compiler_params=_params(("parallel", "arbitrary")),
        name="fox_attention",
    )(qp, kp, vt)


def _hgrn_kernel(hq_ref, hf_ref, hi_ref, hg_ref, lb_ref, ng_ref, o_ref, ss_ref, cpad_ref, vpad_ref,
                 term_ref, score_ref, *, seq, rb, wl):
    ch = HGRN_CHUNK
    n_states = wl // LANES
    rr = lax.broadcasted_iota(I32, (rb, rb), 0)
    cc = lax.broadcasted_iota(I32, (rb, rb), 1)
    same_chunk = (rr // ch) == (cc // ch)
    cum_mat = jnp.concatenate([(same_chunk & (cc <= rr)).astype(BF16), same_chunk.astype(BF16)], axis=0)
    hr = lax.broadcasted_iota(I32, (wl, wl), 0) // HEAD_DIM
    hc = lax.broadcasted_iota(I32, (wl, wl), 1) // HEAD_DIM
    head_ones = (hr == hc).astype(BF16)
    head_mask = (hr == hc)[:LANES, :LANES].astype(F32)
    tmod = lax.broadcasted_iota(I32, (rb, wl), 0) % ch
    lb = lb_ref[...]
    ng = ng_ref[...]
    ss_ref[...] = jnp.zeros(ss_ref.shape, F32)
    cpad_ref[0:ch, :] = jnp.zeros((ch, wl), F32)
    vpad_ref[0:ch, :] = jnp.zeros((ch, wl), F32)

    def block(r, _):
        rows = pl.ds(pl.multiple_of(r * rb, rb), rb)
        z = hf_ref[rows, :]
        g = jnp.log(lb + (1.0 - lb) * _sigmoid(z))
        k = (1.0 - lb) * _sigmoid(-z)
        hq = hq_ref[rows, :].astype(F32)
        qs = hq * _sigmoid(hq)
        v = hi_ref[rows, :].astype(F32)
        cums = _dot_exact_lhs(cum_mat, g * LOG2E)
        b = cums[:rb]
        btot = cums[rb:]
        qd = (qs * jnp.exp2(b)).astype(BF16)
        kd = (k * jnp.exp2(btot - b)).astype(BF16)
        dec = jnp.exp2(btot)

        c = b - jnp.log2(k)
        cpad_ref[ch:ch + rb, :] = c
        vpad_ref[ch:ch + rb, :] = v
        term_ref[0:rb, :] = (qs * k).astype(BF16)
        for o in range(1, ch):
            cs = cpad_ref[ch - o:ch - o + rb, :]
            term = jnp.where(tmod >= o, qs * jnp.exp2(b - cs), 0.0)
            term_ref[o * rb:(o + 1) * rb, :] = term.astype(BF16)
        score_ref[...] = _dot(term_ref[...], head_ones)
        acc = score_ref[0:rb, :] * v
        for o in range(1, ch):
            acc = acc + score_ref[o * rb:(o + 1) * rb, :] * vpad_ref[ch - o:ch - o + rb, :]

        vb = v.astype(BF16)
        chunks = [slice(n * ch, (n + 1) * ch) for n in range(rb // ch)]
        inter_cols = []
        for sidx in range(n_states):
            ln = slice(sidx * LANES, (sidx + 1) * LANES)
            upds = [_dot_tn(vb[sl, ln], kd[sl, ln]) * head_mask for sl in chunks]
            ss = ss_ref[sidx]
            states = []
            for n, sl in enumerate(chunks):
                states.append(ss.astype(BF16))
                ss = ss * dec[n * ch:n * ch + 1, ln] + upds[n]
            ss_ref[sidx] = ss
            inter_cols.append(jnp.concatenate(
                [_dot_nt(qd[sl, ln], st) for sl, st in zip(chunks, states)], axis=0))
        o_blk = acc + jnp.concatenate(inter_cols, axis=1)
        ms = _dot_exact_rhs(o_blk * o_blk, head_ones) * (1.0 / HEAD_DIM)
        hg = hg_ref[rows, :].astype(F32)
        o_blk = o_blk * lax.rsqrt(ms + RMS_EPS) * ng * (hg * _sigmoid(hg))
        o_ref[rows, :] = o_blk.astype(BF16)
        return 0

    lax.fori_loop(0, seq // rb, block, 0, unroll=2 if (seq // rb) % 2 == 0 else 1)


def _hgrn(hq, hf, hi, hg, lb, ng, layer, batch, seq):
    T = hq.shape[0]
    rb = _row_tile(seq, 128)
    wl = 2 * LANES
    n_prog = HGRN_WIDTH // wl
    blk = lambda: pl.BlockSpec((seq, wl), lambda b, p: (b, p))
    par = lambda: pl.BlockSpec((None, 1, wl), lambda b, p: (layer * n_prog + p, 0, 0))
    return pl.pallas_call(
        functools.partial(_hgrn_kernel, seq=seq, rb=rb, wl=wl),
        grid=(batch, n_prog),
        in_specs=[blk(), blk(), blk(), blk(), par(), par()],
        out_specs=blk(),
        out_shape=jax.ShapeDtypeStruct((T, HGRN_WIDTH), BF16),
        scratch_shapes=[pltpu.VMEM((wl // LANES, LANES, LANES), F32),
                        pltpu.VMEM((HGRN_CHUNK + rb, wl), F32),
                        pltpu.VMEM((HGRN_CHUNK + rb, wl), F32),
                        pltpu.VMEM((HGRN_CHUNK * rb, wl), BF16),
                        pltpu.VMEM((HGRN_CHUNK * rb, wl), F32)],
        compiler_params=_params(("parallel", "parallel")),
        name="hgrn2",
    )(hq, hf, hi, hg, lb.reshape(-1, 1, wl), ng.reshape(-1, 1, wl))


def _merge_kernel(fo_ref, ho_ref, gates_ref, x_ref, wf_ref, wh_ref, wm_ref, bm_ref, g_ref, b_ref,
                  wrh_ref, wrl_ref, rb_ref, tri_ref, x1_ref, x1r_ref, route_ref, cnt_ref,
                  *, alpha, d_model):
    i = pl.program_id(0)
    tm = x_ref.shape[0]

    @pl.when(i == 0)
    def _():
        cnt_ref[...] = jnp.zeros(cnt_ref.shape, F32)

    y_fox = _dot(fo_ref[...], wf_ref[...])
    y_hgrn = _dot(ho_ref[...], wh_ref[...])
    g_fox = _sigmoid(gates_ref[:, :d_model].astype(F32))
    g_hgrn = _sigmoid(gates_ref[:, d_model:].astype(F32))
    mixed = _dot((g_fox * y_fox + g_hgrn * y_hgrn).astype(BF16), wm_ref[...]) + bm_ref[...]
    x1 = _layer_norm(alpha * x_ref[...] + mixed, g_ref[...], b_ref[...])
    x1_ref[...] = x1
    nj = d_model // LANES
    for j in range(nj):
        x1r_ref[pl.ds(j, tm, stride=nj), :] = x1[:, j * LANES:(j + 1) * LANES]

    xh = x1.astype(BF16)
    xl = (x1 - xh.astype(F32)).astype(BF16)
    logits = _dot(xh, wrh_ref[...]) + _dot(xl, wrh_ref[...]) + _dot(xh, wrl_ref[...]) + rb_ref[...]
    lane = lax.broadcasted_iota(I32, (tm, LANES), 1)
    lane_f = lane.astype(F32)
    neg = jnp.float32(-jnp.inf)
    lg = jnp.where(lane < N_EXPERTS, logits, neg)
    m1 = jnp.max(lg, axis=-1, keepdims=True)
    idx1 = jnp.min(jnp.where(lg == m1, lane_f, float(LANES)), axis=-1, keepdims=True).astype(I32)
    in_group = (lane // EXPERTS_PER_GROUP == idx1 // EXPERTS_PER_GROUP) & (lane < N_EXPERTS)
    lg2 = jnp.where(in_group & (lane != idx1), logits, neg)
    m2 = jnp.max(lg2, axis=-1, keepdims=True)
    idx2 = jnp.min(jnp.where(lg2 == m2, lane_f, float(LANES)), axis=-1, keepdims=True).astype(I32)
    e21 = jnp.exp(m2 - m1)
    gate1 = 1.0 / (1.0 + e21)
    gate2 = e21 / (1.0 + e21)

    oh1 = lane == idx1
    oh2 = lane == idx2
    oh = (oh1 | oh2).astype(F32)
    before = _dot(tri_ref[...], oh.astype(BF16)) + cnt_ref[0:1, :]
    rank1 = jnp.sum(jnp.where(oh1, before, 0.0), axis=-1, keepdims=True)
    rank2 = jnp.sum(jnp.where(oh2, before, 0.0), axis=-1, keepdims=True)
    cnt_ref[0:1, :] = cnt_ref[0:1, :] + jnp.sum(oh, axis=0, keepdims=True)

    route = jnp.where(lane == 0, idx1.astype(F32), 0.0)
    route = jnp.where(lane == 1, idx2.astype(F32), route)
    route = jnp.where(lane == 2, gate1, route)
    route = jnp.where(lane == 3, gate2, route)
    route = jnp.where(lane == 4, rank1, route)
    route = jnp.where(lane == 5, rank2, route)
    route_ref[...] = route


def _merge(fox_o, hgrn_o, gates, x, wf, wh, wm, bm, g, b, wrh, wrl, rbias, tri, layer, alpha):
    T, D = x.shape
    tm = tri.shape[0]
    nj = D // LANES
    row = lambda n: pl.BlockSpec((tm, n), lambda i: (i, 0))
    lw = lambda r, c: pl.BlockSpec((None, r, c), lambda i: (layer, 0, 0))
    cw = lambda r, c: pl.BlockSpec((r, c), lambda i: (0, 0))
    return pl.pallas_call(
        functools.partial(_merge_kernel, alpha=alpha, d_model=D),
        grid=(T // tm,),
        in_specs=[row(FOX_WIDTH), row(HGRN_WIDTH), row(2 * D), row(D),
                  lw(FOX_WIDTH, D), lw(HGRN_WIDTH, D), lw(D, D), lw(1, D), lw(1, D), lw(1, D),
                  cw(D, LANES), cw(D, LANES), cw(1, LANES), cw(tm, tm)],
        out_specs=[row(D), pl.BlockSpec((tm * nj, LANES), lambda i: (i, 0)), row(LANES),
                   pl.BlockSpec((SUBLANES, LANES), lambda i: (0, 0))],
        out_shape=[jax.ShapeDtypeStruct((T, D), F32), jax.ShapeDtypeStruct((T * nj, LANES), F32),
                   jax.ShapeDtypeStruct((T, LANES), F32), jax.ShapeDtypeStruct((SUBLANES, LANES), F32)],
        compiler_params=_params(("arbitrary",)),
        name="merge_router",
    )(fox_o, hgrn_o, gates, x, wf, wh, wm, bm, g, b, wrh, wrl, rbias, tri)


def _dispatch_kernel(dest_ref, x_ref, xs_hbm, inv_ref, sem, *, tc, nj):
    i = pl.program_id(0)

    def row_copy(r, d):
        return pltpu.make_async_copy(
            x_ref.at[pl.ds(pl.multiple_of(r * nj, nj), nj), :],
            xs_hbm.at[pl.ds(pl.multiple_of(d * nj, nj), nj), :], sem)

    def issue(r, _):
        d0 = dest_ref[0, 0, 2 * r]
        d1 = dest_ref[0, 0, 2 * r + 1]
        assignment = 2 * (i * tc + r)
        inv_ref[d0] = assignment
        inv_ref[d1] = assignment + 1
        row_copy(r, d0).start()
        row_copy(r, d1).start()
        return 0

    lax.fori_loop(0, tc, issue, 0, unroll=8)

    def drain(r, _):
        row_copy(0, 0).wait()
        row_copy(0, 0).wait()
        return 0

    lax.fori_loop(0, tc, drain, 0, unroll=8)


def _dispatch(x1r, dest, n_tokens, nj):
    tc = _row_tile(n_tokens, 512)
    n_steps = n_tokens // tc
    return pl.pallas_call(
        functools.partial(_dispatch_kernel, tc=tc, nj=nj),
        grid=(n_steps,),
        in_specs=[pl.BlockSpec((1, 1, 2 * tc), lambda i: (i, 0, 0), memory_space=pltpu.SMEM),
                  pl.BlockSpec((tc * nj, LANES), lambda i: (i, 0))],
        out_specs=[pl.BlockSpec(memory_space=pl.ANY), pl.BlockSpec(memory_space=pltpu.SMEM)],
        out_shape=[jax.ShapeDtypeStruct((2 * n_tokens * nj, LANES), F32),
                   jax.ShapeDtypeStruct((2 * n_tokens,), I32)],
        scratch_shapes=[pltpu.SemaphoreType.DMA(())],
        compiler_params=pltpu.CompilerParams(dimension_semantics=("arbitrary",), has_side_effects=True),
        name="moe_dispatch",
    )(dest.reshape(n_steps, 1, 2 * tc), x1r)


def _experts_kernel(meta_ref, inv_ref, xs_ref, w1_ref, w3_ref, w2_ref, y2_hbm, ybuf_ref, sem,
                    *, tm, nj, n_assign):
    w = pl.program_id(0)
    lo = meta_ref[2, w]
    hi = meta_ref[3, w]
    prev = jnp.maximum(w - 1, 0)
    prev_lo = jnp.where(w == 0, 0, meta_ref[2, prev])
    prev_hi = jnp.where(w == 0, 0, meta_ref[3, prev])
    cur_par = w % 2
    prev_par = 1 - cur_par

    def row_copy(par, r, slot):
        return pltpu.make_async_copy(
            ybuf_ref.at[par, pl.ds(pl.multiple_of(r * nj, nj), nj), :],
            y2_hbm.at[pl.ds(pl.multiple_of(slot * nj, nj), nj), :], sem.at[par])

    def send_previous_rows():
        for r in range(tm):
            owned = (prev_lo <= r) & (prev_hi > r)
            slot = jnp.where(owned, inv_ref[0, 0, r], n_assign + prev_par * tm + r)
            row_copy(prev_par, r, slot).start()

    def wait_rows(par):
        pltpu.make_async_copy(ybuf_ref.at[par], y2_hbm.at[pl.ds(0, tm * nj), :], sem.at[par]).wait()

    @pl.when(w == 0)
    def _():
        ybuf_ref[...] = jnp.zeros(ybuf_ref.shape, F32)
        spare0 = pltpu.make_async_copy(ybuf_ref.at[0], y2_hbm.at[pl.ds(n_assign * nj, tm * nj), :], sem.at[0])
        spare0.start()
        spare0.wait()

    @pl.when(hi > lo)
    def _():
        send_previous_rows()
        x = jnp.concatenate([xs_ref[pl.ds(j, tm, stride=nj), :] for j in range(nj)], axis=-1).astype(BF16)
        h1 = _dot(x, w1_ref[...].astype(BF16))
        h3 = _dot(x, w3_ref[...].astype(BF16))
        h = (h1 * _sigmoid(h1) * h3).astype(BF16)
        y = _dot(h, w2_ref[...].astype(BF16))

        @pl.when(w >= 1)
        def _():
            wait_rows(cur_par)

        for j in range(nj):
            ybuf_ref[cur_par, pl.ds(j, tm, stride=nj), :] = y[:, j * LANES:(j + 1) * LANES]

    @pl.when((hi == lo) & (prev_hi > prev_lo))
    def _():
        send_previous_rows()
        wait_rows(0)
        wait_rows(1)


def _experts(meta, inv, xs, w1, w3, w2, layer, tm, nj):
    n_items = meta.shape[1]
    D = nj * LANES
    dh = w1.shape[-1]
    n_assign = xs.shape[0] // nj
    prev_tile = lambda w, m: (m[0, jnp.maximum(w - 1, 0)], 0, 0)
    grid_spec = pltpu.PrefetchScalarGridSpec(
        num_scalar_prefetch=1,
        grid=(n_items,),
        in_specs=[pl.BlockSpec((1, 1, tm), prev_tile, memory_space=pltpu.SMEM),
                  pl.BlockSpec((tm * nj, LANES), lambda w, m: (m[0, w], 0)),
                  pl.BlockSpec((None, None, D, dh), lambda w, m: (layer, m[1, w], 0, 0)),
                  pl.BlockSpec((None, None, D, dh), lambda w, m: (layer, m[1, w], 0, 0)),
                  pl.BlockSpec((None, None, dh, D), lambda w, m: (layer, m[1, w], 0, 0))],
        out_specs=pl.BlockSpec(memory_space=pl.ANY),
        scratch_shapes=[pltpu.VMEM((2, tm * nj, LANES), F32), pltpu.SemaphoreType.DMA((2,))],
    )
    return pl.pallas_call(
        functools.partial(_experts_kernel, tm=tm, nj=nj, n_assign=n_assign),
        grid_spec=grid_spec,
        out_shape=jax.ShapeDtypeStruct(((n_assign + 2 * tm) * nj, LANES), F32),
        compiler_params=pltpu.CompilerParams(dimension_semantics=("arbitrary",), vmem_limit_bytes=VMEM_LIMIT,
                                             has_side_effects=True),
        name="moe_experts",
    )(meta, inv.reshape(n_assign // tm, 1, tm), xs, w1, w3, w2)


def _combine_kernel(y2_ref, route_ref, x1_ref, g_ref, b_ref, o_ref, *, tc, nj, alpha):
    gate1 = route_ref[:, 2:3]
    gate2 = route_ref[:, 3:4]
    y = jnp.concatenate(
        [gate1 * y2_ref[pl.ds(j, tc, stride=TOP_K * nj), :] + gate2 * y2_ref[pl.ds(nj + j, tc, stride=TOP_K * nj), :]
         for j in range(nj)], axis=-1)
    o_ref[...] = _layer_norm(alpha * x1_ref[...] + y, g_ref[...], b_ref[...])


def _combine(y2, route, x1, g, b, layer, alpha):
    T, D = x1.shape
    nj = D // LANES
    tc = _row_tile(T, 256)
    row = lambda n: pl.BlockSpec((tc, n), lambda i: (i, 0))
    lw = lambda r, c: pl.BlockSpec((None, r, c), lambda i: (layer, 0, 0))
    return pl.pallas_call(
        functools.partial(_combine_kernel, tc=tc, nj=nj, alpha=alpha),
        grid=(T // tc,),
        in_specs=[pl.BlockSpec((tc * TOP_K * nj, LANES), lambda i: (i, 0)), row(LANES), row(D), lw(1, D), lw(1, D)],
        out_specs=row(D),
        out_shape=jax.ShapeDtypeStruct((T, D), F32),
        compiler_params=_params(("parallel",)),
        name="moe_combine",
    )(y2, route, x1, g, b)


def _routing_tables(route, counts, tm, n_items):
    e = route[:, 0:2].astype(I32)
    rank = route[:, 4:6].astype(I32)
    cnt = counts[0, :N_EXPERTS].astype(I32)
    ends = jnp.cumsum(cnt)
    starts = ends - cnt
    dest = (starts[e] + rank).reshape(-1)

    first_tile = starts // tm
    last_tile = jnp.maximum(ends - 1, 0) // tm
    n_tiles_e = jnp.where(cnt > 0, last_tile - first_tile + 1, 0)
    item_end = jnp.cumsum(n_tiles_e)
    item_start = item_end - n_tiles_e
    n_real = item_end[-1]
    w = jnp.arange(n_items, dtype=I32)
    wc = jnp.minimum(w, n_real - 1)
    ex = jnp.sum((item_end[None, :] <= wc[:, None]).astype(I32), axis=1)
    tile = first_tile[ex] + (wc - item_start[ex])
    lo = jnp.clip(starts[ex] - tile * tm, 0, tm)
    hi = jnp.clip(ends[ex] - tile * tm, 0, tm)
    real = w < n_real
    hi = jnp.where(real, hi, lo)
    meta = jnp.stack([tile, ex, lo, hi]).astype(I32)
    return dest, meta


def kernel(x, ln_in_g, ln_in_b, w_in, b_in, w_fox_branch, hgrn_lb_logits, hgrn_norm_g, w_hgrn_branch,
           w_mix_out, b_mix_out, ln1_g, ln1_b, router_w, router_b, expert_w1, expert_w3, expert_w2,
           ln2_g, ln2_b):
    batch, seq, D = x.shape
    depth = w_in.shape[0]
    T = batch * seq
    nj = D // LANES
    alpha = float((2 * depth) ** 0.25)
    assert D % 512 == 0 and seq % HGRN_CHUNK == 0

    sizes = (FOX_WIDTH, FOX_WIDTH, FOX_WIDTH, FOX_HEADS, HGRN_WIDTH, HGRN_WIDTH, HGRN_WIDTH, HGRN_WIDTH, 2 * D)
    offs = [0]
    for s in sizes:
        offs.append(offs[-1] + s)
    col = lambda a, i: a[..., offs[i]:offs[i + 1]]
    order = (0, 1, 2, 4, 6, 7, 8, 5)
    pad_ff = lambda a: jnp.pad(col(a, 3), [(0, 0)] * (a.ndim - 1) + [(0, LANES - FOX_HEADS)])
    w_all = jnp.concatenate([col(w_in, i) for i in order] + [pad_ff(w_in)], axis=-1).astype(BF16)
    b_all = jnp.concatenate([col(b_in, i) for i in order] + [pad_ff(b_in)], axis=-1).astype(F32)[:, None, :]

    lb_p = jax.nn.softmax(hgrn_lb_logits.astype(F32), axis=0)
    lb_all = (jnp.cumsum(lb_p, axis=0) - lb_p[0]).reshape(depth * HGRN_PAIRS, 1, LANES)
    ng_all = hgrn_norm_g.astype(F32).reshape(depth * HGRN_PAIRS, 1, LANES)

    wf = w_fox_branch.astype(BF16)
    wh = w_hgrn_branch.astype(BF16)
    wm = w_mix_out.astype(BF16)
    r3 = lambda a: a.astype(F32)[:, None, :]
    bm, g1, b1, g2, b2 = r3(b_mix_out), r3(ln1_g), r3(ln1_b), r3(ln2_g), r3(ln2_b)
    rw = jnp.pad(router_w.astype(F32), ((0, 0), (0, LANES - N_EXPERTS)))
    wrh = rw.astype(BF16)
    wrl = (rw - wrh.astype(F32)).astype(BF16)
    rbias = jnp.pad(router_b.astype(F32), (0, LANES - N_EXPERTS)).reshape(1, LANES)

    tm_merge = _row_tile(T, 512)
    tri = jnp.tril(jnp.ones((tm_merge, tm_merge), BF16), k=-1)
    tm_exp = _row_tile(2 * T, 512)
    n_items = (2 * T) // tm_exp + N_EXPERTS
    tm_in = _row_tile(seq, 512)
    fox_block = _row_tile(tm_in, 256)
    tri_in = jnp.tril(jnp.ones((tm_in, tm_in), BF16))

    xc = _ln_in(x.reshape(T, D), ln_in_g.astype(F32), ln_in_b.astype(F32))
    for l in range(depth):
        qp, kp, vt, hq, hi, hg, gates, hf = _inproj(xc, w_all, b_all, tri_in, l, seq, fox_block)
        fox_o = _fox_attention(qp, kp, vt, batch, seq, fox_block)
        hgrn_o = _hgrn(hq, hf, hi, hg, lb_all, ng_all, l, batch, seq)
        x1, x1r, route, counts = _merge(fox_o, hgrn_o, gates, xc, wf, wh, wm, bm, g1, b1,
                                        wrh, wrl, rbias, tri, l, alpha)
        dest, meta = _routing_tables(route, counts, tm_exp, n_items)
        xs, inv = _dispatch(x1r, dest, T, nj)
        y2 = _experts(meta, inv, xs, expert_w1, expert_w3, expert_w2, l, tm_exp, nj)
        xc = _combine(y2, route, x1, g2, b2, l, alpha)
    return xc.reshape(batch, seq, D)
```

```python
import functools

import jax
import jax.numpy as jnp
from jax import lax
from jax.experimental import pallas as pl
from jax.experimental.pallas import tpu as pltpu

F32 = jnp.float32
BF16 = jnp.bfloat16
I32 = jnp.int32

LANES = 128
SUBLANES = 8
HEAD_DIM = 64
FOX_HEADS = 8
HGRN_HEADS = 8
HEADS_PER_VREG = LANES // HEAD_DIM
FOX_PAIRS = FOX_HEADS // HEADS_PER_VREG
HGRN_PAIRS = HGRN_HEADS // HEADS_PER_VREG
FOX_WIDTH = FOX_HEADS * HEAD_DIM
HGRN_WIDTH = HGRN_HEADS * HEAD_DIM
N_EXPERTS = 16
N_GROUPS = 4
EXPERTS_PER_GROUP = N_EXPERTS // N_GROUPS
TOP_K = 2
HGRN_CHUNK = 16
LN_EPS = 1e-5
RMS_EPS = 1e-6
MASK_VALUE = -1e30
LOG2E = 1.4426950408889634
VMEM_LIMIT = 56 * 1024 * 1024

_C_FQ = 0
_C_FK = _C_FQ + FOX_WIDTH
_C_FV = _C_FK + FOX_WIDTH
_C_HQ = _C_FV + FOX_WIDTH
_C_HI = _C_HQ + HGRN_WIDTH
_C_HG = _C_HI + HGRN_WIDTH
_C_GATES = _C_HG + HGRN_WIDTH


def _params(sem, vmem=VMEM_LIMIT):
    return pltpu.CompilerParams(dimension_semantics=sem, vmem_limit_bytes=vmem)


def _split3(x):
    hi = x.astype(BF16)
    r1 = x - hi.astype(F32)
    mid = r1.astype(BF16)
    lo = (r1 - mid.astype(F32)).astype(BF16)
    return hi, mid, lo


def _dot(a, b):
    return jnp.dot(a, b, preferred_element_type=F32)


def _dot_nt(a, b):
    return lax.dot_general(a, b, (((1,), (1,)), ((), ())), preferred_element_type=F32)


def _dot_tn(a, b):
    return lax.dot_general(a, b, (((0,), (0,)), ((), ())), preferred_element_type=F32)


def _dot_exact_lhs(m, x):
    hi, mid, lo = _split3(x)
    return _dot(m, hi) + _dot(m, mid) + _dot(m, lo)


def _dot_exact_rhs(x, m):
    hi, mid, lo = _split3(x)
    return _dot(hi, m) + _dot(mid, m) + _dot(lo, m)


def _sigmoid(x):
    return 1.0 / (1.0 + jnp.exp(-x))


def _layer_norm(x, g, b):
    mu = jnp.mean(x, axis=-1, keepdims=True)
    xc = x - mu
    var = jnp.mean(xc * xc, axis=-1, keepdims=True)
    return xc * lax.rsqrt(var + LN_EPS) * g + b


def _row_tile(n, want):
    t = min(n, want)
    assert n % t == 0, (n, t)
    return t


def _ln_kernel(x_ref, g_ref, b_ref, o_ref):
    o_ref[...] = _layer_norm(x_ref[...], g_ref[...], b_ref[...])


def _ln_in(x, g, b):
    T, D = x.shape
    tm = _row_tile(T, 512)
    return pl.pallas_call(
        _ln_kernel,
        grid=(T // tm,),
        in_specs=[pl.BlockSpec((tm, D), lambda i: (i, 0)),
                  pl.BlockSpec((1, D), lambda i: (0, 0)),
                  pl.BlockSpec((1, D), lambda i: (0, 0))],
        out_specs=pl.BlockSpec((tm, D), lambda i: (i, 0)),
        out_shape=jax.ShapeDtypeStruct((T, D), F32),
        compiler_params=_params(("parallel",)),
        name="ln_in",
    )(x, g.reshape(1, D), b.reshape(1, D))


def _fox_head_lanes(h):
    lane = lax.broadcasted_iota(I32, (1, LANES), 1)
    hh = h % HEADS_PER_VREG
    own = (lane >= hh * HEAD_DIM) & (lane < (hh + 1) * HEAD_DIM)
    e0 = (1 - hh) * HEAD_DIM
    return lane, own, e0


def _inproj_kernel(x_ref, w_ref, b_ref, tri_ref, qp_ref, kp_ref, vt_ref, hq_ref, hi_ref, hg_ref,
                   gates_ref, hf_ref, fc_ref, *, d_model, tiles_per_seq, cb):
    i = pl.program_id(0)
    tm = x_ref.shape[0]
    xb = x_ref[...].astype(BF16)

    def proj(c0, n):
        return _dot(xb, w_ref[:, c0:c0 + n]) + b_ref[:, c0:c0 + n]

    c_hf = _C_GATES + 2 * d_model

    @pl.when(i % tiles_per_seq == 0)
    def _():
        fc_ref[...] = jnp.zeros(fc_ref.shape, F32)

    ff = proj(c_hf + HGRN_WIDTH, LANES)
    log_f = jnp.minimum(ff, 0.0) - jnp.log(1.0 + jnp.exp(-jnp.abs(ff)))
    f_cum = _dot_exact_lhs(tri_ref[...], log_f * LOG2E) + fc_ref[...]
    fc_ref[...] = f_cum[tm - 1:tm, :]

    q_all = proj(_C_FQ, FOX_WIDTH) * (HEAD_DIM ** -0.5 * LOG2E)
    k_all = proj(_C_FK, FOX_WIDTH)
    v_all = proj(_C_FV, FOX_WIDTH)
    for h in range(FOX_HEADS):
        p = h // HEADS_PER_VREG
        cols = slice(p * LANES, (p + 1) * LANES)
        lane, own, e0 = _fox_head_lanes(h)
        hi, mid, lo = (t.astype(F32) for t in _split3(f_cum[:, h:h + 1]))
        ext_q = jnp.where((lane >= e0 + 3) & (lane < e0 + 6), 1.0, 0.0)
        ext_q = jnp.where(lane == e0, hi, ext_q)
        ext_q = jnp.where(lane == e0 + 1, mid, ext_q)
        ext_q = jnp.where(lane == e0 + 2, lo, ext_q)
        ext_k = jnp.where((lane >= e0) & (lane < e0 + 3), 1.0, 0.0)
        ext_k = jnp.where(lane == e0 + 3, -hi, ext_k)
        ext_k = jnp.where(lane == e0 + 4, -mid, ext_k)
        ext_k = jnp.where(lane == e0 + 5, -lo, ext_k)
        qp_ref[h] = jnp.where(own, q_all[:, cols], ext_q).astype(BF16)
        kp = jnp.where(own, k_all[:, cols], ext_k).astype(BF16)
        vt = jnp.where(own, v_all[:, cols], 0.0).T
        for blk in range(tm // cb):
            kp_ref[h, blk] = kp[blk * cb:(blk + 1) * cb]
            vt_ref[h, blk] = vt[:, blk * cb:(blk + 1) * cb].astype(BF16)

    hq_ref[...] = proj(_C_HQ, HGRN_WIDTH).astype(BF16)
    hi_ref[...] = proj(_C_HI, HGRN_WIDTH).astype(BF16)
    hg_ref[...] = proj(_C_HG, HGRN_WIDTH).astype(BF16)
    n_gate_chunks = 2 * d_model // 512
    for c in range(n_gate_chunks):
        gates_ref[:, c * 512:(c + 1) * 512] = proj(_C_GATES + c * 512, 512).astype(BF16)
    hf_ref[...] = proj(c_hf, HGRN_WIDTH)


def _inproj(x, w, b, tri, layer, seq, cb):
    T, D = x.shape
    n_all = w.shape[-1]
    tm = tri.shape[0]
    H = FOX_HEADS
    row = lambda n: pl.BlockSpec((tm, n), lambda i: (i, 0))
    outs = [(HGRN_WIDTH, BF16)] * 3 + [(2 * D, BF16), (HGRN_WIDTH, F32)]
    return pl.pallas_call(
        functools.partial(_inproj_kernel, d_model=D, tiles_per_seq=seq // tm, cb=cb),
        grid=(T // tm,),
        in_specs=[row(D),
                  pl.BlockSpec((None, D, n_all), lambda i: (layer, 0, 0), pipeline_mode=pl.Buffered(1)),
                  pl.BlockSpec((None, 1, n_all), lambda i: (layer, 0, 0), pipeline_mode=pl.Buffered(1)),
                  pl.BlockSpec((tm, tm), lambda i: (0, 0), pipeline_mode=pl.Buffered(1))],
        out_specs=[pl.BlockSpec((H, tm, LANES), lambda i: (0, i, 0)),
                   pl.BlockSpec((H, tm // cb, cb, LANES), lambda i: (0, i, 0, 0)),
                   pl.BlockSpec((H, tm // cb, LANES, cb), lambda i: (0, i, 0, 0))]
                  + [row(n) for n, _ in outs],
        out_shape=[jax.ShapeDtypeStruct((H, T, LANES), BF16),
                   jax.ShapeDtypeStruct((H, T // cb, cb, LANES), BF16),
                   jax.ShapeDtypeStruct((H, T // cb, LANES, cb), BF16)]
                  + [jax.ShapeDtypeStruct((T, n), dt) for n, dt in outs],
        scratch_shapes=[pltpu.VMEM((1, LANES), F32)],
        compiler_params=_params(("arbitrary",)),
        name="inproj",
    )(x, w, b, tri)


def _fox_kernel(qp_ref, kp_ref, vt_ref, o_ref, m_ref, l_ref, acc_ref, s_ref, p_ref, a_ref, *, tq, cb):
    i = pl.program_id(1)
    key_i = lax.broadcasted_iota(I32, (tq, tq), 0)
    qry_i = lax.broadcasted_iota(I32, (tq, tq), 1)
    causal = key_i <= qry_i
    for h in range(FOX_HEADS):
        m_ref[h] = jnp.full((1, tq), MASK_VALUE, F32)
        l_ref[h] = jnp.zeros((1, tq), F32)
        acc_ref[h] = jnp.zeros((LANES, tq), F32)

    def scores(j, par):
        for h in range(FOX_HEADS):
            s_ref[par * FOX_HEADS + h] = _dot_nt(kp_ref[h, j], qp_ref[h])

    def softmax(par, masked):
        for h in range(FOX_HEADS):
            st = s_ref[par * FOX_HEADS + h]
            if masked:
                st = jnp.where(causal, st, MASK_VALUE)
            m = m_ref[h]
            m_new = jnp.maximum(m, jnp.max(st, axis=0, keepdims=True))
            a = jnp.exp2(m - m_new)
            pt = jnp.exp2(st - m_new)
            m_ref[h] = m_new
            l_ref[h] = a * l_ref[h] + jnp.sum(pt, axis=0, keepdims=True)
            p_ref[par * FOX_HEADS + h] = pt.astype(BF16)
            a_ref[par * FOX_HEADS + h] = a

    def values(j, par):
        for h in range(FOX_HEADS):
            acc_ref[h] = a_ref[par * FOX_HEADS + h] * acc_ref[h] + _dot(vt_ref[h, j],
                                                                     p_ref[par * FOX_HEADS + h])

    for h in range(FOX_HEADS):
        p_ref[FOX_HEADS + h] = jnp.zeros((cb, tq), BF16)
        a_ref[FOX_HEADS + h] = jnp.ones((1, tq), F32)
    scores(0, 0)

    def trip(t, par):
        scores(t + 1, 1 - par)
        softmax(par, False)
        values(jnp.maximum(t - 1, 0), 1 - par)

    def two_trips(u, _):
        trip(2 * u, 0)
        trip(2 * u + 1, 1)
        return 0

    lax.fori_loop(0, i // 2, two_trips, 0)

    @pl.when(i % 2 == 1)
    def _():
        trip(i - 1, 0)
        softmax(1, True)
        values(i - 1, 0)
        values(i, 1)

    @pl.when(i % 2 == 0)
    def _():
        softmax(0, True)
        values(jnp.maximum(i - 1, 0), 1)
        values(i, 0)
    for p in range(FOX_PAIRS):
        h0 = p * HEADS_PER_VREG
        out_t = acc_ref[h0] / l_ref[h0]
        for h in range(h0 + 1, h0 + HEADS_PER_VREG):
            out_t = out_t + acc_ref[h] / l_ref[h]
        o_ref[:, p * LANES:(p + 1) * LANES] = out_t.T.astype(BF16)


def _fox_attention(qp, kp, vt, batch, seq, cb):
    H, T, _ = qp.shape
    tq = cb
    nq = seq // tq
    return pl.pallas_call(
        functools.partial(_fox_kernel, tq=tq, cb=cb),
        grid=(batch, nq),
        in_specs=[pl.BlockSpec((H, tq, LANES), lambda b, i: (0, b * nq + i, 0)),
                  pl.BlockSpec((H, nq, cb, LANES), lambda b, i: (0, b, 0, 0)),
                  pl.BlockSpec((H, nq, LANES, cb), lambda b, i: (0, b, 0, 0))],
        out_specs=pl.BlockSpec((tq, FOX_WIDTH), lambda b, i: (b * nq + i, 0)),
        out_shape=jax.ShapeDtypeStruct((T, FOX_WIDTH), BF16),
        scratch_shapes=[pltpu.VMEM((FOX_HEADS, 1, tq), F32),
                        pltpu.VMEM((FOX_HEADS, 1, tq), F32),
                        pltpu.VMEM((FOX_HEADS, LANES, tq), F32),
                        pltpu.VMEM((2 * FOX_HEADS, cb, tq), F32),
                        pltpu.VMEM((2 * FOX_HEADS, cb, tq), BF16),
                        pltpu.VMEM((2 * FOX_HEADS, 1, tq), F32)],
        compiler_params=_params(("parallel", "arbitrary")),
        name="fox_attention",
    )(qp, kp, vt)


def _hgrn_kernel(hq_ref, hf_ref, hi_ref, hg_ref, lb_ref, ng_ref, o_ref, ss_ref, cpad_ref, vpad_ref,
                 term_ref, score_ref, *, seq, rb, wl):
    ch = HGRN_CHUNK
    n_states = wl // LANES
    rr = lax.broadcasted_iota(I32, (rb, rb), 0)
    cc = lax.broadcasted_iota(I32, (rb, rb), 1)
    same_chunk = (rr // ch) == (cc // ch)
    cum_mat = jnp.concatenate([(same_chunk & (cc <= rr)).astype(BF16), same_chunk.astype(BF16)], axis=0)
    hr = lax.broadcasted_iota(I32, (wl, wl), 0) // HEAD_DIM
    hc = lax.broadcasted_iota(I32, (wl, wl), 1) // HEAD_DIM
    head_ones = (hr == hc).astype(BF16)
    head_mask = (hr == hc)[:LANES, :LANES].astype(F32)
    tmod = lax.broadcasted_iota(I32, (rb, wl), 0) % ch
    lb = lb_ref[...]
    ng = ng_ref[...]
    ss_ref[...] = jnp.zeros(ss_ref.shape, F32)
    cpad_ref[0:ch, :] = jnp.zeros((ch, wl), F32)
    vpad_ref[0:ch, :] = jnp.zeros((ch, wl), F32)

    def block(r, _):
        rows = pl.ds(pl.multiple_of(r * rb, rb), rb)
        z = hf_ref[rows, :]
        g = jnp.log(lb + (1.0 - lb) * _sigmoid(z))
        k = (1.0 - lb) * _sigmoid(-z)
        hq = hq_ref[rows, :].astype(F32)
        qs = hq * _sigmoid(hq)
        v = hi_ref[rows, :].astype(F32)
        cums = _dot_exact_lhs(cum_mat, g * LOG2E)
        b = cums[:rb]
        btot = cums[rb:]
        qd = (qs * jnp.exp2(b)).astype(BF16)
        kd = (k * jnp.exp2(btot - b)).astype(BF16)
        dec = jnp.exp2(btot)

        c = b - jnp.log2(k)
        cpad_ref[ch:ch + rb, :] = c
        vpad_ref[ch:ch + rb, :] = v
        term_ref[0:rb, :] = (qs * k).astype(BF16)
        for o in range(1, ch):
            cs = cpad_ref[ch - o:ch - o + rb, :]
            term = jnp.where(tmod >= o, qs * jnp.exp2(b - cs), 0.0)
            term_ref[o * rb:(o + 1) * rb, :] = term.astype(BF16)
        score_ref[...] = _dot(term_ref[...], head_ones)
        acc = score_ref[0:rb, :] * v
        for o in range(1, ch):
            acc = acc + score_ref[o * rb:(o + 1) * rb, :] * vpad_ref[ch - o:ch - o + rb, :]

        vb = v.astype(BF16)
        chunks = [slice(n * ch, (n + 1) * ch) for n in range(rb // ch)]
        inter_cols = []
        for sidx in range(n_states):
            ln = slice(sidx * LANES, (sidx + 1) * LANES)
            upds = [_dot_tn(vb[sl, ln], kd[sl, ln]) * head_mask for sl in chunks]
            ss = ss_ref[sidx]
            states = []
            for n, sl in enumerate(chunks):
                states.append(ss.astype(BF16))
                ss = ss * dec[n * ch:n * ch + 1, ln] + upds[n]
            ss_ref[sidx] = ss
            inter_cols.append(jnp.concatenate(
                [_dot_nt(qd[sl, ln], st) for sl, st in zip(chunks, states)], axis=0))
        o_blk = acc + jnp.concatenate(inter_cols, axis=1)
        ms = _dot_exact_rhs(o_blk * o_blk, head_ones) * (1.0 / HEAD_DIM)
        hg = hg_ref[rows, :].astype(F32)
        o_blk = o_blk * lax.rsqrt(ms + RMS_EPS) * ng * (hg * _sigmoid(hg))
        o_ref[rows, :] = o_blk.astype(BF16)
        return 0

    lax.fori_loop(0, seq // rb, block, 0, unroll=2 if (seq // rb) % 2 == 0 else 1)


def _hgrn(hq, hf, hi, hg, lb, ng, layer, batch, seq):
    T = hq.shape[0]
    rb = _row_tile(seq, 128)
    wl = 2 * LANES
    n_prog = HGRN_WIDTH // wl
    blk = lambda: pl.BlockSpec((seq, wl), lambda b, p: (b, p))
    par = lambda: pl.BlockSpec((None, 1, wl), lambda b, p: (layer * n_prog + p, 0, 0))
    return pl.pallas_call(
        functools.partial(_hgrn_kernel, seq=seq, rb=rb, wl=wl),
        grid=(batch, n_prog),
        in_specs=[blk(), blk(), blk(), blk(), par(), par()],
        out_specs=blk(),
        out_shape=jax.ShapeDtypeStruct((T, HGRN_WIDTH), BF16),
        scratch_shapes=[pltpu.VMEM((wl // LANES, LANES, LANES), F32),
                        pltpu.VMEM((HGRN_CHUNK + rb, wl), F32),
                        pltpu.VMEM((HGRN_CHUNK + rb, wl), F32),
                        pltpu.VMEM((HGRN_CHUNK * rb, wl), BF16),
                        pltpu.VMEM((HGRN_CHUNK * rb, wl), F32)],
        compiler_params=_params(("parallel", "parallel")),
        name="hgrn2",
    )(hq, hf, hi, hg, lb.reshape(-1, 1, wl), ng.reshape(-1, 1, wl))


def _merge_kernel(fo_ref, ho_ref, gates_ref, x_ref, wf_ref, wh_ref, wm_ref, bm_ref, g_ref, b_ref,
                  wrh_ref, wrl_ref, rb_ref, tri_ref, x1_ref, x1r_ref, route_ref, cnt_ref,
                  *, alpha, d_model):
    i = pl.program_id(0)
    tm = x_ref.shape[0]

    @pl.when(i == 0)
    def _():
        cnt_ref[...] = jnp.zeros(cnt_ref.shape, F32)

    y_fox = _dot(fo_ref[...], wf_ref[...])
    y_hgrn = _dot(ho_ref[...], wh_ref[...])
    g_fox = _sigmoid(gates_ref[:, :d_model].astype(F32))
    g_hgrn = _sigmoid(gates_ref[:, d_model:].astype(F32))
    mixed = _dot((g_fox * y_fox + g_hgrn * y_hgrn).astype(BF16), wm_ref[...]) + bm_ref[...]
    x1 = _layer_norm(alpha * x_ref[...] + mixed, g_ref[...], b_ref[...])
    x1_ref[...] = x1
    nj = d_model // LANES
    for j in range(nj):
        x1r_ref[pl.ds(j, tm, stride=nj), :] = x1[:, j * LANES:(j + 1) * LANES]

    xh = x1.astype(BF16)
    xl = (x1 - xh.astype(F32)).astype(BF16)
    logits = _dot(xh, wrh_ref[...]) + _dot(xl, wrh_ref[...]) + _dot(xh, wrl_ref[...]) + rb_ref[...]
    lane = lax.broadcasted_iota(I32, (tm, LANES), 1)
    lane_f = lane.astype(F32)
    neg = jnp.float32(-jnp.inf)
    lg = jnp.where(lane < N_EXPERTS, logits, neg)
    m1 = jnp.max(lg, axis=-1, keepdims=True)
    idx1 = jnp.min(jnp.where(lg == m1, lane_f, float(LANES)), axis=-1, keepdims=True).astype(I32)
    in_group = (lane // EXPERTS_PER_GROUP == idx1 // EXPERTS_PER_GROUP) & (lane < N_EXPERTS)
    lg2 = jnp.where(in_group & (lane != idx1), logits, neg)
    m2 = jnp.max(lg2, axis=-1, keepdims=True)
    idx2 = jnp.min(jnp.where(lg2 == m2, lane_f, float(LANES)), axis=-1, keepdims=True).astype(I32)
    e21 = jnp.exp(m2 - m1)
    gate1 = 1.0 / (1.0 + e21)
    gate2 = e21 / (1.0 + e21)

    oh1 = lane == idx1
    oh2 = lane == idx2
    oh = (oh1 | oh2).astype(F32)
    before = _dot(tri_ref[...], oh.astype(BF16)) + cnt_ref[0:1, :]
    rank1 = jnp.sum(jnp.where(oh1, before, 0.0), axis=-1, keepdims=True)
    rank2 = jnp.sum(jnp.where(oh2, before, 0.0), axis=-1, keepdims=True)
    cnt_ref[0:1, :] = cnt_ref[0:1, :] + jnp.sum(oh, axis=0, keepdims=True)

    route = jnp.where(lane == 0, idx1.astype(F32), 0.0)
    route = jnp.where(lane == 1, idx2.astype(F32), route)
    route = jnp.where(lane == 2, gate1, route)
    route = jnp.where(lane == 3, gate2, route)
    route = jnp.where(lane == 4, rank1, route)
    route = jnp.where(lane == 5, rank2, route)
    route_ref[...] = route


def _merge(fox_o, hgrn_o, gates, x, wf, wh, wm, bm, g, b, wrh, wrl, rbias, tri, layer, alpha):
    T, D = x.shape
    tm = tri.shape[0]
    nj = D // LANES
    row = lambda n: pl.BlockSpec((tm, n), lambda i: (i, 0))
    lw = lambda r, c: pl.BlockSpec((None, r, c), lambda i: (layer, 0, 0))
    cw = lambda r, c: pl.BlockSpec((r, c), lambda i: (0, 0))
    return pl.pallas_call(
        functools.partial(_merge_kernel, alpha=alpha, d_model=D),
        grid=(T // tm,),
        in_specs=[row(FOX_WIDTH), row(HGRN_WIDTH), row(2 * D), row(D),
                  lw(FOX_WIDTH, D), lw(HGRN_WIDTH, D), lw(D, D), lw(1, D), lw(1, D), lw(1, D),
                  cw(D, LANES), cw(D, LANES), cw(1, LANES), cw(tm, tm)],
        out_specs=[row(D), pl.BlockSpec((tm * nj, LANES), lambda i: (i, 0)), row(LANES),
                   pl.BlockSpec((SUBLANES, LANES), lambda i: (0, 0))],
        out_shape=[jax.ShapeDtypeStruct((T, D), F32), jax.ShapeDtypeStruct((T * nj, LANES), F32),
                   jax.ShapeDtypeStruct((T, LANES), F32), jax.ShapeDtypeStruct((SUBLANES, LANES), F32)],
        compiler_params=_params(("arbitrary",)),
        name="merge_router",
    )(fox_o, hgrn_o, gates, x, wf, wh, wm, bm, g, b, wrh, wrl, rbias, tri)


def _dispatch_kernel(dest_ref, x_ref, xs_hbm, inv_ref, sem, *, tc, nj):
    i = pl.program_id(0)

    def row_copy(r, d):
        return pltpu.make_async_copy(
            x_ref.at[pl.ds(pl.multiple_of(r * nj, nj), nj), :],
            xs_hbm.at[pl.ds(pl.multiple_of(d * nj, nj), nj), :], sem)

    def issue(r, _):
        d0 = dest_ref[0, 0, 2 * r]
        d1 = dest_ref[0, 0, 2 * r + 1]
        assignment = 2 * (i * tc + r)
        inv_ref[d0] = assignment
        inv_ref[d1] = assignment + 1
        row_copy(r, d0).start()
        row_copy(r, d1).start()
        return 0

    lax.fori_loop(0, tc, issue, 0, unroll=8)

    def drain(r, _):
        row_copy(0, 0).wait()
        row_copy(0, 0).wait()
        return 0

    lax.fori_loop(0, tc, drain, 0, unroll=8)


def _dispatch(x1r, dest, n_tokens, nj):
    tc = _row_tile(n_tokens, 512)
    n_steps = n_tokens // tc
    return pl.pallas_call(
        functools.partial(_dispatch_kernel, tc=tc, nj=nj),
        grid=(n_steps,),
        in_specs=[pl.BlockSpec((1, 1, 2 * tc), lambda i: (i, 0, 0), memory_space=pltpu.SMEM),
                  pl.BlockSpec((tc * nj, LANES), lambda i: (i, 0))],
        out_specs=[pl.BlockSpec(memory_space=pl.ANY), pl.BlockSpec(memory_space=pltpu.SMEM)],
        out_shape=[jax.ShapeDtypeStruct((2 * n_tokens * nj, LANES), F32),
                   jax.ShapeDtypeStruct((2 * n_tokens,), I32)],
        scratch_shapes=[pltpu.SemaphoreType.DMA(())],
        compiler_params=pltpu.CompilerParams(dimension_semantics=("arbitrary",), has_side_effects=True),
        name="moe_dispatch",
    )(dest.reshape(n_steps, 1, 2 * tc), x1r)


def _experts_kernel(meta_ref, slot_ref, xs_ref, w1_ref, w3_ref, w2_ref, y2_hbm, ybuf_ref, sem,
                    *, tm, nj, n_assign):
    w = pl.program_id(0)
    lo = meta_ref[2, w]
    hi = meta_ref[3, w]
    prev = jnp.maximum(w - 1, 0)
    prev_real = (w >= 1) & (meta_ref[3, prev] > meta_ref[2, prev])
    cur_par = w % 2
    prev_par = 1 - cur_par

    def row_copy(par, r, slot):
        return pltpu.make_async_copy(
            ybuf_ref.at[par, pl.ds(pl.multiple_of(r * nj, nj), nj), :],
            y2_hbm.at[pl.ds(pl.multiple_of(slot * nj, nj), nj), :], sem.at[par])

    def send_previous_rows():
        for r in range(tm):
            row_copy(prev_par, r, slot_ref[0, 0, r]).start()

    def wait_rows(par):
        pltpu.make_async_copy(ybuf_ref.at[par], y2_hbm.at[pl.ds(0, tm * nj), :], sem.at[par]).wait()

    @pl.when(w == 0)
    def _():
        ybuf_ref[...] = jnp.zeros(ybuf_ref.shape, F32)
        spare0 = pltpu.make_async_copy(ybuf_ref.at[0], y2_hbm.at[pl.ds(n_assign * nj, tm * nj), :], sem.at[0])
        spare0.start()
        spare0.wait()

    @pl.when(hi > lo)
    def _():
        send_previous_rows()
        x = jnp.concatenate([xs_ref[pl.ds(j, tm, stride=nj), :] for j in range(nj)], axis=-1).astype(BF16)
        h1 = _dot(x, w1_ref[...].astype(BF16))
        h3 = _dot(x, w3_ref[...].astype(BF16))
        h = (h1 * _sigmoid(h1) * h3).astype(BF16)
        y = _dot(h, w2_ref[...].astype(BF16))

        @pl.when(w >= 1)
        def _():
            wait_rows(cur_par)

        for j in range(nj):
            ybuf_ref[cur_par, pl.ds(j, tm, stride=nj), :] = y[:, j * LANES:(j + 1) * LANES]

    @pl.when((hi == lo) & prev_real)
    def _():
        send_previous_rows()
        wait_rows(0)
        wait_rows(1)


def _experts(meta, inv, xs, w1, w3, w2, layer, tm, nj):
    n_items = meta.shape[1]
    D = nj * LANES
    dh = w1.shape[-1]
    n_assign = xs.shape[0] // nj
    tile, lo, hi = meta[0], meta[2], meta[3]
    r = jnp.arange(tm, dtype=I32)[None, :]
    owned = (r >= lo[:, None]) & (r < hi[:, None])
    spare = n_assign + (jnp.arange(n_items, dtype=I32)[:, None] % 2) * tm + r
    slots = jnp.where(owned, inv[tile[:, None] * tm + r], spare)
    slots = jnp.concatenate([n_assign + tm + r, slots[:-1]], axis=0).reshape(n_items, 1, tm)
    grid_spec = pltpu.PrefetchScalarGridSpec(
        num_scalar_prefetch=1,
        grid=(n_items,),
        in_specs=[pl.BlockSpec((1, 1, tm), lambda w, m: (w, 0, 0), memory_space=pltpu.SMEM),
                  pl.BlockSpec((tm * nj, LANES), lambda w, m: (m[0, w], 0)),
                  pl.BlockSpec((None, None, D, dh), lambda w, m: (layer, m[1, w], 0, 0)),
                  pl.BlockSpec((None, None, D, dh), lambda w, m: (layer, m[1, w], 0, 0)),
                  pl.BlockSpec((None, None, dh, D), lambda w, m: (layer, m[1, w], 0, 0))],
        out_specs=pl.BlockSpec(memory_space=pl.ANY),
        scratch_shapes=[pltpu.VMEM((2, tm * nj, LANES), F32), pltpu.SemaphoreType.DMA((2,))],
    )
    return pl.pallas_call(
        functools.partial(_experts_kernel, tm=tm, nj=nj, n_assign=n_assign),
        grid_spec=grid_spec,
        out_shape=jax.ShapeDtypeStruct(((n_assign + 2 * tm) * nj, LANES), F32),
        compiler_params=pltpu.CompilerParams(dimension_semantics=("arbitrary",), vmem_limit_bytes=VMEM_LIMIT,
                                             has_side_effects=True),
        name="moe_experts",
    )(meta, slots, xs, w1, w3, w2)


def _combine_kernel(y2_ref, route_ref, x1_ref, g_ref, b_ref, o_ref, *, tc, nj, alpha):
    gate1 = route_ref[:, 2:3]
    gate2 = route_ref[:, 3:4]
    y = jnp.concatenate(
        [gate1 * y2_ref[pl.ds(j, tc, stride=TOP_K * nj), :] + gate2 * y2_ref[pl.ds(nj + j, tc, stride=TOP_K * nj), :]
         for j in range(nj)], axis=-1)
    o_ref[...] = _layer_norm(alpha * x1_ref[...] + y, g_ref[...], b_ref[...])


def _combine(y2, route, x1, g, b, layer, alpha):
    T, D = x1.shape
    nj = D // LANES
    tc = _row_tile(T, 512)
    row = lambda n: pl.BlockSpec((tc, n), lambda i: (i, 0))
    lw = lambda r, c: pl.BlockSpec((None, r, c), lambda i: (layer, 0, 0))
    return pl.pallas_call(
        functools.partial(_combine_kernel, tc=tc, nj=nj, alpha=alpha),
        grid=(T // tc,),
        in_specs=[pl.BlockSpec((tc * TOP_K * nj, LANES), lambda i: (i, 0)), row(LANES), row(D), lw(1, D), lw(1, D)],
        out_specs=row(D),
        out_shape=jax.ShapeDtypeStruct((T, D), F32),
        compiler_params=_params(("parallel",)),
        name="moe_combine",
    )(y2, route, x1, g, b)


def _routing_tables(route, counts, tm, n_items):
    e = route[:, 0:2].astype(I32)
    rank = route[:, 4:6].astype(I32)
    cnt = counts[0, :N_EXPERTS].astype(I32)
    ends = jnp.cumsum(cnt)
    starts = ends - cnt
    dest = (starts[e] + rank).reshape(-1)

    first_tile = starts // tm
    last_tile = jnp.maximum(ends - 1, 0) // tm
    n_tiles_e = jnp.where(cnt > 0, last_tile - first_tile + 1, 0)
    item_end = jnp.cumsum(n_tiles_e)
    item_start = item_end - n_tiles_e
    n_real = item_end[-1]
    w = jnp.arange(n_items, dtype=I32)
    wc = jnp.minimum(w, n_real - 1)
    ex = jnp.sum((item_end[None, :] <= wc[:, None]).astype(I32), axis=1)
    tile = first_tile[ex] + (wc - item_start[ex])
    lo = jnp.clip(starts[ex] - tile * tm, 0, tm)
    hi = jnp.clip(ends[ex] - tile * tm, 0, tm)
    real = w < n_real
    hi = jnp.where(real, hi, lo)
    meta = jnp.stack([tile, ex, lo, hi]).astype(I32)
    return dest, meta


def kernel(x, ln_in_g, ln_in_b, w_in, b_in, w_fox_branch, hgrn_lb_logits, hgrn_norm_g, w_hgrn_branch,
           w_mix_out, b_mix_out, ln1_g, ln1_b, router_w, router_b, expert_w1, expert_w3, expert_w2,
           ln2_g, ln2_b):
    batch, seq, D = x.shape
    depth = w_in.shape[0]
    T = batch * seq
    nj = D // LANES
    alpha = float((2 * depth) ** 0.25)
    assert D % 512 == 0 and seq % HGRN_CHUNK == 0

    sizes = (FOX_WIDTH, FOX_WIDTH, FOX_WIDTH, FOX_HEADS, HGRN_WIDTH, HGRN_WIDTH, HGRN_WIDTH, HGRN_WIDTH, 2 * D)
    offs = [0]
    for s in sizes:
        offs.append(offs[-1] + s)
    col = lambda a, i: a[..., offs[i]:offs[i + 1]]
    order = (0, 1, 2, 4, 6, 7, 8, 5)
    pad_ff = lambda a: jnp.pad(col(a, 3), [(0, 0)] * (a.ndim - 1) + [(0, LANES - FOX_HEADS)])
    w_all = jnp.concatenate([col(w_in, i) for i in order] + [pad_ff(w_in)], axis=-1).astype(BF16)
    b_all = jnp.concatenate([col(b_in, i) for i in order] + [pad_ff(b_in)], axis=-1).astype(F32)[:, None, :]

    lb_p = jax.nn.softmax(hgrn_lb_logits.astype(F32), axis=0)
    lb_all = (jnp.cumsum(lb_p, axis=0) - lb_p[0]).reshape(depth * HGRN_PAIRS, 1, LANES)
    ng_all = hgrn_norm_g.astype(F32).reshape(depth * HGRN_PAIRS, 1, LANES)

    wf = w_fox_branch.astype(BF16)
    wh = w_hgrn_branch.astype(BF16)
    wm = w_mix_out.astype(BF16)
    r3 = lambda a: a.astype(F32)[:, None, :]
    bm, g1, b1, g2, b2 = r3(b_mix_out), r3(ln1_g), r3(ln1_b), r3(ln2_g), r3(ln2_b)
    rw = jnp.pad(router_w.astype(F32), ((0, 0), (0, LANES - N_EXPERTS)))
    wrh = rw.astype(BF16)
    wrl = (rw - wrh.astype(F32)).astype(BF16)
    rbias = jnp.pad(router_b.astype(F32), (0, LANES - N_EXPERTS)).reshape(1, LANES)

    tm_merge = _row_tile(T, 512)
    tri = jnp.tril(jnp.ones((tm_merge, tm_merge), BF16), k=-1)
    tm_exp = _row_tile(2 * T, 512)
    n_items = (2 * T) // tm_exp + N_EXPERTS
    tm_in = _row_tile(seq, 512)
    fox_block = _row_tile(tm_in, 256)
    tri_in = jnp.tril(jnp.ones((tm_in, tm_in), BF16))

    xc = _ln_in(x.reshape(T, D), ln_in_g.astype(F32), ln_in_b.astype(F32))
    for l in range(depth):
        qp, kp, vt, hq, hi, hg, gates, hf = _inproj(xc, w_all, b_all, tri_in, l, seq, fox_block)
        fox_o = _fox_attention(qp, kp, vt, batch, seq, fox_block)
        hgrn_o = _hgrn(hq, hf, hi, hg, lb_all, ng_all, l, batch, seq)
        x1, x1r, route, counts = _merge(fox_o, hgrn_o, gates, xc, wf, wh, wm, bm, g1, b1,
                                        wrh, wrl, rbias, tri, l, alpha)
        dest, meta = _routing_tables(route, counts, tm_exp, n_items)
        xs, inv = _dispatch(x1r, dest, T, nj)
        y2 = _experts(meta, inv, xs, expert_w1, expert_w3, expert_w2, l, tm_exp, nj)
        xc = _combine(y2, route, x1, g2, b2, l, alpha)
    return xc.reshape(batch, seq, D)
```

```python
import functools

import jax
import jax.numpy as jnp
from jax import lax
from jax.experimental import pallas as pl
from jax.experimental.pallas import tpu as pltpu

F32 = jnp.float32
BF16 = jnp.bfloat16
I32 = jnp.int32

LANES = 128
SUBLANES = 8
HEAD_DIM = 64
FOX_HEADS = 8
HGRN_HEADS = 8
HEADS_PER_VREG = LANES // HEAD_DIM
FOX_PAIRS = FOX_HEADS // HEADS_PER_VREG
HGRN_PAIRS = HGRN_HEADS // HEADS_PER_VREG
FOX_WIDTH = FOX_HEADS * HEAD_DIM
HGRN_WIDTH = HGRN_HEADS * HEAD_DIM
N_EXPERTS = 16
N_GROUPS = 4
EXPERTS_PER_GROUP = N_EXPERTS // N_GROUPS
TOP_K = 2
HGRN_CHUNK = 16
LN_EPS = 1e-5
RMS_EPS = 1e-6
MASK_VALUE = -1e30
LOG2E = 1.4426950408889634
VMEM_LIMIT = 56 * 1024 * 1024

_C_FQ = 0
_C_FK = _C_FQ + FOX_WIDTH
_C_FV = _C_FK + FOX_WIDTH
_C_HQ = _C_FV + FOX_WIDTH
_C_HI = _C_HQ + HGRN_WIDTH
_C_HG = _C_HI + HGRN_WIDTH
_C_GATES = _C_HG + HGRN_WIDTH


def _params(sem, vmem=VMEM_LIMIT):
    return pltpu.CompilerParams(dimension_semantics=sem, vmem_limit_bytes=vmem)


def _split3(x):
    hi = x.astype(BF16)
    r1 = x - hi.astype(F32)
    mid = r1.astype(BF16)
    lo = (r1 - mid.astype(F32)).astype(BF16)
    return hi, mid, lo


def _dot(a, b):
    return jnp.dot(a, b, preferred_element_type=F32)


def _dot_nt(a, b):
    return lax.dot_general(a, b, (((1,), (1,)), ((), ())), preferred_element_type=F32)


def _dot_tn(a, b):
    return lax.dot_general(a, b, (((0,), (0,)), ((), ())), preferred_element_type=F32)


def _dot_exact_lhs(m, x):
    hi, mid, lo = _split3(x)
    return _dot(m, hi) + _dot(m, mid) + _dot(m, lo)


def _dot_exact_rhs(x, m):
    hi, mid, lo = _split3(x)
    return _dot(hi, m) + _dot(mid, m) + _dot(lo, m)


def _sigmoid(x):
    return 1.0 / (1.0 + jnp.exp(-x))


def _layer_norm(x, g, b):
    mu = jnp.mean(x, axis=-1, keepdims=True)
    xc = x - mu
    var = jnp.mean(xc * xc, axis=-1, keepdims=True)
    return xc * lax.rsqrt(var + LN_EPS) * g + b


def _row_tile(n, want):
    t = min(n, want)
    assert n % t == 0, (n, t)
    return t


def _ln_kernel(x_ref, g_ref, b_ref, o_ref):
    o_ref[...] = _layer_norm(x_ref[...], g_ref[...], b_ref[...])


def _ln_in(x, g, b):
    T, D = x.shape
    tm = _row_tile(T, 512)
    return pl.pallas_call(
        _ln_kernel,
        grid=(T // tm,),
        in_specs=[pl.BlockSpec((tm, D), lambda i: (i, 0)),
                  pl.BlockSpec((1, D), lambda i: (0, 0)),
                  pl.BlockSpec((1, D), lambda i: (0, 0))],
        out_specs=pl.BlockSpec((tm, D), lambda i: (i, 0)),
        out_shape=jax.ShapeDtypeStruct((T, D), F32),
        compiler_params=_params(("parallel",)),
        name="ln_in",
    )(x, g.reshape(1, D), b.reshape(1, D))


def _fox_head_lanes(h):
    lane = lax.broadcasted_iota(I32, (1, LANES), 1)
    hh = h % HEADS_PER_VREG
    own = (lane >= hh * HEAD_DIM) & (lane < (hh + 1) * HEAD_DIM)
    e0 = (1 - hh) * HEAD_DIM
    return lane, own, e0


def _inproj_kernel(x_ref, w_ref, b_ref, tri_ref, qp_ref, kp_ref, vt_ref, hq_ref, hi_ref, hg_ref,
                   gates_ref, hf_ref, fc_ref, *, d_model, tiles_per_seq, cb):
    i = pl.program_id(0)
    tm = x_ref.shape[0]
    xb = x_ref[...].astype(BF16)

    def proj(c0, n):
        return _dot(xb, w_ref[:, c0:c0 + n]) + b_ref[:, c0:c0 + n]

    c_hf = _C_GATES + 2 * d_model

    @pl.when(i % tiles_per_seq == 0)
    def _():
        fc_ref[...] = jnp.zeros(fc_ref.shape, F32)

    ff = proj(c_hf + HGRN_WIDTH, LANES)
    log_f = jnp.minimum(ff, 0.0) - jnp.log(1.0 + jnp.exp(-jnp.abs(ff)))
    f_cum = _dot_exact_lhs(tri_ref[...], log_f * LOG2E) + fc_ref[...]
    fc_ref[...] = f_cum[tm - 1:tm, :]

    q_all = proj(_C_FQ, FOX_WIDTH) * (HEAD_DIM ** -0.5 * LOG2E)
    k_all = proj(_C_FK, FOX_WIDTH)
    v_all = proj(_C_FV, FOX_WIDTH)
    for h in range(FOX_HEADS):
        p = h // HEADS_PER_VREG
        cols = slice(p * LANES, (p + 1) * LANES)
        lane, own, e0 = _fox_head_lanes(h)
        hi, mid, lo = (t.astype(F32) for t in _split3(f_cum[:, h:h + 1]))
        ext_q = jnp.where((lane >= e0 + 3) & (lane < e0 + 6), 1.0, 0.0)
        ext_q = jnp.where(lane == e0, hi, ext_q)
        ext_q = jnp.where(lane == e0 + 1, mid, ext_q)
        ext_q = jnp.where(lane == e0 + 2, lo, ext_q)
        ext_k = jnp.where((lane >= e0) & (lane < e0 + 3), 1.0, 0.0)
        ext_k = jnp.where(lane == e0 + 3, -hi, ext_k)
        ext_k = jnp.where(lane == e0 + 4, -mid, ext_k)
        ext_k = jnp.where(lane == e0 + 5, -lo, ext_k)
        qp_ref[h] = jnp.where(own, q_all[:, cols], ext_q).astype(BF16)
        kp = jnp.where(own, k_all[:, cols], ext_k).astype(BF16)
        vt = jnp.where(own, v_all[:, cols], 0.0).T
        for blk in range(tm // cb):
            kp_ref[h, blk] = kp[blk * cb:(blk + 1) * cb]
            vt_ref[h, blk] = vt[:, blk * cb:(blk + 1) * cb].astype(BF16)

    hq_ref[...] = proj(_C_HQ, HGRN_WIDTH).astype(BF16)
    hi_ref[...] = proj(_C_HI, HGRN_WIDTH).astype(BF16)
    hg_ref[...] = proj(_C_HG, HGRN_WIDTH).astype(BF16)
    n_gate_chunks = 2 * d_model // 512
    for c in range(n_gate_chunks):
        gates_ref[:, c * 512:(c + 1) * 512] = proj(_C_GATES + c * 512, 512).astype(BF16)
    hf_ref[...] = proj(c_hf, HGRN_WIDTH)


def _inproj(x, w, b, tri, layer, seq, cb):
    T, D = x.shape
    n_all = w.shape[-1]
    tm = tri.shape[0]
    H = FOX_HEADS
    row = lambda n: pl.BlockSpec((tm, n), lambda i: (i, 0))
    outs = [(HGRN_WIDTH, BF16)] * 3 + [(2 * D, BF16), (HGRN_WIDTH, F32)]
    return pl.pallas_call(
        functools.partial(_inproj_kernel, d_model=D, tiles_per_seq=seq // tm, cb=cb),
        grid=(T // tm,),
        in_specs=[row(D),
                  pl.BlockSpec((None, D, n_all), lambda i: (layer, 0, 0), pipeline_mode=pl.Buffered(1)),
                  pl.BlockSpec((None, 1, n_all), lambda i: (layer, 0, 0), pipeline_mode=pl.Buffered(1)),
                  pl.BlockSpec((tm, tm), lambda i: (0, 0), pipeline_mode=pl.Buffered(1))],
        out_specs=[pl.BlockSpec((H, tm, LANES), lambda i: (0, i, 0)),
                   pl.BlockSpec((H, tm // cb, cb, LANES), lambda i: (0, i, 0, 0)),
                   pl.BlockSpec((H, tm // cb, LANES, cb), lambda i: (0, i, 0, 0))]
                  + [row(n) for n, _ in outs],
        out_shape=[jax.ShapeDtypeStruct((H, T, LANES), BF16),
                   jax.ShapeDtypeStruct((H, T // cb, cb, LANES), BF16),
                   jax.ShapeDtypeStruct((H, T // cb, LANES, cb), BF16)]
                  + [jax.ShapeDtypeStruct((T, n), dt) for n, dt in outs],
        scratch_shapes=[pltpu.VMEM((1, LANES), F32)],
        compiler_params=_params(("arbitrary",)),
        name="inproj",
    )(x, w, b, tri)


def _fox_kernel(qp_ref, kp_ref, vt_ref, o_ref, m_ref, l_ref, acc_ref, s_ref, p_ref, a_ref, *, tq, cb):
    i = pl.program_id(1)
    key_i = lax.broadcasted_iota(I32, (tq, tq), 0)
    qry_i = lax.broadcasted_iota(I32, (tq, tq), 1)
    causal = key_i <= qry_i
    for h in range(FOX_HEADS):
        m_ref[h] = jnp.full((1, tq), MASK_VALUE, F32)
        l_ref[h] = jnp.zeros((1, tq), F32)
        acc_ref[h] = jnp.zeros((LANES, tq), F32)

    def scores(j, par):
        for h in range(FOX_HEADS):
            s_ref[par * FOX_HEADS + h] = _dot_nt(kp_ref[h, j], qp_ref[h])

    def softmax(par, masked):
        for h in range(FOX_HEADS):
            st = s_ref[par * FOX_HEADS + h]
            if masked:
                st = jnp.where(causal, st, MASK_VALUE)
            m = m_ref[h]
            m_new = jnp.maximum(m, jnp.max(st, axis=0, keepdims=True))
            a = jnp.exp2(m - m_new)
            pt = jnp.exp2(st - m_new)
            m_ref[h] = m_new
            l_ref[h] = a * l_ref[h] + jnp.sum(pt, axis=0, keepdims=True)
            p_ref[par * FOX_HEADS + h] = pt.astype(BF16)
            a_ref[par * FOX_HEADS + h] = a

    def values(j, par):
        for h in range(FOX_HEADS):
            acc_ref[h] = a_ref[par * FOX_HEADS + h] * acc_ref[h] + _dot(vt_ref[h, j],
                                                                     p_ref[par * FOX_HEADS + h])

    for h in range(FOX_HEADS):
        p_ref[FOX_HEADS + h] = jnp.zeros((cb, tq), BF16)
        a_ref[FOX_HEADS + h] = jnp.ones((1, tq), F32)
    scores(0, 0)

    def trip(t, par):
        scores(t + 1, 1 - par)
        softmax(par, False)
        values(jnp.maximum(t - 1, 0), 1 - par)

    def two_trips(u, _):
        trip(2 * u, 0)
        trip(2 * u + 1, 1)
        return 0

    lax.fori_loop(0, i // 2, two_trips, 0)

    @pl.when(i % 2 == 1)
    def _():
        trip(i - 1, 0)
        softmax(1, True)
        values(i - 1, 0)
        values(i, 1)

    @pl.when(i % 2 == 0)
    def _():
        softmax(0, True)
        values(jnp.maximum(i - 1, 0), 1)
        values(i, 0)
    for p in range(FOX_PAIRS):
        h0 = p * HEADS_PER_VREG
        out_t = acc_ref[h0] / l_ref[h0]
        for h in range(h0 + 1, h0 + HEADS_PER_VREG):
            out_t = out_t + acc_ref[h] / l_ref[h]
        o_ref[:, p * LANES:(p + 1) * LANES] = out_t.T.astype(BF16)


def _fox_attention(qp, kp, vt, batch, seq, cb):
    H, T, _ = qp.shape
    tq = cb
    nq = seq // tq
    return pl.pallas_call(
        functools.partial(_fox_kernel, tq=tq, cb=cb),
        grid=(batch, nq),
        in_specs=[pl.BlockSpec((H, tq, LANES), lambda b, i: (0, b * nq + i, 0)),
                  pl.BlockSpec((H, nq, cb, LANES), lambda b, i: (0, b, 0, 0)),
                  pl.BlockSpec((H, nq, LANES, cb), lambda b, i: (0, b, 0, 0))],
        out_specs=pl.BlockSpec((tq, FOX_WIDTH), lambda b, i: (b * nq + i, 0)),
        out_shape=jax.ShapeDtypeStruct((T, FOX_WIDTH), BF16),
        scratch_shapes=[pltpu.VMEM((FOX_HEADS, 1, tq), F32),
                        pltpu.VMEM((FOX_HEADS, 1, tq), F32),
                        pltpu.VMEM((FOX_HEADS, LANES, tq), F32),
                        pltpu.VMEM((2 * FOX_HEADS, cb, tq), F32),
                        pltpu.VMEM((2 * FOX_HEADS, cb, tq), BF16),
                        pltpu.VMEM((2 * FOX_HEADS, 1, tq), F32)],
        compiler_params=_params(("parallel", "arbitrary")),
        name="fox_attention",
    )(qp, kp, vt)


def _hgrn_kernel(hq_ref, hf_ref, hi_ref, hg_ref, lb_ref, ng_ref, o_ref, ss_ref, cpad_ref, vpad_ref,
                 term_ref, score_ref, *, seq, rb, wl):
    ch = HGRN_CHUNK
    n_states = wl // LANES
    rr = lax.broadcasted_iota(I32, (rb, rb), 0)
    cc = lax.broadcasted_iota(I32, (rb, rb), 1)
    same_chunk = (rr // ch) == (cc // ch)
    cum_mat = jnp.concatenate([(same_chunk & (cc <= rr)).astype(BF16), same_chunk.astype(BF16)], axis=0)
    hr = lax.broadcasted_iota(I32, (wl, wl), 0) // HEAD_DIM
    hc = lax.broadcasted_iota(I32, (wl, wl), 1) // HEAD_DIM
    head_ones = (hr == hc).astype(BF16)
    head_mask = (hr == hc)[:LANES, :LANES].astype(F32)
    tmod = lax.broadcasted_iota(I32, (rb, wl), 0) % ch
    lb = lb_ref[...]
    ng = ng_ref[...]
    ss_ref[...] = jnp.zeros(ss_ref.shape, F32)
    cpad_ref[0:ch, :] = jnp.zeros((ch, wl), F32)
    vpad_ref[0:ch, :] = jnp.zeros((ch, wl), F32)

    def block(r, _):
        rows = pl.ds(pl.multiple_of(r * rb, rb), rb)
        z = hf_ref[rows, :]
        g = jnp.log(lb + (1.0 - lb) * _sigmoid(z))
        k = (1.0 - lb) * _sigmoid(-z)
        hq = hq_ref[rows, :].astype(F32)
        qs = hq * _sigmoid(hq)
        v = hi_ref[rows, :].astype(F32)
        cums = _dot_exact_lhs(cum_mat, g * LOG2E)
        b = cums[:rb]
        btot = cums[rb:]
        qd = (qs * jnp.exp2(b)).astype(BF16)
        kd = (k * jnp.exp2(btot - b)).astype(BF16)
        dec = jnp.exp2(btot)

        c = b - jnp.log2(k)
        cpad_ref[ch:ch + rb, :] = c
        vpad_ref[ch:ch + rb, :] = v
        term_ref[0:rb, :] = (qs * k).astype(BF16)
        for o in range(1, ch):
            cs = cpad_ref[ch - o:ch - o + rb, :]
            term = jnp.where(tmod >= o, qs * jnp.exp2(b - cs), 0.0)
            term_ref[o * rb:(o + 1) * rb, :] = term.astype(BF16)
        score_ref[...] = _dot(term_ref[...], head_ones)
        acc = score_ref[0:rb, :] * v
        for o in range(1, ch):
            acc = acc + score_ref[o * rb:(o + 1) * rb, :] * vpad_ref[ch - o:ch - o + rb, :]

        vb = v.astype(BF16)
        chunks = [slice(n * ch, (n + 1) * ch) for n in range(rb // ch)]
        inter_cols = []
        for sidx in range(n_states):
            ln = slice(sidx * LANES, (sidx + 1) * LANES)
            upds = [_dot_tn(vb[sl, ln], kd[sl, ln]) * head_mask for sl in chunks]
            ss = ss_ref[sidx]
            states = []
            for n, sl in enumerate(chunks):
                states.append(ss.astype(BF16))
                ss = ss * dec[n * ch:n * ch + 1, ln] + upds[n]
            ss_ref[sidx] = ss
            inter_cols.append(jnp.concatenate(
                [_dot_nt(qd[sl, ln], st) for sl, st in zip(chunks, states)], axis=0))
        o_blk = acc + jnp.concatenate(inter_cols, axis=1)
        ms = _dot_exact_rhs(o_blk * o_blk, head_ones) * (1.0 / HEAD_DIM)
        hg = hg_ref[rows, :].astype(F32)
        o_blk = o_blk * lax.rsqrt(ms + RMS_EPS) * ng * (hg * _sigmoid(hg))
        o_ref[rows, :] = o_blk.astype(BF16)
        return 0

    lax.fori_loop(0, seq // rb, block, 0, unroll=2 if (seq // rb) % 2 == 0 else 1)


def _hgrn(hq, hf, hi, hg, lb, ng, layer, batch, seq):
    T = hq.shape[0]
    rb = _row_tile(seq, 128)
    wl = 2 * LANES
    n_prog = HGRN_WIDTH // wl
    blk = lambda: pl.BlockSpec((seq, wl), lambda b, p: (b, p))
    par = lambda: pl.BlockSpec((None, 1, wl), lambda b, p: (layer * n_prog + p, 0, 0))
    return pl.pallas_call(
        functools.partial(_hgrn_kernel, seq=seq, rb=rb, wl=wl),
        grid=(batch, n_prog),
        in_specs=[blk(), blk(), blk(), blk(), par(), par()],
        out_specs=blk(),
        out_shape=jax.ShapeDtypeStruct((T, HGRN_WIDTH), BF16),
        scratch_shapes=[pltpu.VMEM((wl // LANES, LANES, LANES), F32),
                        pltpu.VMEM((HGRN_CHUNK + rb, wl), F32),
                        pltpu.VMEM((HGRN_CHUNK + rb, wl), F32),
                        pltpu.VMEM((HGRN_CHUNK * rb, wl), BF16),
                        pltpu.VMEM((HGRN_CHUNK * rb, wl), F32)],
        compiler_params=_params(("parallel", "parallel")),
        name="hgrn2",
    )(hq, hf, hi, hg, lb.reshape(-1, 1, wl), ng.reshape(-1, 1, wl))


def _merge_kernel(fo_ref, ho_ref, gates_ref, x_ref, wf_ref, wh_ref, wm_ref, bm_ref, g_ref, b_ref,
                  wrh_ref, wrl_ref, rb_ref, tri_ref, x1r_ref, route_ref, cnt_ref,
                  *, alpha, d_model):
    i = pl.program_id(0)
    tm = x_ref.shape[0]

    @pl.when(i == 0)
    def _():
        cnt_ref[...] = jnp.zeros(cnt_ref.shape, F32)

    y_fox = _dot(fo_ref[...], wf_ref[...])
    y_hgrn = _dot(ho_ref[...], wh_ref[...])
    g_fox = _sigmoid(gates_ref[:, :d_model].astype(F32))
    g_hgrn = _sigmoid(gates_ref[:, d_model:].astype(F32))
    mixed = _dot((g_fox * y_fox + g_hgrn * y_hgrn).astype(BF16), wm_ref[...]) + bm_ref[...]
    x1 = _layer_norm(alpha * x_ref[...] + mixed, g_ref[...], b_ref[...])
    nj = d_model // LANES
    for j in range(nj):
        x1r_ref[pl.ds(j, tm, stride=nj), :] = x1[:, j * LANES:(j + 1) * LANES]

    xh = x1.astype(BF16)
    xl = (x1 - xh.astype(F32)).astype(BF16)
    logits = _dot(xh, wrh_ref[...]) + _dot(xl, wrh_ref[...]) + _dot(xh, wrl_ref[...]) + rb_ref[...]
    lane = lax.broadcasted_iota(I32, (tm, LANES), 1)
    lane_f = lane.astype(F32)
    neg = jnp.float32(-jnp.inf)
    lg = jnp.where(lane < N_EXPERTS, logits, neg)
    m1 = jnp.max(lg, axis=-1, keepdims=True)
    idx1 = jnp.min(jnp.where(lg == m1, lane_f, float(LANES)), axis=-1, keepdims=True).astype(I32)
    in_group = (lane // EXPERTS_PER_GROUP == idx1 // EXPERTS_PER_GROUP) & (lane < N_EXPERTS)
    lg2 = jnp.where(in_group & (lane != idx1), logits, neg)
    m2 = jnp.max(lg2, axis=-1, keepdims=True)
    idx2 = jnp.min(jnp.where(lg2 == m2, lane_f, float(LANES)), axis=-1, keepdims=True).astype(I32)
    e21 = jnp.exp(m2 - m1)
    gate1 = 1.0 / (1.0 + e21)
    gate2 = e21 / (1.0 + e21)

    oh1 = lane == idx1
    oh2 = lane == idx2
    oh = (oh1 | oh2).astype(F32)
    before = _dot(tri_ref[...], oh.astype(BF16)) + cnt_ref[0:1, :]
    rank1 = jnp.sum(jnp.where(oh1, before, 0.0), axis=-1, keepdims=True)
    rank2 = jnp.sum(jnp.where(oh2, before, 0.0), axis=-1, keepdims=True)
    cnt_ref[0:1, :] = cnt_ref[0:1, :] + jnp.sum(oh, axis=0, keepdims=True)

    route = jnp.where(lane == 0, idx1.astype(F32), 0.0)
    route = jnp.where(lane == 1, idx2.astype(F32), route)
    route = jnp.where(lane == 2, gate1, route)
    route = jnp.where(lane == 3, gate2, route)
    route = jnp.where(lane == 4, rank1, route)
    route = jnp.where(lane == 5, rank2, route)
    route_ref[...] = route


def _merge(fox_o, hgrn_o, gates, x, wf, wh, wm, bm, g, b, wrh, wrl, rbias, tri, layer, alpha):
    T, D = x.shape
    tm = tri.shape[0]
    nj = D // LANES
    row = lambda n: pl.BlockSpec((tm, n), lambda i: (i, 0))
    lw = lambda r, c: pl.BlockSpec((None, r, c), lambda i: (layer, 0, 0))
    cw = lambda r, c: pl.BlockSpec((r, c), lambda i: (0, 0))
    return pl.pallas_call(
        functools.partial(_merge_kernel, alpha=alpha, d_model=D),
        grid=(T // tm,),
        in_specs=[row(FOX_WIDTH), row(HGRN_WIDTH), row(2 * D), row(D),
                  lw(FOX_WIDTH, D), lw(HGRN_WIDTH, D), lw(D, D), lw(1, D), lw(1, D), lw(1, D),
                  cw(D, LANES), cw(D, LANES), cw(1, LANES), cw(tm, tm)],
        out_specs=[pl.BlockSpec((tm * nj, LANES), lambda i: (i, 0)), row(LANES),
                   pl.BlockSpec((SUBLANES, LANES), lambda i: (0, 0))],
        out_shape=[jax.ShapeDtypeStruct((T * nj, LANES), F32),
                   jax.ShapeDtypeStruct((T, LANES), F32), jax.ShapeDtypeStruct((SUBLANES, LANES), F32)],
        compiler_params=_params(("arbitrary",)),
        name="merge_router",
    )(fox_o, hgrn_o, gates, x, wf, wh, wm, bm, g, b, wrh, wrl, rbias, tri)


def _dispatch_kernel(dest_ref, x_ref, xs_hbm, inv_ref, sem, *, tc, nj):
    i = pl.program_id(0)

    def row_copy(r, d):
        return pltpu.make_async_copy(
            x_ref.at[pl.ds(pl.multiple_of(r * nj, nj), nj), :],
            xs_hbm.at[pl.ds(pl.multiple_of(d * nj, nj), nj), :], sem)

    def issue(r, _):
        d0 = dest_ref[0, 0, 2 * r]
        d1 = dest_ref[0, 0, 2 * r + 1]
        assignment = 2 * (i * tc + r)
        inv_ref[d0] = assignment
        inv_ref[d1] = assignment + 1
        row_copy(r, d0).start()
        row_copy(r, d1).start()
        return 0

    lax.fori_loop(0, tc, issue, 0, unroll=8)

    def drain(r, _):
        row_copy(0, 0).wait()
        row_copy(0, 0).wait()
        return 0

    lax.fori_loop(0, tc, drain, 0, unroll=8)


def _dispatch(x1r, dest, n_tokens, nj):
    tc = _row_tile(n_tokens, 512)
    n_steps = n_tokens // tc
    return pl.pallas_call(
        functools.partial(_dispatch_kernel, tc=tc, nj=nj),
        grid=(n_steps,),
        in_specs=[pl.BlockSpec((1, 1, 2 * tc), lambda i: (i, 0, 0), memory_space=pltpu.SMEM),
                  pl.BlockSpec((tc * nj, LANES), lambda i: (i, 0))],
        out_specs=[pl.BlockSpec(memory_space=pl.ANY), pl.BlockSpec(memory_space=pltpu.SMEM)],
        out_shape=[jax.ShapeDtypeStruct((2 * n_tokens * nj, LANES), F32),
                   jax.ShapeDtypeStruct((2 * n_tokens,), I32)],
        scratch_shapes=[pltpu.SemaphoreType.DMA(())],
        compiler_params=pltpu.CompilerParams(dimension_semantics=("arbitrary",), has_side_effects=True),
        name="moe_dispatch",
    )(dest.reshape(n_steps, 1, 2 * tc), x1r)


def _experts_kernel(meta_ref, slot_ref, xs_ref, w1_ref, w3_ref, w2_ref, y2_hbm, ybuf_ref, sem,
                    *, tm, nj, n_assign):
    w = pl.program_id(0)
    lo = meta_ref[2, w]
    hi = meta_ref[3, w]
    prev = jnp.maximum(w - 1, 0)
    prev_real = (w >= 1) & (meta_ref[3, prev] > meta_ref[2, prev])
    cur_par = w % 2
    prev_par = 1 - cur_par

    def row_copy(par, r, slot):
        return pltpu.make_async_copy(
            ybuf_ref.at[par, pl.ds(pl.multiple_of(r * nj, nj), nj), :],
            y2_hbm.at[pl.ds(pl.multiple_of(slot * nj, nj), nj), :], sem.at[par])

    def send_previous_rows():
        for r in range(tm):
            row_copy(prev_par, r, slot_ref[0, 0, r]).start()

    def wait_rows(par):
        pltpu.make_async_copy(ybuf_ref.at[par], y2_hbm.at[pl.ds(0, tm * nj), :], sem.at[par]).wait()

    @pl.when(w == 0)
    def _():
        ybuf_ref[...] = jnp.zeros(ybuf_ref.shape, F32)
        spare0 = pltpu.make_async_copy(ybuf_ref.at[0], y2_hbm.at[pl.ds(n_assign * nj, tm * nj), :], sem.at[0])
        spare0.start()
        spare0.wait()

    @pl.when(hi > lo)
    def _():
        send_previous_rows()
        x = jnp.concatenate([xs_ref[pl.ds(j, tm, stride=nj), :] for j in range(nj)], axis=-1).astype(BF16)
        h1 = _dot(x, w1_ref[...].astype(BF16))
        h3 = _dot(x, w3_ref[...].astype(BF16))
        h = (h1 * _sigmoid(h1) * h3).astype(BF16)
        y = _dot(h, w2_ref[...].astype(BF16))

        @pl.when(w >= 1)
        def _():
            wait_rows(cur_par)

        for j in range(nj):
            ybuf_ref[cur_par, pl.ds(j, tm, stride=nj), :] = y[:, j * LANES:(j + 1) * LANES]

    @pl.when((hi == lo) & prev_real)
    def _():
        send_previous_rows()
        wait_rows(0)
        wait_rows(1)


def _experts(meta, inv, xs, w1, w3, w2, layer, tm, nj):
    n_items = meta.shape[1]
    D = nj * LANES
    dh = w1.shape[-1]
    n_assign = xs.shape[0] // nj
    tile, lo, hi = meta[0], meta[2], meta[3]
    r = jnp.arange(tm, dtype=I32)[None, :]
    owned = (r >= lo[:, None]) & (r < hi[:, None])
    spare = n_assign + (jnp.arange(n_items, dtype=I32)[:, None] % 2) * tm + r
    slots = jnp.where(owned, inv[tile[:, None] * tm + r], spare)
    slots = jnp.concatenate([n_assign + tm + r, slots[:-1]], axis=0).reshape(n_items, 1, tm)
    grid_spec = pltpu.PrefetchScalarGridSpec(
        num_scalar_prefetch=1,
        grid=(n_items,),
        in_specs=[pl.BlockSpec((1, 1, tm), lambda w, m: (w, 0, 0), memory_space=pltpu.SMEM),
                  pl.BlockSpec((tm * nj, LANES), lambda w, m: (m[0, w], 0)),
                  pl.BlockSpec((None, None, D, dh), lambda w, m: (layer, m[1, w], 0, 0)),
                  pl.BlockSpec((None, None, D, dh), lambda w, m: (layer, m[1, w], 0, 0)),
                  pl.BlockSpec((None, None, dh, D), lambda w, m: (layer, m[1, w], 0, 0))],
        out_specs=pl.BlockSpec(memory_space=pl.ANY),
        scratch_shapes=[pltpu.VMEM((2, tm * nj, LANES), F32), pltpu.SemaphoreType.DMA((2,))],
    )
    return pl.pallas_call(
        functools.partial(_experts_kernel, tm=tm, nj=nj, n_assign=n_assign),
        grid_spec=grid_spec,
        out_shape=jax.ShapeDtypeStruct(((n_assign + 2 * tm) * nj, LANES), F32),
        compiler_params=pltpu.CompilerParams(dimension_semantics=("arbitrary",), vmem_limit_bytes=VMEM_LIMIT,
                                             has_side_effects=True),
        name="moe_experts",
    )(meta, slots, xs, w1, w3, w2)


def _combine_kernel(y2_ref, route_ref, x1r_ref, g_ref, b_ref, o_ref, *, tc, nj, alpha):
    gate1 = route_ref[:, 2:3]
    gate2 = route_ref[:, 3:4]
    resid = jnp.concatenate(
        [alpha * x1r_ref[pl.ds(j, tc, stride=nj), :]
         + gate1 * y2_ref[pl.ds(j, tc, stride=TOP_K * nj), :]
         + gate2 * y2_ref[pl.ds(nj + j, tc, stride=TOP_K * nj), :]
         for j in range(nj)], axis=-1)
    o_ref[...] = _layer_norm(resid, g_ref[...], b_ref[...])


def _combine(y2, route, x1r, g, b, layer, alpha):
    T = route.shape[0]
    nj = x1r.shape[0] // T
    D = nj * LANES
    tc = _row_tile(T, 512)
    row = lambda n: pl.BlockSpec((tc, n), lambda i: (i, 0))
    lw = lambda r, c: pl.BlockSpec((None, r, c), lambda i: (layer, 0, 0))
    return pl.pallas_call(
        functools.partial(_combine_kernel, tc=tc, nj=nj, alpha=alpha),
        grid=(T // tc,),
        in_specs=[pl.BlockSpec((tc * TOP_K * nj, LANES), lambda i: (i, 0)), row(LANES),
                  pl.BlockSpec((tc * nj, LANES), lambda i: (i, 0)), lw(1, D), lw(1, D)],
        out_specs=row(D),
        out_shape=jax.ShapeDtypeStruct((T, D), F32),
        compiler_params=_params(("parallel",)),
        name="moe_combine",
    )(y2, route, x1r, g, b)


def _routing_tables(route, counts, tm, n_items):
    e = route[:, 0:2].astype(I32)
    rank = route[:, 4:6].astype(I32)
    cnt = counts[0, :N_EXPERTS].astype(I32)
    ends = jnp.cumsum(cnt)
    starts = ends - cnt
    dest = (starts[e] + rank).reshape(-1)

    first_tile = starts // tm
    last_tile = jnp.maximum(ends - 1, 0) // tm
    n_tiles_e = jnp.where(cnt > 0, last_tile - first_tile + 1, 0)
    item_end = jnp.cumsum(n_tiles_e)
    item_start = item_end - n_tiles_e
    n_real = item_end[-1]
    w = jnp.arange(n_items, dtype=I32)
    wc = jnp.minimum(w, n_real - 1)
    ex = jnp.sum((item_end[None, :] <= wc[:, None]).astype(I32), axis=1)
    tile = first_tile[ex] + (wc - item_start[ex])
    lo = jnp.clip(starts[ex] - tile * tm, 0, tm)
    hi = jnp.clip(ends[ex] - tile * tm, 0, tm)
    real = w < n_real
    hi = jnp.where(real, hi, lo)
    meta = jnp.stack([tile, ex, lo, hi]).astype(I32)
    return dest, meta


def kernel(x, ln_in_g, ln_in_b, w_in, b_in, w_fox_branch, hgrn_lb_logits, hgrn_norm_g, w_hgrn_branch,
           w_mix_out, b_mix_out, ln1_g, ln1_b, router_w, router_b, expert_w1, expert_w3, expert_w2,
           ln2_g, ln2_b):
    batch, seq, D = x.shape
    depth = w_in.shape[0]
    T = batch * seq
    nj = D // LANES
    alpha = float((2 * depth) ** 0.25)
    assert D % 512 == 0 and seq % HGRN_CHUNK == 0

    sizes = (FOX_WIDTH, FOX_WIDTH, FOX_WIDTH, FOX_HEADS, HGRN_WIDTH, HGRN_WIDTH, HGRN_WIDTH, HGRN_WIDTH, 2 * D)
    offs = [0]
    for s in sizes:
        offs.append(offs[-1] + s)
    col = lambda a, i: a[..., offs[i]:offs[i + 1]]
    order = (0, 1, 2, 4, 6, 7, 8, 5)
    pad_ff = lambda a: jnp.pad(col(a, 3), [(0, 0)] * (a.ndim - 1) + [(0, LANES - FOX_HEADS)])
    w_all = jnp.concatenate([col(w_in, i) for i in order] + [pad_ff(w_in)], axis=-1).astype(BF16)
    b_all = jnp.concatenate([col(b_in, i) for i in order] + [pad_ff(b_in)], axis=-1).astype(F32)[:, None, :]

    lb_p = jax.nn.softmax(hgrn_lb_logits.astype(F32), axis=0)
    lb_all = (jnp.cumsum(lb_p, axis=0) - lb_p[0]).reshape(depth * HGRN_PAIRS, 1, LANES)
    ng_all = hgrn_norm_g.astype(F32).reshape(depth * HGRN_PAIRS, 1, LANES)

    wf = w_fox_branch.astype(BF16)
    wh = w_hgrn_branch.astype(BF16)
    wm = w_mix_out.astype(BF16)
    r3 = lambda a: a.astype(F32)[:, None, :]
    bm, g1, b1, g2, b2 = r3(b_mix_out), r3(ln1_g), r3(ln1_b), r3(ln2_g), r3(ln2_b)
    rw = jnp.pad(router_w.astype(F32), ((0, 0), (0, LANES - N_EXPERTS)))
    wrh = rw.astype(BF16)
    wrl = (rw - wrh.astype(F32)).astype(BF16)
    rbias = jnp.pad(router_b.astype(F32), (0, LANES - N_EXPERTS)).reshape(1, LANES)

    tm_merge = _row_tile(T, 512)
    tri = jnp.tril(jnp.ones((tm_merge, tm_merge), BF16), k=-1)
    tm_exp = _row_tile(2 * T, 512)
    n_items = (2 * T) // tm_exp + N_EXPERTS
    tm_in = _row_tile(seq, 512)
    fox_block = _row_tile(tm_in, 256)
    tri_in = jnp.tril(jnp.ones((tm_in, tm_in), BF16))

    xc = _ln_in(x.reshape(T, D), ln_in_g.astype(F32), ln_in_b.astype(F32))
    for l in range(depth):
        qp, kp, vt, hq, hi, hg, gates, hf = _inproj(xc, w_all, b_all, tri_in, l, seq, fox_block)
        fox_o = _fox_attention(qp, kp, vt, batch, seq, fox_block)
        hgrn_o = _hgrn(hq, hf, hi, hg, lb_all, ng_all, l, batch, seq)
        x1r, route, counts = _merge(fox_o, hgrn_o, gates, xc, wf, wh, wm, bm, g1, b1,
                                    wrh, wrl, rbias, tri, l, alpha)
        dest, meta = _routing_tables(route, counts, tm_exp, n_items)
        xs, inv = _dispatch(x1r, dest, T, nj)
        y2 = _experts(meta, inv, xs, expert_w1, expert_w3, expert_w2, l, tm_exp, nj)
        xc = _combine(y2, route, x1r, g2, b2, l, alpha)
    return xc.reshape(batch, seq, D)
```

```python
import functools

import jax
import jax.numpy as jnp
from jax import lax
from jax.experimental import pallas as pl
from jax.experimental.pallas import tpu as pltpu

F32 = jnp.float32
BF16 = jnp.bfloat16
I32 = jnp.int32

LANES = 128
SUBLANES = 8
HEAD_DIM = 64
FOX_HEADS = 8
HGRN_HEADS = 8
HEADS_PER_VREG = LANES // HEAD_DIM
FOX_PAIRS = FOX_HEADS // HEADS_PER_VREG
HGRN_PAIRS = HGRN_HEADS // HEADS_PER_VREG
FOX_WIDTH = FOX_HEADS * HEAD_DIM
HGRN_WIDTH = HGRN_HEADS * HEAD_DIM
N_EXPERTS = 16
N_GROUPS = 4
EXPERTS_PER_GROUP = N_EXPERTS // N_GROUPS
TOP_K = 2
ROUTE_COLS = 8
HGRN_CHUNK = 16
LN_EPS = 1e-5
RMS_EPS = 1e-6
MASK_VALUE = -1e30
LOG2E = 1.4426950408889634
VMEM_LIMIT = 56 * 1024 * 1024

_C_FQ = 0
_C_FK = _C_FQ + FOX_WIDTH
_C_FV = _C_FK + FOX_WIDTH
_C_HQ = _C_FV + FOX_WIDTH
_C_HI = _C_HQ + HGRN_WIDTH
_C_HG = _C_HI + HGRN_WIDTH
_C_GATES = _C_HG + HGRN_WIDTH


def _params(sem, vmem=VMEM_LIMIT):
    return pltpu.CompilerParams(dimension_semantics=sem, vmem_limit_bytes=vmem)


def _split3(x):
    hi = x.astype(BF16)
    r1 = x - hi.astype(F32)
    mid = r1.astype(BF16)
    lo = (r1 - mid.astype(F32)).astype(BF16)
    return hi, mid, lo


def _dot(a, b):
    return jnp.dot(a, b, preferred_element_type=F32)


def _dot_nt(a, b):
    return lax.dot_general(a, b, (((1,), (1,)), ((), ())), preferred_element_type=F32)


def _dot_tn(a, b):
    return lax.dot_general(a, b, (((0,), (0,)), ((), ())), preferred_element_type=F32)


def _dot_exact_lhs(m, x):
    hi, mid, lo = _split3(x)
    return _dot(m, hi) + _dot(m, mid) + _dot(m, lo)


def _dot_exact_rhs(x, m):
    hi, mid, lo = _split3(x)
    return _dot(hi, m) + _dot(mid, m) + _dot(lo, m)


def _sigmoid(x):
    return 1.0 / (1.0 + jnp.exp(-x))


def _layer_norm(x, g, b):
    mu = jnp.mean(x, axis=-1, keepdims=True)
    xc = x - mu
    var = jnp.mean(xc * xc, axis=-1, keepdims=True)
    return xc * lax.rsqrt(var + LN_EPS) * g + b


def _row_tile(n, want):
    t = min(n, want)
    assert n % t == 0, (n, t)
    return t


def _ln_kernel(x_ref, g_ref, b_ref, o_ref):
    o_ref[...] = _layer_norm(x_ref[...], g_ref[...], b_ref[...])


def _ln_in(x, g, b):
    T, D = x.shape
    tm = _row_tile(T, 512)
    return pl.pallas_call(
        _ln_kernel,
        grid=(T // tm,),
        in_specs=[pl.BlockSpec((tm, D), lambda i: (i, 0)),
                  pl.BlockSpec((1, D), lambda i: (0, 0)),
                  pl.BlockSpec((1, D), lambda i: (0, 0))],
        out_specs=pl.BlockSpec((tm, D), lambda i: (i, 0)),
        out_shape=jax.ShapeDtypeStruct((T, D), F32),
        compiler_params=_params(("parallel",)),
        name="ln_in",
    )(x, g.reshape(1, D), b.reshape(1, D))


def _fox_head_lanes(h):
    lane = lax.broadcasted_iota(I32, (1, LANES), 1)
    hh = h % HEADS_PER_VREG
    own = (lane >= hh * HEAD_DIM) & (lane < (hh + 1) * HEAD_DIM)
    e0 = (1 - hh) * HEAD_DIM
    return lane, own, e0


def _inproj_kernel(x_ref, w_ref, b_ref, tri_ref, qp_ref, kp_ref, vt_ref, hq_ref, hi_ref, hg_ref,
                   gates_ref, hf_ref, fc_ref, *, d_model, tiles_per_seq, cb):
    i = pl.program_id(0)
    tm = x_ref.shape[0]
    xb = x_ref[...].astype(BF16)

    def proj(c0, n):
        return _dot(xb, w_ref[:, c0:c0 + n]) + b_ref[:, c0:c0 + n]

    c_hf = _C_GATES + 2 * d_model

    @pl.when(i % tiles_per_seq == 0)
    def _():
        fc_ref[...] = jnp.zeros(fc_ref.shape, F32)

    ff = proj(c_hf + HGRN_WIDTH, LANES)
    log_f = jnp.minimum(ff, 0.0) - jnp.log(1.0 + jnp.exp(-jnp.abs(ff)))
    f_cum = _dot_exact_lhs(tri_ref[...], log_f * LOG2E) + fc_ref[...]
    fc_ref[...] = f_cum[tm - 1:tm, :]

    q_all = proj(_C_FQ, FOX_WIDTH) * (HEAD_DIM ** -0.5 * LOG2E)
    k_all = proj(_C_FK, FOX_WIDTH)
    v_all = proj(_C_FV, FOX_WIDTH)
    for h in range(FOX_HEADS):
        p = h // HEADS_PER_VREG
        cols = slice(p * LANES, (p + 1) * LANES)
        lane, own, e0 = _fox_head_lanes(h)
        hi, mid, lo = (t.astype(F32) for t in _split3(f_cum[:, h:h + 1]))
        ext_q = jnp.where((lane >= e0 + 3) & (lane < e0 + 6), 1.0, 0.0)
        ext_q = jnp.where(lane == e0, hi, ext_q)
        ext_q = jnp.where(lane == e0 + 1, mid, ext_q)
        ext_q = jnp.where(lane == e0 + 2, lo, ext_q)
        ext_k = jnp.where((lane >= e0) & (lane < e0 + 3), 1.0, 0.0)
        ext_k = jnp.where(lane == e0 + 3, -hi, ext_k)
        ext_k = jnp.where(lane == e0 + 4, -mid, ext_k)
        ext_k = jnp.where(lane == e0 + 5, -lo, ext_k)
        qp_ref[h] = jnp.where(own, q_all[:, cols], ext_q).astype(BF16)
        kp = jnp.where(own, k_all[:, cols], ext_k).astype(BF16)
        vt = jnp.where(own, v_all[:, cols], 0.0).T
        for blk in range(tm // cb):
            kp_ref[h, blk] = kp[blk * cb:(blk + 1) * cb]
            vt_ref[h, blk] = vt[:, blk * cb:(blk + 1) * cb].astype(BF16)

    hq_ref[...] = proj(_C_HQ, HGRN_WIDTH).astype(BF16)
    hi_ref[...] = proj(_C_HI, HGRN_WIDTH).astype(BF16)
    hg_ref[...] = proj(_C_HG, HGRN_WIDTH).astype(BF16)
    n_gate_chunks = 2 * d_model // 512
    for c in range(n_gate_chunks):
        gates_ref[:, c * 512:(c + 1) * 512] = proj(_C_GATES + c * 512, 512).astype(BF16)
    hf_ref[...] = proj(c_hf, HGRN_WIDTH)


def _inproj(x, w, b, tri, layer, seq, cb):
    T, D = x.shape
    n_all = w.shape[-1]
    tm = tri.shape[0]
    H = FOX_HEADS
    row = lambda n: pl.BlockSpec((tm, n), lambda i: (i, 0))
    outs = [(HGRN_WIDTH, BF16)] * 3 + [(2 * D, BF16), (HGRN_WIDTH, F32)]
    return pl.pallas_call(
        functools.partial(_inproj_kernel, d_model=D, tiles_per_seq=seq // tm, cb=cb),
        grid=(T // tm,),
        in_specs=[row(D),
                  pl.BlockSpec((None, D, n_all), lambda i: (layer, 0, 0), pipeline_mode=pl.Buffered(1)),
                  pl.BlockSpec((None, 1, n_all), lambda i: (layer, 0, 0), pipeline_mode=pl.Buffered(1)),
                  pl.BlockSpec((tm, tm), lambda i: (0, 0), pipeline_mode=pl.Buffered(1))],
        out_specs=[pl.BlockSpec((H, tm, LANES), lambda i: (0, i, 0)),
                   pl.BlockSpec((H, tm // cb, cb, LANES), lambda i: (0, i, 0, 0)),
                   pl.BlockSpec((H, tm // cb, LANES, cb), lambda i: (0, i, 0, 0))]
                  + [row(n) for n, _ in outs],
        out_shape=[jax.ShapeDtypeStruct((H, T, LANES), BF16),
                   jax.ShapeDtypeStruct((H, T // cb, cb, LANES), BF16),
                   jax.ShapeDtypeStruct((H, T // cb, LANES, cb), BF16)]
                  + [jax.ShapeDtypeStruct((T, n), dt) for n, dt in outs],
        scratch_shapes=[pltpu.VMEM((1, LANES), F32)],
        compiler_params=_params(("arbitrary",)),
        name="inproj",
    )(x, w, b, tri)


def _fox_kernel(qp_ref, kp_ref, vt_ref, o_ref, m_ref, l_ref, acc_ref, s_ref, p_ref, a_ref, *, tq, cb):
    i = pl.program_id(1)
    key_i = lax.broadcasted_iota(I32, (tq, tq), 0)
    qry_i = lax.broadcasted_iota(I32, (tq, tq), 1)
    causal = key_i <= qry_i
    for h in range(FOX_HEADS):
        m_ref[h] = jnp.full((1, tq), MASK_VALUE, F32)
        l_ref[h] = jnp.zeros((1, tq), F32)
        acc_ref[h] = jnp.zeros((LANES, tq), F32)

    def scores(j, par):
        for h in range(FOX_HEADS):
            s_ref[par * FOX_HEADS + h] = _dot_nt(kp_ref[h, j], qp_ref[h])

    def softmax(par, masked):
        for h in range(FOX_HEADS):
            st = s_ref[par * FOX_HEADS + h]
            if masked:
                st = jnp.where(causal, st, MASK_VALUE)
            m = m_ref[h]
            m_new = jnp.maximum(m, jnp.max(st, axis=0, keepdims=True))
            a = jnp.exp2(m - m_new)
            pt = jnp.exp2(st - m_new)
            m_ref[h] = m_new
            l_ref[h] = a * l_ref[h] + jnp.sum(pt, axis=0, keepdims=True)
            p_ref[par * FOX_HEADS + h] = pt.astype(BF16)
            a_ref[par * FOX_HEADS + h] = a

    def values(j, par):
        for h in range(FOX_HEADS):
            acc_ref[h] = a_ref[par * FOX_HEADS + h] * acc_ref[h] + _dot(vt_ref[h, j],
                                                                     p_ref[par * FOX_HEADS + h])

    for h in range(FOX_HEADS):
        p_ref[FOX_HEADS + h] = jnp.zeros((cb, tq), BF16)
        a_ref[FOX_HEADS + h] = jnp.ones((1, tq), F32)
    scores(0, 0)

    def trip(t, par):
        scores(t + 1, 1 - par)
        softmax(par, False)
        values(jnp.maximum(t - 1, 0), 1 - par)

    def two_trips(u, _):
        trip(2 * u, 0)
        trip(2 * u + 1, 1)
        return 0

    lax.fori_loop(0, i // 2, two_trips, 0)

    @pl.when(i % 2 == 1)
    def _():
        trip(i - 1, 0)
        softmax(1, True)
        values(i - 1, 0)
        values(i, 1)

    @pl.when(i % 2 == 0)
    def _():
        softmax(0, True)
        values(jnp.maximum(i - 1, 0), 1)
        values(i, 0)
    for p in range(FOX_PAIRS):
        h0 = p * HEADS_PER_VREG
        out_t = acc_ref[h0] / l_ref[h0]
        for h in range(h0 + 1, h0 + HEADS_PER_VREG):
            out_t = out_t + acc_ref[h] / l_ref[h]
        o_ref[:, p * LANES:(p + 1) * LANES] = out_t.T.astype(BF16)


def _fox_attention(qp, kp, vt, batch, seq, cb):
    H, T, _ = qp.shape
    tq = cb
    nq = seq // tq
    return pl.pallas_call(
        functools.partial(_fox_kernel, tq=tq, cb=cb),
        grid=(batch, nq),
        in_specs=[pl.BlockSpec((H, tq, LANES), lambda b, i: (0, b * nq + i, 0)),
                  pl.BlockSpec((H, nq, cb, LANES), lambda b, i: (0, b, 0, 0)),
                  pl.BlockSpec((H, nq, LANES, cb), lambda b, i: (0, b, 0, 0))],
        out_specs=pl.BlockSpec((tq, FOX_WIDTH), lambda b, i: (b * nq + i, 0)),
        out_shape=jax.ShapeDtypeStruct((T, FOX_WIDTH), BF16),
        scratch_shapes=[pltpu.VMEM((FOX_HEADS, 1, tq), F32),
                        pltpu.VMEM((FOX_HEADS, 1, tq), F32),
                        pltpu.VMEM((FOX_HEADS, LANES, tq), F32),
                        pltpu.VMEM((2 * FOX_HEADS, cb, tq), F32),
                        pltpu.VMEM((2 * FOX_HEADS, cb, tq), BF16),
                        pltpu.VMEM((2 * FOX_HEADS, 1, tq), F32)],
        compiler_params=_params(("parallel", "arbitrary")),
        name="fox_attention",
    )(qp, kp, vt)


def _hgrn_kernel(hq_ref, hf_ref, hi_ref, hg_ref, lb_ref, ng_ref, o_ref, ss_ref, cpad_ref, vpad_ref,
                 term_ref, score_ref, *, seq, rb, wl):
    ch = HGRN_CHUNK
    n_states = wl // LANES
    rr = lax.broadcasted_iota(I32, (rb, rb), 0)
    cc = lax.broadcasted_iota(I32, (rb, rb), 1)
    same_chunk = (rr // ch) == (cc // ch)
    cum_mat = jnp.concatenate([(same_chunk & (cc <= rr)).astype(BF16), same_chunk.astype(BF16)], axis=0)
    hr = lax.broadcasted_iota(I32, (wl, wl), 0) // HEAD_DIM
    hc = lax.broadcasted_iota(I32, (wl, wl), 1) // HEAD_DIM
    head_ones = (hr == hc).astype(BF16)
    head_mask = (hr == hc)[:LANES, :LANES].astype(F32)
    tmod = lax.broadcasted_iota(I32, (rb, wl), 0) % ch
    lb = lb_ref[...]
    ng = ng_ref[...]
    ss_ref[...] = jnp.zeros(ss_ref.shape, F32)
    cpad_ref[0:ch, :] = jnp.zeros((ch, wl), F32)
    vpad_ref[0:ch, :] = jnp.zeros((ch, wl), F32)

    def block(r, _):
        rows = pl.ds(pl.multiple_of(r * rb, rb), rb)
        z = hf_ref[rows, :]
        g = jnp.log(lb + (1.0 - lb) * _sigmoid(z))
        k = (1.0 - lb) * _sigmoid(-z)
        hq = hq_ref[rows, :].astype(F32)
        qs = hq * _sigmoid(hq)
        v = hi_ref[rows, :].astype(F32)
        cums = _dot_exact_lhs(cum_mat, g * LOG2E)
        b = cums[:rb]
        btot = cums[rb:]
        qd = (qs * jnp.exp2(b)).astype(BF16)
        kd = (k * jnp.exp2(btot - b)).astype(BF16)
        dec = jnp.exp2(btot)

        c = b - jnp.log2(k)
        cpad_ref[ch:ch + rb, :] = c
        vpad_ref[ch:ch + rb, :] = v
        term_ref[0:rb, :] = (qs * k).astype(BF16)
        for o in range(1, ch):
            cs = cpad_ref[ch - o:ch - o + rb, :]
            term = jnp.where(tmod >= o, qs * jnp.exp2(b - cs), 0.0)
            term_ref[o * rb:(o + 1) * rb, :] = term.astype(BF16)
        score_ref[...] = _dot(term_ref[...], head_ones)
        acc = score_ref[0:rb, :] * v
        for o in range(1, ch):
            acc = acc + score_ref[o * rb:(o + 1) * rb, :] * vpad_ref[ch - o:ch - o + rb, :]

        vb = v.astype(BF16)
        chunks = [slice(n * ch, (n + 1) * ch) for n in range(rb // ch)]
        inter_cols = []
        for sidx in range(n_states):
            ln = slice(sidx * LANES, (sidx + 1) * LANES)
            upds = [_dot_tn(vb[sl, ln], kd[sl, ln]) * head_mask for sl in chunks]
            ss = ss_ref[sidx]
            states = []
            for n, sl in enumerate(chunks):
                states.append(ss.astype(BF16))
                ss = ss * dec[n * ch:n * ch + 1, ln] + upds[n]
            ss_ref[sidx] = ss
            inter_cols.append(jnp.concatenate(
                [_dot_nt(qd[sl, ln], st) for sl, st in zip(chunks, states)], axis=0))
        o_blk = acc + jnp.concatenate(inter_cols, axis=1)
        ms = _dot_exact_rhs(o_blk * o_blk, head_ones) * (1.0 / HEAD_DIM)
        hg = hg_ref[rows, :].astype(F32)
        o_blk = o_blk * lax.rsqrt(ms + RMS_EPS) * ng * (hg * _sigmoid(hg))
        o_ref[rows, :] = o_blk.astype(BF16)
        return 0

    lax.fori_loop(0, seq // rb, block, 0, unroll=2 if (seq // rb) % 2 == 0 else 1)


def _hgrn(hq, hf, hi, hg, lb, ng, layer, batch, seq):
    T = hq.shape[0]
    rb = _row_tile(seq, 128)
    wl = 2 * LANES
    n_prog = HGRN_WIDTH // wl
    blk = lambda: pl.BlockSpec((seq, wl), lambda b, p: (b, p))
    par = lambda: pl.BlockSpec((None, 1, wl), lambda b, p: (layer * n_prog + p, 0, 0))
    return pl.pallas_call(
        functools.partial(_hgrn_kernel, seq=seq, rb=rb, wl=wl),
        grid=(batch, n_prog),
        in_specs=[blk(), blk(), blk(), blk(), par(), par()],
        out_specs=blk(),
        out_shape=jax.ShapeDtypeStruct((T, HGRN_WIDTH), BF16),
        scratch_shapes=[pltpu.VMEM((wl // LANES, LANES, LANES), F32),
                        pltpu.VMEM((HGRN_CHUNK + rb, wl), F32),
                        pltpu.VMEM((HGRN_CHUNK + rb, wl), F32),
                        pltpu.VMEM((HGRN_CHUNK * rb, wl), BF16),
                        pltpu.VMEM((HGRN_CHUNK * rb, wl), F32)],
        compiler_params=_params(("parallel", "parallel")),
        name="hgrn2",
    )(hq, hf, hi, hg, lb.reshape(-1, 1, wl), ng.reshape(-1, 1, wl))


def _merge_kernel(fo_ref, ho_ref, gates_ref, x_ref, wf_ref, wh_ref, wm_ref, bm_ref, g_ref, b_ref,
                  wrh_ref, wrl_ref, rb_ref, tri_ref, x1r_ref, route_ref, cnt_ref,
                  *, alpha, d_model):
    i = pl.program_id(0)
    tm = x_ref.shape[0]

    @pl.when(i == 0)
    def _():
        cnt_ref[...] = jnp.zeros(cnt_ref.shape, F32)

    y_fox = _dot(fo_ref[...], wf_ref[...])
    y_hgrn = _dot(ho_ref[...], wh_ref[...])
    g_fox = _sigmoid(gates_ref[:, :d_model].astype(F32))
    g_hgrn = _sigmoid(gates_ref[:, d_model:].astype(F32))
    mixed = _dot((g_fox * y_fox + g_hgrn * y_hgrn).astype(BF16), wm_ref[...]) + bm_ref[...]
    x1 = _layer_norm(alpha * x_ref[...] + mixed, g_ref[...], b_ref[...])
    nj = d_model // LANES
    for j in range(nj):
        x1r_ref[pl.ds(j, tm, stride=nj), :] = x1[:, j * LANES:(j + 1) * LANES]

    xh = x1.astype(BF16)
    xl = (x1 - xh.astype(F32)).astype(BF16)
    logits = _dot(xh, wrh_ref[...]) + _dot(xl, wrh_ref[...]) + _dot(xh, wrl_ref[...]) + rb_ref[...]
    lane = lax.broadcasted_iota(I32, (tm, LANES), 1)
    lane_f = lane.astype(F32)
    neg = jnp.float32(-jnp.inf)
    lg = jnp.where(lane < N_EXPERTS, logits, neg)
    m1 = jnp.max(lg, axis=-1, keepdims=True)
    idx1 = jnp.min(jnp.where(lg == m1, lane_f, float(LANES)), axis=-1, keepdims=True).astype(I32)
    in_group = (lane // EXPERTS_PER_GROUP == idx1 // EXPERTS_PER_GROUP) & (lane < N_EXPERTS)
    lg2 = jnp.where(in_group & (lane != idx1), logits, neg)
    m2 = jnp.max(lg2, axis=-1, keepdims=True)
    idx2 = jnp.min(jnp.where(lg2 == m2, lane_f, float(LANES)), axis=-1, keepdims=True).astype(I32)
    e21 = jnp.exp(m2 - m1)
    gate1 = 1.0 / (1.0 + e21)
    gate2 = e21 / (1.0 + e21)

    oh1 = lane == idx1
    oh2 = lane == idx2
    oh = (oh1 | oh2).astype(F32)
    before = _dot(tri_ref[...], oh.astype(BF16)) + cnt_ref[0:1, :]
    rank1 = jnp.sum(jnp.where(oh1, before, 0.0), axis=-1, keepdims=True)
    rank2 = jnp.sum(jnp.where(oh2, before, 0.0), axis=-1, keepdims=True)
    cnt_ref[0:1, :] = cnt_ref[0:1, :] + jnp.sum(oh, axis=0, keepdims=True)

    route = jnp.where(lane == 0, idx1.astype(F32), 0.0)
    route = jnp.where(lane == 1, idx2.astype(F32), route)
    route = jnp.where(lane == 2, gate1, route)
    route = jnp.where(lane == 3, gate2, route)
    route = jnp.where(lane == 4, rank1, route)
    route = jnp.where(lane == 5, rank2, route)
    route_ref[...] = route[:, :ROUTE_COLS]


def _merge(fox_o, hgrn_o, gates, x, wf, wh, wm, bm, g, b, wrh, wrl, rbias, tri, layer, alpha):
    T, D = x.shape
    tm = tri.shape[0]
    nj = D // LANES
    row = lambda n: pl.BlockSpec((tm, n), lambda i: (i, 0))
    lw = lambda r, c: pl.BlockSpec((None, r, c), lambda i: (layer, 0, 0))
    cw = lambda r, c: pl.BlockSpec((r, c), lambda i: (0, 0))
    return pl.pallas_call(
        functools.partial(_merge_kernel, alpha=alpha, d_model=D),
        grid=(T // tm,),
        in_specs=[row(FOX_WIDTH), row(HGRN_WIDTH), row(2 * D), row(D),
                  lw(FOX_WIDTH, D), lw(HGRN_WIDTH, D), lw(D, D), lw(1, D), lw(1, D), lw(1, D),
                  cw(D, LANES), cw(D, LANES), cw(1, LANES), cw(tm, tm)],
        out_specs=[pl.BlockSpec((tm * nj, LANES), lambda i: (i, 0)), row(ROUTE_COLS),
                   pl.BlockSpec((SUBLANES, LANES), lambda i: (0, 0))],
        out_shape=[jax.ShapeDtypeStruct((T * nj, LANES), F32),
                   jax.ShapeDtypeStruct((T, ROUTE_COLS), F32), jax.ShapeDtypeStruct((SUBLANES, LANES), F32)],
        compiler_params=_params(("arbitrary",)),
        name="merge_router",
    )(fox_o, hgrn_o, gates, x, wf, wh, wm, bm, g, b, wrh, wrl, rbias, tri)


def _dispatch_kernel(dest_ref, x_ref, xs_hbm, inv_ref, sem, *, tc, nj):
    i = pl.program_id(0)

    def row_copy(r, d):
        return pltpu.make_async_copy(
            x_ref.at[pl.ds(pl.multiple_of(r * nj, nj), nj), :],
            xs_hbm.at[pl.ds(pl.multiple_of(d * nj, nj), nj), :], sem)

    def issue(r, _):
        d0 = dest_ref[0, 0, 2 * r]
        d1 = dest_ref[0, 0, 2 * r + 1]
        assignment = 2 * (i * tc + r)
        inv_ref[d0] = assignment
        inv_ref[d1] = assignment + 1
        row_copy(r, d0).start()
        row_copy(r, d1).start()
        return 0

    lax.fori_loop(0, tc, issue, 0, unroll=8)

    def drain(r, _):
        row_copy(0, 0).wait()
        row_copy(0, 0).wait()
        return 0

    lax.fori_loop(0, tc, drain, 0, unroll=8)


def _dispatch(x1r, dest, n_tokens, nj):
    tc = _row_tile(n_tokens, 512)
    n_steps = n_tokens // tc
    return pl.pallas_call(
        functools.partial(_dispatch_kernel, tc=tc, nj=nj),
        grid=(n_steps,),
        in_specs=[pl.BlockSpec((1, 1, 2 * tc), lambda i: (i, 0, 0), memory_space=pltpu.SMEM),
                  pl.BlockSpec((tc * nj, LANES), lambda i: (i, 0))],
        out_specs=[pl.BlockSpec(memory_space=pl.ANY), pl.BlockSpec(memory_space=pltpu.SMEM)],
        out_shape=[jax.ShapeDtypeStruct((2 * n_tokens * nj, LANES), F32),
                   jax.ShapeDtypeStruct((2 * n_tokens,), I32)],
        scratch_shapes=[pltpu.SemaphoreType.DMA(())],
        compiler_params=pltpu.CompilerParams(dimension_semantics=("arbitrary",), has_side_effects=True),
        name="moe_dispatch",
    )(dest.reshape(n_steps, 1, 2 * tc), x1r)


def _experts_kernel(meta_ref, slot_ref, xs_ref, w1_ref, w3_ref, w2_ref, y2_hbm, ybuf_ref, sem,
                    *, tm, nj, n_assign):
    w = pl.program_id(0)
    lo = meta_ref[2, w]
    hi = meta_ref[3, w]
    prev = jnp.maximum(w - 1, 0)
    prev_real = (w >= 1) & (meta_ref[3, prev] > meta_ref[2, prev])
    cur_par = w % 2
    prev_par = 1 - cur_par

    def row_copy(par, r, slot):
        return pltpu.make_async_copy(
            ybuf_ref.at[par, pl.ds(pl.multiple_of(r * nj, nj), nj), :],
            y2_hbm.at[pl.ds(pl.multiple_of(slot * nj, nj), nj), :], sem.at[par])

    def send_previous_rows():
        for r in range(tm):
            row_copy(prev_par, r, slot_ref[0, 0, r]).start()

    def wait_rows(par):
        pltpu.make_async_copy(ybuf_ref.at[par], y2_hbm.at[pl.ds(0, tm * nj), :], sem.at[par]).wait()

    @pl.when(w == 0)
    def _():
        ybuf_ref[...] = jnp.zeros(ybuf_ref.shape, F32)
        spare0 = pltpu.make_async_copy(ybuf_ref.at[0], y2_hbm.at[pl.ds(n_assign * nj, tm * nj), :], sem.at[0])
        spare0.start()
        spare0.wait()

    @pl.when(hi > lo)
    def _():
        send_previous_rows()
        x = jnp.concatenate([xs_ref[pl.ds(j, tm, stride=nj), :] for j in range(nj)], axis=-1).astype(BF16)
        h1 = _dot(x, w1_ref[...].astype(BF16))
        h3 = _dot(x, w3_ref[...].astype(BF16))
        h = (h1 * _sigmoid(h1) * h3).astype(BF16)
        y = _dot(h, w2_ref[...].astype(BF16))

        @pl.when(w >= 1)
        def _():
            wait_rows(cur_par)

        for j in range(nj):
            ybuf_ref[cur_par, pl.ds(j, tm, stride=nj), :] = y[:, j * LANES:(j + 1) * LANES]

    @pl.when((hi == lo) & prev_real)
    def _():
        send_previous_rows()
        wait_rows(0)
        wait_rows(1)


def _experts(meta, inv, xs, w1, w3, w2, layer, tm, nj):
    n_items = meta.shape[1]
    D = nj * LANES
    dh = w1.shape[-1]
    n_assign = xs.shape[0] // nj
    tile, lo, hi = meta[0], meta[2], meta[3]
    r = jnp.arange(tm, dtype=I32)[None, :]
    owned = (r >= lo[:, None]) & (r < hi[:, None])
    spare = n_assign + (jnp.arange(n_items, dtype=I32)[:, None] % 2) * tm + r
    slots = jnp.where(owned, inv[tile[:, None] * tm + r], spare)
    slots = jnp.concatenate([n_assign + tm + r, slots[:-1]], axis=0).reshape(n_items, 1, tm)
    grid_spec = pltpu.PrefetchScalarGridSpec(
        num_scalar_prefetch=1,
        grid=(n_items,),
        in_specs=[pl.BlockSpec((1, 1, tm), lambda w, m: (w, 0, 0), memory_space=pltpu.SMEM),
                  pl.BlockSpec((tm * nj, LANES), lambda w, m: (m[0, w], 0)),
                  pl.BlockSpec((None, None, D, dh), lambda w, m: (layer, m[1, w], 0, 0)),
                  pl.BlockSpec((None, None, D, dh), lambda w, m: (layer, m[1, w], 0, 0)),
                  pl.BlockSpec((None, None, dh, D), lambda w, m: (layer, m[1, w], 0, 0))],
        out_specs=pl.BlockSpec(memory_space=pl.ANY),
        scratch_shapes=[pltpu.VMEM((2, tm * nj, LANES), F32), pltpu.SemaphoreType.DMA((2,))],
    )
    return pl.pallas_call(
        functools.partial(_experts_kernel, tm=tm, nj=nj, n_assign=n_assign),
        grid_spec=grid_spec,
        out_shape=jax.ShapeDtypeStruct(((n_assign + 2 * tm) * nj, LANES), F32),
        compiler_params=pltpu.CompilerParams(dimension_semantics=("arbitrary",), vmem_limit_bytes=VMEM_LIMIT,
                                             has_side_effects=True),
        name="moe_experts",
    )(meta, slots, xs, w1, w3, w2)


def _combine_kernel(y2_ref, route_ref, x1r_ref, g_ref, b_ref, o_ref, *, tc, nj, alpha):
    gate1 = route_ref[:, 2:3]
    gate2 = route_ref[:, 3:4]
    resid = jnp.concatenate(
        [alpha * x1r_ref[pl.ds(j, tc, stride=nj), :]
         + gate1 * y2_ref[pl.ds(j, tc, stride=TOP_K * nj), :]
         + gate2 * y2_ref[pl.ds(nj + j, tc, stride=TOP_K * nj), :]
         for j in range(nj)], axis=-1)
    o_ref[...] = _layer_norm(resid, g_ref[...], b_ref[...])


def _combine(y2, route, x1r, g, b, layer, alpha):
    T = route.shape[0]
    nj = x1r.shape[0] // T
    D = nj * LANES
    tc = _row_tile(T, 512)
    row = lambda n: pl.BlockSpec((tc, n), lambda i: (i, 0))
    lw = lambda r, c: pl.BlockSpec((None, r, c), lambda i: (layer, 0, 0))
    return pl.pallas_call(
        functools.partial(_combine_kernel, tc=tc, nj=nj, alpha=alpha),
        grid=(T // tc,),
        in_specs=[pl.BlockSpec((tc * TOP_K * nj, LANES), lambda i: (i, 0)), row(ROUTE_COLS),
                  pl.BlockSpec((tc * nj, LANES), lambda i: (i, 0)), lw(1, D), lw(1, D)],
        out_specs=row(D),
        out_shape=jax.ShapeDtypeStruct((T, D), F32),
        compiler_params=_params(("parallel",)),
        name="moe_combine",
    )(y2, route, x1r, g, b)


def _routing_tables(route, counts, tm, n_items):
    e = route[:, 0:2].astype(I32)
    rank = route[:, 4:6].astype(I32)
    cnt = counts[0, :N_EXPERTS].astype(I32)
    ends = jnp.cumsum(cnt)
    starts = ends - cnt
    dest = (starts[e] + rank).reshape(-1)

    first_tile = starts // tm
    last_tile = jnp.maximum(ends - 1, 0) // tm
    n_tiles_e = jnp.where(cnt > 0, last_tile - first_tile + 1, 0)
    item_end = jnp.cumsum(n_tiles_e)
    item_start = item_end - n_tiles_e
    n_real = item_end[-1]
    w = jnp.arange(n_items, dtype=I32)
    wc = jnp.minimum(w, n_real - 1)
    ex = jnp.sum((item_end[None, :] <= wc[:, None]).astype(I32), axis=1)
    tile = first_tile[ex] + (wc - item_start[ex])
    lo = jnp.clip(starts[ex] - tile * tm, 0, tm)
    hi = jnp.clip(ends[ex] - tile * tm, 0, tm)
    real = w < n_real
    hi = jnp.where(real, hi, lo)
    meta = jnp.stack([tile, ex, lo, hi]).astype(I32)
    return dest, meta


def kernel(x, ln_in_g, ln_in_b, w_in, b_in, w_fox_branch, hgrn_lb_logits, hgrn_norm_g, w_hgrn_branch,
           w_mix_out, b_mix_out, ln1_g, ln1_b, router_w, router_b, expert_w1, expert_w3, expert_w2,
           ln2_g, ln2_b):
    batch, seq, D = x.shape
    depth = w_in.shape[0]
    T = batch * seq
    nj = D // LANES
    alpha = float((2 * depth) ** 0.25)
    assert D % 512 == 0 and seq % HGRN_CHUNK == 0

    sizes = (FOX_WIDTH, FOX_WIDTH, FOX_WIDTH, FOX_HEADS, HGRN_WIDTH, HGRN_WIDTH, HGRN_WIDTH, HGRN_WIDTH, 2 * D)
    offs = [0]
    for s in sizes:
        offs.append(offs[-1] + s)
    col = lambda a, i: a[..., offs[i]:offs[i + 1]]
    order = (0, 1, 2, 4, 6, 7, 8, 5)
    pad_ff = lambda a: jnp.pad(col(a, 3), [(0, 0)] * (a.ndim - 1) + [(0, LANES - FOX_HEADS)])
    w_all = jnp.concatenate([col(w_in, i) for i in order] + [pad_ff(w_in)], axis=-1).astype(BF16)
    b_all = jnp.concatenate([col(b_in, i) for i in order] + [pad_ff(b_in)], axis=-1).astype(F32)[:, None, :]

    lb_p = jax.nn.softmax(hgrn_lb_logits.astype(F32), axis=0)
    lb_all = (jnp.cumsum(lb_p, axis=0) - lb_p[0]).reshape(depth * HGRN_PAIRS, 1, LANES)
    ng_all = hgrn_norm_g.astype(F32).reshape(depth * HGRN_PAIRS, 1, LANES)

    wf = w_fox_branch.astype(BF16)
    wh = w_hgrn_branch.astype(BF16)
    wm = w_mix_out.astype(BF16)
    r3 = lambda a: a.astype(F32)[:, None, :]
    bm, g1, b1, g2, b2 = r3(b_mix_out), r3(ln1_g), r3(ln1_b), r3(ln2_g), r3(ln2_b)
    rw = jnp.pad(router_w.astype(F32), ((0, 0), (0, LANES - N_EXPERTS)))
    wrh = rw.astype(BF16)
    wrl = (rw - wrh.astype(F32)).astype(BF16)
    rbias = jnp.pad(router_b.astype(F32), (0, LANES - N_EXPERTS)).reshape(1, LANES)

    tm_merge = _row_tile(T, 512)
    tri = jnp.tril(jnp.ones((tm_merge, tm_merge), BF16), k=-1)
    tm_exp = _row_tile(2 * T, 512)
    n_items = (2 * T) // tm_exp + N_EXPERTS
    tm_in = _row_tile(seq, 512)
    fox_block = _row_tile(tm_in, 256)
    tri_in = jnp.tril(jnp.ones((tm_in, tm_in), BF16))

    xc = _ln_in(x.reshape(T, D), ln_in_g.astype(F32), ln_in_b.astype(F32))
    for l in range(depth):
        qp, kp, vt, hq, hi, hg, gates, hf = _inproj(xc, w_all, b_all, tri_in, l, seq, fox_block)
        fox_o = _fox_attention(qp, kp, vt, batch, seq, fox_block)
        hgrn_o = _hgrn(hq, hf, hi, hg, lb_all, ng_all, l, batch, seq)
        x1r, route, counts = _merge(fox_o, hgrn_o, gates, xc, wf, wh, wm, bm, g1, b1,
                                    wrh, wrl, rbias, tri, l, alpha)
        dest, meta = _routing_tables(route, counts, tm_exp, n_items)
        xs, inv = _dispatch(x1r, dest, T, nj)
        y2 = _experts(meta, inv, xs, expert_w1, expert_w3, expert_w2, l, tm_exp, nj)
        xc = _combine(y2, route, x1r, g2, b2, l, alpha)
    return xc.reshape(batch, seq, D)
```

```python
import functools

import jax
import jax.numpy as jnp
from jax import lax
from jax.experimental import pallas as pl
from jax.experimental.pallas import tpu as pltpu

F32 = jnp.float32
BF16 = jnp.bfloat16
I32 = jnp.int32

LANES = 128
SUBLANES = 8
HEAD_DIM = 64
FOX_HEADS = 8
HGRN_HEADS = 8
HEADS_PER_VREG = LANES // HEAD_DIM
FOX_PAIRS = FOX_HEADS // HEADS_PER_VREG
HGRN_PAIRS = HGRN_HEADS // HEADS_PER_VREG
FOX_WIDTH = FOX_HEADS * HEAD_DIM
HGRN_WIDTH = HGRN_HEADS * HEAD_DIM
N_EXPERTS = 16
N_GROUPS = 4
EXPERTS_PER_GROUP = N_EXPERTS // N_GROUPS
TOP_K = 2
ROUTE_COLS = 8
HGRN_CHUNK = 16
LN_EPS = 1e-5
RMS_EPS = 1e-6
MASK_VALUE = -1e30
LOG2E = 1.4426950408889634
VMEM_LIMIT = 56 * 1024 * 1024

_C_FQ = 0
_C_FK = _C_FQ + FOX_WIDTH
_C_FV = _C_FK + FOX_WIDTH
_C_HQ = _C_FV + FOX_WIDTH
_C_HI = _C_HQ + HGRN_WIDTH
_C_HG = _C_HI + HGRN_WIDTH
_C_GATES = _C_HG + HGRN_WIDTH


def _params(sem, vmem=VMEM_LIMIT):
    return pltpu.CompilerParams(dimension_semantics=sem, vmem_limit_bytes=vmem)


def _split3(x):
    hi = x.astype(BF16)
    r1 = x - hi.astype(F32)
    mid = r1.astype(BF16)
    lo = (r1 - mid.astype(F32)).astype(BF16)
    return hi, mid, lo


def _dot(a, b):
    return jnp.dot(a, b, preferred_element_type=F32)


def _dot_nt(a, b):
    return lax.dot_general(a, b, (((1,), (1,)), ((), ())), preferred_element_type=F32)


def _dot_tn(a, b):
    return lax.dot_general(a, b, (((0,), (0,)), ((), ())), preferred_element_type=F32)


def _dot_exact_lhs(m, x):
    hi, mid, lo = _split3(x)
    return _dot(m, hi) + _dot(m, mid) + _dot(m, lo)


def _dot_exact_rhs(x, m):
    hi, mid, lo = _split3(x)
    return _dot(hi, m) + _dot(mid, m) + _dot(lo, m)


def _sigmoid(x):
    return 1.0 / (1.0 + jnp.exp(-x))


def _layer_norm(x, g, b):
    mu = jnp.mean(x, axis=-1, keepdims=True)
    xc = x - mu
    var = jnp.mean(xc * xc, axis=-1, keepdims=True)
    return xc * lax.rsqrt(var + LN_EPS) * g + b


def _row_tile(n, want):
    t = min(n, want)
    assert n % t == 0, (n, t)
    return t


def _ln_kernel(x_ref, g_ref, b_ref, o_ref):
    o_ref[...] = _layer_norm(x_ref[...], g_ref[...], b_ref[...])


def _ln_in(x, g, b):
    T, D = x.shape
    tm = _row_tile(T, 512)
    return pl.pallas_call(
        _ln_kernel,
        grid=(T // tm,),
        in_specs=[pl.BlockSpec((tm, D), lambda i: (i, 0)),
                  pl.BlockSpec((1, D), lambda i: (0, 0)),
                  pl.BlockSpec((1, D), lambda i: (0, 0))],
        out_specs=pl.BlockSpec((tm, D), lambda i: (i, 0)),
        out_shape=jax.ShapeDtypeStruct((T, D), F32),
        compiler_params=_params(("parallel",)),
        name="ln_in",
    )(x, g.reshape(1, D), b.reshape(1, D))


def _fox_head_lanes(h):
    lane = lax.broadcasted_iota(I32, (1, LANES), 1)
    hh = h % HEADS_PER_VREG
    own = (lane >= hh * HEAD_DIM) & (lane < (hh + 1) * HEAD_DIM)
    e0 = (1 - hh) * HEAD_DIM
    return lane, own, e0


def _inproj_kernel(x_ref, w_ref, b_ref, tri_ref, qp_ref, kp_ref, vt_ref, hq_ref, hi_ref, hg_ref,
                   gates_ref, hf_ref, fc_ref, *, d_model, tiles_per_seq, cb):
    i = pl.program_id(0)
    tm = x_ref.shape[0]
    xb = x_ref[...].astype(BF16)

    def proj(c0, n):
        return _dot(xb, w_ref[:, c0:c0 + n]) + b_ref[:, c0:c0 + n]

    c_hf = _C_GATES + 2 * d_model

    @pl.when(i % tiles_per_seq == 0)
    def _():
        fc_ref[...] = jnp.zeros(fc_ref.shape, F32)

    ff = proj(c_hf + HGRN_WIDTH, LANES)
    log_f = jnp.minimum(ff, 0.0) - jnp.log(1.0 + jnp.exp(-jnp.abs(ff)))
    f_cum = _dot_exact_lhs(tri_ref[...], log_f * LOG2E) + fc_ref[...]
    fc_ref[...] = f_cum[tm - 1:tm, :]

    q_all = proj(_C_FQ, FOX_WIDTH) * (HEAD_DIM ** -0.5 * LOG2E)
    k_all = proj(_C_FK, FOX_WIDTH)
    v_all = proj(_C_FV, FOX_WIDTH)
    for h in range(FOX_HEADS):
        p = h // HEADS_PER_VREG
        cols = slice(p * LANES, (p + 1) * LANES)
        lane, own, e0 = _fox_head_lanes(h)
        hi, mid, lo = (t.astype(F32) for t in _split3(f_cum[:, h:h + 1]))
        ext_q = jnp.where((lane >= e0 + 3) & (lane < e0 + 6), 1.0, 0.0)
        ext_q = jnp.where(lane == e0, hi, ext_q)
        ext_q = jnp.where(lane == e0 + 1, mid, ext_q)
        ext_q = jnp.where(lane == e0 + 2, lo, ext_q)
        ext_k = jnp.where((lane >= e0) & (lane < e0 + 3), 1.0, 0.0)
        ext_k = jnp.where(lane == e0 + 3, -hi, ext_k)
        ext_k = jnp.where(lane == e0 + 4, -mid, ext_k)
        ext_k = jnp.where(lane == e0 + 5, -lo, ext_k)
        qp_ref[h] = jnp.where(own, q_all[:, cols], ext_q).astype(BF16)
        kp = jnp.where(own, k_all[:, cols], ext_k).astype(BF16)
        vt = jnp.where(own, v_all[:, cols], 0.0).T
        for blk in range(tm // cb):
            kp_ref[h, blk] = kp[blk * cb:(blk + 1) * cb]
            vt_ref[h, blk] = vt[:, blk * cb:(blk + 1) * cb].astype(BF16)

    hq_ref[...] = proj(_C_HQ, HGRN_WIDTH).astype(BF16)
    hi_ref[...] = proj(_C_HI, HGRN_WIDTH).astype(BF16)
    hg_ref[...] = proj(_C_HG, HGRN_WIDTH).astype(BF16)
    n_gate_chunks = 2 * d_model // 512
    for c in range(n_gate_chunks):
        gates_ref[:, c * 512:(c + 1) * 512] = proj(_C_GATES + c * 512, 512).astype(BF16)
    hf_ref[...] = proj(c_hf, HGRN_WIDTH)


def _inproj(x, w, b, tri, layer, seq, cb):
    T, D = x.shape
    n_all = w.shape[-1]
    tm = tri.shape[0]
    H = FOX_HEADS
    row = lambda n: pl.BlockSpec((tm, n), lambda i: (i, 0))
    outs = [(HGRN_WIDTH, BF16)] * 3 + [(2 * D, BF16), (HGRN_WIDTH, F32)]
    return pl.pallas_call(
        functools.partial(_inproj_kernel, d_model=D, tiles_per_seq=seq // tm, cb=cb),
        grid=(T // tm,),
        in_specs=[row(D),
                  pl.BlockSpec((None, D, n_all), lambda i: (layer, 0, 0), pipeline_mode=pl.Buffered(1)),
                  pl.BlockSpec((None, 1, n_all), lambda i: (layer, 0, 0), pipeline_mode=pl.Buffered(1)),
                  pl.BlockSpec((tm, tm), lambda i: (0, 0), pipeline_mode=pl.Buffered(1))],
        out_specs=[pl.BlockSpec((H, tm, LANES), lambda i: (0, i, 0)),
                   pl.BlockSpec((H, tm // cb, cb, LANES), lambda i: (0, i, 0, 0)),
                   pl.BlockSpec((H, tm // cb, LANES, cb), lambda i: (0, i, 0, 0))]
                  + [row(n) for n, _ in outs],
        out_shape=[jax.ShapeDtypeStruct((H, T, LANES), BF16),
                   jax.ShapeDtypeStruct((H, T // cb, cb, LANES), BF16),
                   jax.ShapeDtypeStruct((H, T // cb, LANES, cb), BF16)]
                  + [jax.ShapeDtypeStruct((T, n), dt) for n, dt in outs],
        scratch_shapes=[pltpu.VMEM((1, LANES), F32)],
        compiler_params=_params(("arbitrary",)),
        name="inproj",
    )(x, w, b, tri)


def _fox_kernel(qp_ref, kp_ref, vt_ref, o_ref, m_ref, l_ref, acc_ref, s_ref, p_ref, a_ref, *, tq, cb):
    i = pl.program_id(1)
    key_i = lax.broadcasted_iota(I32, (tq, tq), 0)
    qry_i = lax.broadcasted_iota(I32, (tq, tq), 1)
    causal = key_i <= qry_i
    for h in range(FOX_HEADS):
        m_ref[h] = jnp.full((1, tq), MASK_VALUE, F32)
        l_ref[h] = jnp.zeros((1, tq), F32)
        acc_ref[h] = jnp.zeros((LANES, tq), F32)

    def scores(j, par):
        for h in range(FOX_HEADS):
            s_ref[par * FOX_HEADS + h] = _dot_nt(kp_ref[h, j], qp_ref[h])

    def softmax(par, masked):
        for h in range(FOX_HEADS):
            st = s_ref[par * FOX_HEADS + h]
            if masked:
                st = jnp.where(causal, st, MASK_VALUE)
            m = m_ref[h]
            m_new = jnp.maximum(m, jnp.max(st, axis=0, keepdims=True))
            a = jnp.exp2(m - m_new)
            pt = jnp.exp2(st - m_new)
            m_ref[h] = m_new
            l_ref[h] = a * l_ref[h] + jnp.sum(pt, axis=0, keepdims=True)
            p_ref[par * FOX_HEADS + h] = pt.astype(BF16)
            a_ref[par * FOX_HEADS + h] = a

    def values(j, par):
        for h in range(FOX_HEADS):
            acc_ref[h] = a_ref[par * FOX_HEADS + h] * acc_ref[h] + _dot(vt_ref[h, j],
                                                                     p_ref[par * FOX_HEADS + h])

    for h in range(FOX_HEADS):
        p_ref[FOX_HEADS + h] = jnp.zeros((cb, tq), BF16)
        a_ref[FOX_HEADS + h] = jnp.ones((1, tq), F32)
    scores(0, 0)

    def trip(t, par):
        scores(t + 1, 1 - par)
        softmax(par, False)
        values(jnp.maximum(t - 1, 0), 1 - par)

    def two_trips(u, _):
        trip(2 * u, 0)
        trip(2 * u + 1, 1)
        return 0

    lax.fori_loop(0, i // 2, two_trips, 0)

    @pl.when(i % 2 == 1)
    def _():
        trip(i - 1, 0)
        softmax(1, True)
        values(i - 1, 0)
        values(i, 1)

    @pl.when(i % 2 == 0)
    def _():
        softmax(0, True)
        values(jnp.maximum(i - 1, 0), 1)
        values(i, 0)
    for p in range(FOX_PAIRS):
        h0 = p * HEADS_PER_VREG
        out_t = acc_ref[h0] / l_ref[h0]
        for h in range(h0 + 1, h0 + HEADS_PER_VREG):
            out_t = out_t + acc_ref[h] / l_ref[h]
        o_ref[:, p * LANES:(p + 1) * LANES] = out_t.T.astype(BF16)


def _fox_attention(qp, kp, vt, batch, seq, cb):
    H, T, _ = qp.shape
    tq = cb
    nq = seq // tq
    return pl.pallas_call(
        functools.partial(_fox_kernel, tq=tq, cb=cb),
        grid=(batch, nq),
        in_specs=[pl.BlockSpec((H, tq, LANES), lambda b, i: (0, b * nq + i, 0)),
                  pl.BlockSpec((H, nq, cb, LANES), lambda b, i: (0, b, 0, 0)),
                  pl.BlockSpec((H, nq, LANES, cb), lambda b, i: (0, b, 0, 0))],
        out_specs=pl.BlockSpec((tq, FOX_WIDTH), lambda b, i: (b * nq + i, 0)),
        out_shape=jax.ShapeDtypeStruct((T, FOX_WIDTH), BF16),
        scratch_shapes=[pltpu.VMEM((FOX_HEADS, 1, tq), F32),
                        pltpu.VMEM((FOX_HEADS, 1, tq), F32),
                        pltpu.VMEM((FOX_HEADS, LANES, tq), F32),
                        pltpu.VMEM((2 * FOX_HEADS, cb, tq), F32),
                        pltpu.VMEM((2 * FOX_HEADS, cb, tq), BF16),
                        pltpu.VMEM((2 * FOX_HEADS, 1, tq), F32)],
        compiler_params=_params(("parallel", "arbitrary")),
        name="fox_attention",
    )(qp, kp, vt)


def _hgrn_kernel(hq_ref, hf_ref, hi_ref, hg_ref, lb_ref, ng_ref, o_ref, ss_ref, cpad_ref, vpad_ref,
                 term_ref, score_ref, *, seq, rb, wl):
    ch = HGRN_CHUNK
    n_states = wl // LANES
    rr = lax.broadcasted_iota(I32, (rb, rb), 0)
    cc = lax.broadcasted_iota(I32, (rb, rb), 1)
    same_chunk = (rr // ch) == (cc // ch)
    cum_mat = jnp.concatenate([(same_chunk & (cc <= rr)).astype(BF16), same_chunk.astype(BF16)], axis=0)
    hr = lax.broadcasted_iota(I32, (wl, wl), 0) // HEAD_DIM
    hc = lax.broadcasted_iota(I32, (wl, wl), 1) // HEAD_DIM
    head_ones = (hr == hc).astype(BF16)
    head_mask = (hr == hc)[:LANES, :LANES].astype(F32)
    tmod = lax.broadcasted_iota(I32, (rb, wl), 0) % ch
    lb = lb_ref[...]
    ng = ng_ref[...]
    ss_ref[...] = jnp.zeros(ss_ref.shape, F32)
    cpad_ref[0:ch, :] = jnp.zeros((ch, wl), F32)
    vpad_ref[0:ch, :] = jnp.zeros((ch, wl), F32)

    def block(r, _):
        rows = pl.ds(pl.multiple_of(r * rb, rb), rb)
        z = hf_ref[rows, :]
        g = jnp.log(lb + (1.0 - lb) * _sigmoid(z))
        k = (1.0 - lb) * _sigmoid(-z)
        hq = hq_ref[rows, :].astype(F32)
        qs = hq * _sigmoid(hq)
        v = hi_ref[rows, :].astype(F32)
        cums = _dot_exact_lhs(cum_mat, g * LOG2E)
        b = cums[:rb]
        btot = cums[rb:]
        qd = (qs * jnp.exp2(b)).astype(BF16)
        kd = (k * jnp.exp2(btot - b)).astype(BF16)
        dec = jnp.exp2(btot)

        c = b - jnp.log2(k)
        cpad_ref[ch:ch + rb, :] = c
        vpad_ref[ch:ch + rb, :] = v
        term_ref[0:rb, :] = (qs * k).astype(BF16)
        for o in range(1, ch):
            cs = cpad_ref[ch - o:ch - o + rb, :]
            term = jnp.where(tmod >= o, qs * jnp.exp2(b - cs), 0.0)
            term_ref[o * rb:(o + 1) * rb, :] = term.astype(BF16)
        score_ref[...] = _dot(term_ref[...], head_ones)
        acc = score_ref[0:rb, :] * v
        for o in range(1, ch):
            acc = acc + score_ref[o * rb:(o + 1) * rb, :] * vpad_ref[ch - o:ch - o + rb, :]

        vb = v.astype(BF16)
        chunks = [slice(n * ch, (n + 1) * ch) for n in range(rb // ch)]
        inter_cols = []
        for sidx in range(n_states):
            ln = slice(sidx * LANES, (sidx + 1) * LANES)
            upds = [_dot_tn(vb[sl, ln], kd[sl, ln]) * head_mask for sl in chunks]
            ss = ss_ref[sidx]
            states = []
            for n, sl in enumerate(chunks):
                states.append(ss.astype(BF16))
                ss = ss * dec[n * ch:n * ch + 1, ln] + upds[n]
            ss_ref[sidx] = ss
            inter_cols.append(jnp.concatenate(
                [_dot_nt(qd[sl, ln], st) for sl, st in zip(chunks, states)], axis=0))
        o_blk = acc + jnp.concatenate(inter_cols, axis=1)
        ms = _dot_exact_rhs(o_blk * o_blk, head_ones) * (1.0 / HEAD_DIM)
        hg = hg_ref[rows, :].astype(F32)
        o_blk = o_blk * lax.rsqrt(ms + RMS_EPS) * ng * (hg * _sigmoid(hg))
        o_ref[rows, :] = o_blk.astype(BF16)
        return 0

    lax.fori_loop(0, seq // rb, block, 0, unroll=2 if (seq // rb) % 2 == 0 else 1)


def _hgrn(hq, hf, hi, hg, lb, ng, layer, batch, seq):
    T = hq.shape[0]
    rb = _row_tile(seq, 128)
    wl = 2 * LANES
    n_prog = HGRN_WIDTH // wl
    blk = lambda: pl.BlockSpec((seq, wl), lambda b, p: (b, p))
    par = lambda: pl.BlockSpec((None, 1, wl), lambda b, p: (layer * n_prog + p, 0, 0))
    return pl.pallas_call(
        functools.partial(_hgrn_kernel, seq=seq, rb=rb, wl=wl),
        grid=(batch, n_prog),
        in_specs=[blk(), blk(), blk(), blk(), par(), par()],
        out_specs=blk(),
        out_shape=jax.ShapeDtypeStruct((T, HGRN_WIDTH), BF16),
        scratch_shapes=[pltpu.VMEM((wl // LANES, LANES, LANES), F32),
                        pltpu.VMEM((HGRN_CHUNK + rb, wl), F32),
                        pltpu.VMEM((HGRN_CHUNK + rb, wl), F32),
                        pltpu.VMEM((HGRN_CHUNK * rb, wl), BF16),
                        pltpu.VMEM((HGRN_CHUNK * rb, wl), F32)],
        compiler_params=_params(("parallel", "parallel")),
        name="hgrn2",
    )(hq, hf, hi, hg, lb.reshape(-1, 1, wl), ng.reshape(-1, 1, wl))


def _merge_kernel(fo_ref, ho_ref, gates_ref, x_ref, wf_ref, wh_ref, wm_ref, bm_ref, g_ref, b_ref,
                  wrh_ref, wrl_ref, rb_ref, tri_ref, x1r_ref, route_ref, cnt_ref,
                  *, alpha, d_model):
    i = pl.program_id(0)
    tm = x_ref.shape[0]

    @pl.when(i == 0)
    def _():
        cnt_ref[...] = jnp.zeros(cnt_ref.shape, F32)

    y_fox = _dot(fo_ref[...], wf_ref[...])
    y_hgrn = _dot(ho_ref[...], wh_ref[...])
    g_fox = _sigmoid(gates_ref[:, :d_model].astype(F32))
    g_hgrn = _sigmoid(gates_ref[:, d_model:].astype(F32))
    mixed = _dot((g_fox * y_fox + g_hgrn * y_hgrn).astype(BF16), wm_ref[...]) + bm_ref[...]
    x1 = _layer_norm(alpha * x_ref[...] + mixed, g_ref[...], b_ref[...])
    nj = d_model // LANES
    for j in range(nj):
        x1r_ref[pl.ds(j, tm, stride=nj), :] = x1[:, j * LANES:(j + 1) * LANES]

    xh = x1.astype(BF16)
    xl = (x1 - xh.astype(F32)).astype(BF16)
    logits = _dot(xh, wrh_ref[...]) + _dot(xl, wrh_ref[...]) + _dot(xh, wrl_ref[...]) + rb_ref[...]
    lane = lax.broadcasted_iota(I32, (tm, LANES), 1)
    lane_f = lane.astype(F32)
    neg = jnp.float32(-jnp.inf)
    lg = jnp.where(lane < N_EXPERTS, logits, neg)
    m1 = jnp.max(lg, axis=-1, keepdims=True)
    idx1 = jnp.min(jnp.where(lg == m1, lane_f, float(LANES)), axis=-1, keepdims=True).astype(I32)
    in_group = (lane // EXPERTS_PER_GROUP == idx1 // EXPERTS_PER_GROUP) & (lane < N_EXPERTS)
    lg2 = jnp.where(in_group & (lane != idx1), logits, neg)
    m2 = jnp.max(lg2, axis=-1, keepdims=True)
    idx2 = jnp.min(jnp.where(lg2 == m2, lane_f, float(LANES)), axis=-1, keepdims=True).astype(I32)
    e21 = jnp.exp(m2 - m1)
    gate1 = 1.0 / (1.0 + e21)
    gate2 = e21 / (1.0 + e21)

    oh1 = lane == idx1
    oh2 = lane == idx2
    oh = (oh1 | oh2).astype(F32)
    before = _dot(tri_ref[...], oh.astype(BF16)) + cnt_ref[0:1, :]
    rank1 = jnp.sum(jnp.where(oh1, before, 0.0), axis=-1, keepdims=True)
    rank2 = jnp.sum(jnp.where(oh2, before, 0.0), axis=-1, keepdims=True)
    cnt_ref[0:1, :] = cnt_ref[0:1, :] + jnp.sum(oh, axis=0, keepdims=True)

    route = jnp.where(lane == 0, idx1.astype(F32), 0.0)
    route = jnp.where(lane == 1, idx2.astype(F32), route)
    route = jnp.where(lane == 2, gate1, route)
    route = jnp.where(lane == 3, gate2, route)
    route = jnp.where(lane == 4, rank1, route)
    route = jnp.where(lane == 5, rank2, route)
    route_ref[...] = route[:, :ROUTE_COLS]


def _merge(fox_o, hgrn_o, gates, x, wf, wh, wm, bm, g, b, wrh, wrl, rbias, tri, layer, alpha):
    T, D = x.shape
    tm = tri.shape[0]
    nj = D // LANES
    row = lambda n: pl.BlockSpec((tm, n), lambda i: (i, 0))
    lw = lambda r, c: pl.BlockSpec((None, r, c), lambda i: (layer, 0, 0))
    cw = lambda r, c: pl.BlockSpec((r, c), lambda i: (0, 0))
    return pl.pallas_call(
        functools.partial(_merge_kernel, alpha=alpha, d_model=D),
        grid=(T // tm,),
        in_specs=[row(FOX_WIDTH), row(HGRN_WIDTH), row(2 * D), row(D),
                  lw(FOX_WIDTH, D), lw(HGRN_WIDTH, D), lw(D, D), lw(1, D), lw(1, D), lw(1, D),
                  cw(D, LANES), cw(D, LANES), cw(1, LANES), cw(tm, tm)],
        out_specs=[pl.BlockSpec((tm * nj, LANES), lambda i: (i, 0)), row(ROUTE_COLS),
                   pl.BlockSpec((SUBLANES, LANES), lambda i: (0, 0))],
        out_shape=[jax.ShapeDtypeStruct((T * nj, LANES), F32),
                   jax.ShapeDtypeStruct((T, ROUTE_COLS), F32), jax.ShapeDtypeStruct((SUBLANES, LANES), F32)],
        compiler_params=_params(("arbitrary",)),
        name="merge_router",
    )(fox_o, hgrn_o, gates, x, wf, wh, wm, bm, g, b, wrh, wrl, rbias, tri)


def _dispatch_kernel(dest_ref, x_ref, xs_hbm, inv_ref, sem, *, tc, nj):
    i = pl.program_id(0)

    def row_copy(r, d):
        return pltpu.make_async_copy(
            x_ref.at[pl.ds(pl.multiple_of(r * nj, nj), nj), :],
            xs_hbm.at[pl.ds(pl.multiple_of(d * nj, nj), nj), :], sem)

    def issue(r, _):
        d0 = dest_ref[0, 0, 2 * r]
        d1 = dest_ref[0, 0, 2 * r + 1]
        assignment = 2 * (i * tc + r)
        inv_ref[d0] = assignment
        inv_ref[d1] = assignment + 1
        row_copy(r, d0).start(priority=0)
        row_copy(r, d1).start(priority=1)
        return 0

    lax.fori_loop(0, tc, issue, 0, unroll=8)

    def drain(r, _):
        row_copy(0, 0).wait()
        row_copy(0, 0).wait()
        return 0

    lax.fori_loop(0, tc, drain, 0, unroll=8)


def _dispatch(x1r, dest, n_tokens, nj):
    tc = _row_tile(n_tokens, 512)
    n_steps = n_tokens // tc
    return pl.pallas_call(
        functools.partial(_dispatch_kernel, tc=tc, nj=nj),
        grid=(n_steps,),
        in_specs=[pl.BlockSpec((1, 1, 2 * tc), lambda i: (i, 0, 0), memory_space=pltpu.SMEM),
                  pl.BlockSpec((tc * nj, LANES), lambda i: (i, 0))],
        out_specs=[pl.BlockSpec(memory_space=pl.ANY), pl.BlockSpec(memory_space=pltpu.SMEM)],
        out_shape=[jax.ShapeDtypeStruct((2 * n_tokens * nj, LANES), F32),
                   jax.ShapeDtypeStruct((2 * n_tokens,), I32)],
        scratch_shapes=[pltpu.SemaphoreType.DMA(())],
        compiler_params=pltpu.CompilerParams(dimension_semantics=("arbitrary",), has_side_effects=True),
        name="moe_dispatch",
    )(dest.reshape(n_steps, 1, 2 * tc), x1r)


def _experts_kernel(meta_ref, slot_ref, xs_ref, w1_ref, w3_ref, w2_ref, y2_hbm, ybuf_ref, sem,
                    *, tm, nj, n_assign):
    w = pl.program_id(0)
    lo = meta_ref[2, w]
    hi = meta_ref[3, w]
    prev = jnp.maximum(w - 1, 0)
    prev_real = (w >= 1) & (meta_ref[3, prev] > meta_ref[2, prev])
    cur_par = w % 2
    prev_par = 1 - cur_par

    def row_copy(par, r, slot):
        return pltpu.make_async_copy(
            ybuf_ref.at[par, pl.ds(pl.multiple_of(r * nj, nj), nj), :],
            y2_hbm.at[pl.ds(pl.multiple_of(slot * nj, nj), nj), :], sem.at[par])

    def send_previous_rows():
        for r in range(tm):
            row_copy(prev_par, r, slot_ref[0, 0, r]).start(priority=r % 2)

    def wait_rows(par):
        pltpu.make_async_copy(ybuf_ref.at[par], y2_hbm.at[pl.ds(0, tm * nj), :], sem.at[par]).wait()

    @pl.when(w == 0)
    def _():
        ybuf_ref[...] = jnp.zeros(ybuf_ref.shape, F32)
        spare0 = pltpu.make_async_copy(ybuf_ref.at[0], y2_hbm.at[pl.ds(n_assign * nj, tm * nj), :], sem.at[0])
        spare0.start()
        spare0.wait()

    @pl.when(hi > lo)
    def _():
        send_previous_rows()
        x = jnp.concatenate([xs_ref[pl.ds(j, tm, stride=nj), :] for j in range(nj)], axis=-1).astype(BF16)
        h1 = _dot(x, w1_ref[...].astype(BF16))
        h3 = _dot(x, w3_ref[...].astype(BF16))
        h = (h1 * _sigmoid(h1) * h3).astype(BF16)
        y = _dot(h, w2_ref[...].astype(BF16))

        @pl.when(w >= 1)
        def _():
            wait_rows(cur_par)

        for j in range(nj):
            ybuf_ref[cur_par, pl.ds(j, tm, stride=nj), :] = y[:, j * LANES:(j + 1) * LANES]

    @pl.when((hi == lo) & prev_real)
    def _():
        send_previous_rows()
        wait_rows(0)
        wait_rows(1)


def _experts(meta, inv, xs, w1, w3, w2, layer, tm, nj):
    n_items = meta.shape[1]
    D = nj * LANES
    dh = w1.shape[-1]
    n_assign = xs.shape[0] // nj
    tile, lo, hi = meta[0], meta[2], meta[3]
    r = jnp.arange(tm, dtype=I32)[None, :]
    owned = (r >= lo[:, None]) & (r < hi[:, None])
    spare = n_assign + (jnp.arange(n_items, dtype=I32)[:, None] % 2) * tm + r
    slots = jnp.where(owned, inv[tile[:, None] * tm + r], spare)
    slots = jnp.concatenate([n_assign + tm + r, slots[:-1]], axis=0).reshape(n_items, 1, tm)
    grid_spec = pltpu.PrefetchScalarGridSpec(
        num_scalar_prefetch=1,
        grid=(n_items,),
        in_specs=[pl.BlockSpec((1, 1, tm), lambda w, m: (w, 0, 0), memory_space=pltpu.SMEM),
                  pl.BlockSpec((tm * nj, LANES), lambda w, m: (m[0, w], 0)),
                  pl.BlockSpec((None, None, D, dh), lambda w, m: (layer, m[1, w], 0, 0)),
                  pl.BlockSpec((None, None, D, dh), lambda w, m: (layer, m[1, w], 0, 0)),
                  pl.BlockSpec((None, None, dh, D), lambda w, m: (layer, m[1, w], 0, 0))],
        out_specs=pl.BlockSpec(memory_space=pl.ANY),
        scratch_shapes=[pltpu.VMEM((2, tm * nj, LANES), F32), pltpu.SemaphoreType.DMA((2,))],
    )
    return pl.pallas_call(
        functools.partial(_experts_kernel, tm=tm, nj=nj, n_assign=n_assign),
        grid_spec=grid_spec,
        out_shape=jax.ShapeDtypeStruct(((n_assign + 2 * tm) * nj, LANES), F32),
        compiler_params=pltpu.CompilerParams(dimension_semantics=("arbitrary",), vmem_limit_bytes=VMEM_LIMIT,
                                             has_side_effects=True),
        name="moe_experts",
    )(meta, slots, xs, w1, w3, w2)


def _combine_kernel(y2_ref, route_ref, x1r_ref, g_ref, b_ref, o_ref, *, tc, nj, alpha):
    gate1 = route_ref[:, 2:3]
    gate2 = route_ref[:, 3:4]
    resid = jnp.concatenate(
        [alpha * x1r_ref[pl.ds(j, tc, stride=nj), :]
         + gate1 * y2_ref[pl.ds(j, tc, stride=TOP_K * nj), :]
         + gate2 * y2_ref[pl.ds(nj + j, tc, stride=TOP_K * nj), :]
         for j in range(nj)], axis=-1)
    o_ref[...] = _layer_norm(resid, g_ref[...], b_ref[...])


def _combine(y2, route, x1r, g, b, layer, alpha):
    T = route.shape[0]
    nj = x1r.shape[0] // T
    D = nj * LANES
    tc = _row_tile(T, 512)
    row = lambda n: pl.BlockSpec((tc, n), lambda i: (i, 0))
    lw = lambda r, c: pl.BlockSpec((None, r, c), lambda i: (layer, 0, 0))
    return pl.pallas_call(
        functools.partial(_combine_kernel, tc=tc, nj=nj, alpha=alpha),
        grid=(T // tc,),
        in_specs=[pl.BlockSpec((tc * TOP_K * nj, LANES), lambda i: (i, 0)), row(ROUTE_COLS),
                  pl.BlockSpec((tc * nj, LANES), lambda i: (i, 0)), lw(1, D), lw(1, D)],
        out_specs=row(D),
        out_shape=jax.ShapeDtypeStruct((T, D), F32),
        compiler_params=_params(("parallel",)),
        name="moe_combine",
    )(y2, route, x1r, g, b)


def _routing_tables(route, counts, tm, n_items):
    e = route[:, 0:2].astype(I32)
    rank = route[:, 4:6].astype(I32)
    cnt = counts[0, :N_EXPERTS].astype(I32)
    ends = jnp.cumsum(cnt)
    starts = ends - cnt
    dest = (starts[e] + rank).reshape(-1)

    first_tile = starts // tm
    last_tile = jnp.maximum(ends - 1, 0) // tm
    n_tiles_e = jnp.where(cnt > 0, last_tile - first_tile + 1, 0)
    item_end = jnp.cumsum(n_tiles_e)
    item_start = item_end - n_tiles_e
    n_real = item_end[-1]
    w = jnp.arange(n_items, dtype=I32)
    wc = jnp.minimum(w, n_real - 1)
    ex = jnp.sum((item_end[None, :] <= wc[:, None]).astype(I32), axis=1)
    tile = first_tile[ex] + (wc - item_start[ex])
    lo = jnp.clip(starts[ex] - tile * tm, 0, tm)
    hi = jnp.clip(ends[ex] - tile * tm, 0, tm)
    real = w < n_real
    hi = jnp.where(real, hi, lo)
    meta = jnp.stack([tile, ex, lo, hi]).astype(I32)
    return dest, meta


def kernel(x, ln_in_g, ln_in_b, w_in, b_in, w_fox_branch, hgrn_lb_logits, hgrn_norm_g, w_hgrn_branch,
           w_mix_out, b_mix_out, ln1_g, ln1_b, router_w, router_b, expert_w1, expert_w3, expert_w2,
           ln2_g, ln2_b):
    batch, seq, D = x.shape
    depth = w_in.shape[0]
    T = batch * seq
    nj = D // LANES
    alpha = float((2 * depth) ** 0.25)
    assert D % 512 == 0 and seq % HGRN_CHUNK == 0

    sizes = (FOX_WIDTH, FOX_WIDTH, FOX_WIDTH, FOX_HEADS, HGRN_WIDTH, HGRN_WIDTH, HGRN_WIDTH, HGRN_WIDTH, 2 * D)
    offs = [0]
    for s in sizes:
        offs.append(offs[-1] + s)
    col = lambda a, i: a[..., offs[i]:offs[i + 1]]
    order = (0, 1, 2, 4, 6, 7, 8, 5)
    pad_ff = lambda a: jnp.pad(col(a, 3), [(0, 0)] * (a.ndim - 1) + [(0, LANES - FOX_HEADS)])
    w_all = jnp.concatenate([col(w_in, i) for i in order] + [pad_ff(w_in)], axis=-1).astype(BF16)
    b_all = jnp.concatenate([col(b_in, i) for i in order] + [pad_ff(b_in)], axis=-1).astype(F32)[:, None, :]

    lb_p = jax.nn.softmax(hgrn_lb_logits.astype(F32), axis=0)
    lb_all = (jnp.cumsum(lb_p, axis=0) - lb_p[0]).reshape(depth * HGRN_PAIRS, 1, LANES)
    ng_all = hgrn_norm_g.astype(F32).reshape(depth * HGRN_PAIRS, 1, LANES)

    wf = w_fox_branch.astype(BF16)
    wh = w_hgrn_branch.astype(BF16)
    wm = w_mix_out.astype(BF16)
    r3 = lambda a: a.astype(F32)[:, None, :]
    bm, g1, b1, g2, b2 = r3(b_mix_out), r3(ln1_g), r3(ln1_b), r3(ln2_g), r3(ln2_b)
    rw = jnp.pad(router_w.astype(F32), ((0, 0), (0, LANES - N_EXPERTS)))
    wrh = rw.astype(BF16)
    wrl = (rw - wrh.astype(F32)).astype(BF16)
    rbias = jnp.pad(router_b.astype(F32), (0, LANES - N_EXPERTS)).reshape(1, LANES)

    tm_merge = _row_tile(T, 512)
    tri = jnp.tril(jnp.ones((tm_merge, tm_merge), BF16), k=-1)
    tm_exp = _row_tile(2 * T, 512)
    n_items = (2 * T) // tm_exp + N_EXPERTS
    tm_in = _row_tile(seq, 512)
    fox_block = _row_tile(tm_in, 256)
    tri_in = jnp.tril(jnp.ones((tm_in, tm_in), BF16))

    xc = _ln_in(x.reshape(T, D), ln_in_g.astype(F32), ln_in_b.astype(F32))
    for l in range(depth):
        qp, kp, vt, hq, hi, hg, gates, hf = _inproj(xc, w_all, b_all, tri_in, l, seq, fox_block)
        fox_o = _fox_attention(qp, kp, vt, batch, seq, fox_block)
        hgrn_o = _hgrn(hq, hf, hi, hg, lb_all, ng_all, l, batch, seq)
        x1r, route, counts = _merge(fox_o, hgrn_o, gates, xc, wf, wh, wm, bm, g1, b1,
                                    wrh, wrl, rbias, tri, l, alpha)
        dest, meta = _routing_tables(route, counts, tm_exp, n_items)
        xs, inv = _dispatch(x1r, dest, T, nj)
        y2 = _experts(meta, inv, xs, expert_w1, expert_w3, expert_w2, l, tm_exp, nj)
        xc = _combine(y2, route, x1r, g2, b2, l, alpha)
    return xc.reshape(batch, seq, D)
```

```python
import functools

import jax
import jax.numpy as jnp
from jax import lax
from jax.experimental import pallas as pl
from jax.experimental.pallas import tpu as pltpu

F32 = jnp.float32
BF16 = jnp.bfloat16
I32 = jnp.int32

LANES = 128
SUBLANES = 8
HEAD_DIM = 64
FOX_HEADS = 8
HGRN_HEADS = 8
HEADS_PER_VREG = LANES // HEAD_DIM
FOX_PAIRS = FOX_HEADS // HEADS_PER_VREG
HGRN_PAIRS = HGRN_HEADS // HEADS_PER_VREG
FOX_WIDTH = FOX_HEADS * HEAD_DIM
HGRN_WIDTH = HGRN_HEADS * HEAD_DIM
N_EXPERTS = 16
N_GROUPS = 4
EXPERTS_PER_GROUP = N_EXPERTS // N_GROUPS
TOP_K = 2
ROUTE_COLS = 8
HGRN_CHUNK = 16
LN_EPS = 1e-5
RMS_EPS = 1e-6
MASK_VALUE = -1e30
LOG2E = 1.4426950408889634
VMEM_LIMIT = 56 * 1024 * 1024

_C_FQ = 0
_C_FK = _C_FQ + FOX_WIDTH
_C_FV = _C_FK + FOX_WIDTH
_C_HQ = _C_FV + FOX_WIDTH
_C_HI = _C_HQ + HGRN_WIDTH
_C_HG = _C_HI + HGRN_WIDTH
_C_GATES = _C_HG + HGRN_WIDTH


def _params(sem, vmem=VMEM_LIMIT):
    return pltpu.CompilerParams(dimension_semantics=sem, vmem_limit_bytes=vmem)


def _split3(x):
    hi = x.astype(BF16)
    r1 = x - hi.astype(F32)
    mid = r1.astype(BF16)
    lo = (r1 - mid.astype(F32)).astype(BF16)
    return hi, mid, lo


def _dot(a, b):
    return jnp.dot(a, b, preferred_element_type=F32)


def _dot_nt(a, b):
    return lax.dot_general(a, b, (((1,), (1,)), ((), ())), preferred_element_type=F32)


def _dot_tn(a, b):
    return lax.dot_general(a, b, (((0,), (0,)), ((), ())), preferred_element_type=F32)


def _dot_exact_lhs(m, x):
    hi, mid, lo = _split3(x)
    return _dot(m, hi) + _dot(m, mid) + _dot(m, lo)


def _dot_exact_rhs(x, m):
    hi, mid, lo = _split3(x)
    return _dot(hi, m) + _dot(mid, m) + _dot(lo, m)


def _sigmoid(x):
    return 1.0 / (1.0 + jnp.exp(-x))


def _layer_norm(x, g, b):
    mu = jnp.mean(x, axis=-1, keepdims=True)
    xc = x - mu
    var = jnp.mean(xc * xc, axis=-1, keepdims=True)
    return xc * lax.rsqrt(var + LN_EPS) * g + b


def _row_tile(n, want):
    t = min(n, want)
    assert n % t == 0, (n, t)
    return t


def _fox_head_lanes(h):
    lane = lax.broadcasted_iota(I32, (1, LANES), 1)
    hh = h % HEADS_PER_VREG
    own = (lane >= hh * HEAD_DIM) & (lane < (hh + 1) * HEAD_DIM)
    e0 = (1 - hh) * HEAD_DIM
    return lane, own, e0


def _combined_rows(y2_ref, route_ref, x1r_ref, g_ref, b_ref, *, tc, nj, alpha):
    gate1 = route_ref[:, 2:3]
    gate2 = route_ref[:, 3:4]
    resid = jnp.concatenate(
        [alpha * x1r_ref[pl.ds(j, tc, stride=nj), :]
         + gate1 * y2_ref[pl.ds(j, tc, stride=TOP_K * nj), :]
         + gate2 * y2_ref[pl.ds(nj + j, tc, stride=TOP_K * nj), :]
         for j in range(nj)], axis=-1)
    return _layer_norm(resid, g_ref[...], b_ref[...])


def _inproj_kernel(*refs, d_model, tiles_per_seq, cb, alpha, first_layer):
    if first_layer:
        x_ref, g_ref, bl_ref, *refs = refs
        x = _layer_norm(x_ref[...], g_ref[...], bl_ref[...])
        tm = x_ref.shape[0]
    else:
        y2_ref, route_ref, x1r_ref, g_ref, bl_ref, *refs = refs
        tm = route_ref.shape[0]
        x = _combined_rows(y2_ref, route_ref, x1r_ref, g_ref, bl_ref, tc=tm, nj=d_model // LANES, alpha=alpha)
    (w_ref, b_ref, tri_ref, xc_ref, qp_ref, kp_ref, vt_ref, hq_ref, hi_ref, hg_ref,
     gates_ref, hf_ref, fc_ref) = refs
    i = pl.program_id(0)
    xc_ref[...] = x
    xb = x.astype(BF16)

    def proj(c0, n):
        return _dot(xb, w_ref[:, c0:c0 + n]) + b_ref[:, c0:c0 + n]

    c_hf = _C_GATES + 2 * d_model

    @pl.when(i % tiles_per_seq == 0)
    def _():
        fc_ref[...] = jnp.zeros(fc_ref.shape, F32)

    ff = proj(c_hf + HGRN_WIDTH, LANES)
    log_f = jnp.minimum(ff, 0.0) - jnp.log(1.0 + jnp.exp(-jnp.abs(ff)))
    f_cum = _dot_exact_lhs(tri_ref[...], log_f * LOG2E) + fc_ref[...]
    fc_ref[...] = f_cum[tm - 1:tm, :]

    q_all = proj(_C_FQ, FOX_WIDTH) * (HEAD_DIM ** -0.5 * LOG2E)
    k_all = proj(_C_FK, FOX_WIDTH)
    v_all = proj(_C_FV, FOX_WIDTH)
    for h in range(FOX_HEADS):
        p = h // HEADS_PER_VREG
        cols = slice(p * LANES, (p + 1) * LANES)
        lane, own, e0 = _fox_head_lanes(h)
        hi, mid, lo = (t.astype(F32) for t in _split3(f_cum[:, h:h + 1]))
        ext_q = jnp.where((lane >= e0 + 3) & (lane < e0 + 6), 1.0, 0.0)
        ext_q = jnp.where(lane == e0, hi, ext_q)
        ext_q = jnp.where(lane == e0 + 1, mid, ext_q)
        ext_q = jnp.where(lane == e0 + 2, lo, ext_q)
        ext_k = jnp.where((lane >= e0) & (lane < e0 + 3), 1.0, 0.0)
        ext_k = jnp.where(lane == e0 + 3, -hi, ext_k)
        ext_k = jnp.where(lane == e0 + 4, -mid, ext_k)
        ext_k = jnp.where(lane == e0 + 5, -lo, ext_k)
        qp_ref[h] = jnp.where(own, q_all[:, cols], ext_q).astype(BF16)
        kp = jnp.where(own, k_all[:, cols], ext_k).astype(BF16)
        vt = jnp.where(own, v_all[:, cols], 0.0).T
        for blk in range(tm // cb):
            kp_ref[h, blk] = kp[blk * cb:(blk + 1) * cb]
            vt_ref[h, blk] = vt[:, blk * cb:(blk + 1) * cb].astype(BF16)

    hq_ref[...] = proj(_C_HQ, HGRN_WIDTH).astype(BF16)
    hi_ref[...] = proj(_C_HI, HGRN_WIDTH).astype(BF16)
    hg_ref[...] = proj(_C_HG, HGRN_WIDTH).astype(BF16)
    n_gate_chunks = 2 * d_model // 512
    for c in range(n_gate_chunks):
        gates_ref[:, c * 512:(c + 1) * 512] = proj(_C_GATES + c * 512, 512).astype(BF16)
    hf_ref[...] = proj(c_hf, HGRN_WIDTH)


def _inproj(source, w, b, tri, layer, seq, cb, alpha):
    first_layer = len(source) == 3
    n_all = w.shape[-1]
    tm = tri.shape[0]
    H = FOX_HEADS
    row = lambda n: pl.BlockSpec((tm, n), lambda i: (i, 0))
    if first_layer:
        T, D = source[0].shape
        src_specs = [row(D), pl.BlockSpec((1, D), lambda i: (0, 0)), pl.BlockSpec((1, D), lambda i: (0, 0))]
    else:
        _, route, x1r, _, _ = source
        T = route.shape[0]
        nj = x1r.shape[0] // T
        D = nj * LANES
        prev = layer - 1
        src_specs = [pl.BlockSpec((tm * TOP_K * nj, LANES), lambda i: (i, 0)), row(ROUTE_COLS),
                     pl.BlockSpec((tm * nj, LANES), lambda i: (i, 0)),
                     pl.BlockSpec((None, 1, D), lambda i: (prev, 0, 0)),
                     pl.BlockSpec((None, 1, D), lambda i: (prev, 0, 0))]
    outs = [(D, F32)] + [(HGRN_WIDTH, BF16)] * 3 + [(2 * D, BF16), (HGRN_WIDTH, F32)]
    attn = [((H, T, LANES), (H, tm, LANES), lambda i: (0, i, 0)),
            ((H, T // cb, cb, LANES), (H, tm // cb, cb, LANES), lambda i: (0, i, 0, 0)),
            ((H, T // cb, LANES, cb), (H, tm // cb, LANES, cb), lambda i: (0, i, 0, 0))]
    res = pl.pallas_call(
        functools.partial(_inproj_kernel, d_model=D, tiles_per_seq=seq // tm, cb=cb, alpha=alpha,
                          first_layer=first_layer),
        grid=(T // tm,),
        in_specs=src_specs
                 + [pl.BlockSpec((None, D, n_all), lambda i: (layer, 0, 0), pipeline_mode=pl.Buffered(1)),
                    pl.BlockSpec((None, 1, n_all), lambda i: (layer, 0, 0), pipeline_mode=pl.Buffered(1)),
                    pl.BlockSpec((tm, tm), lambda i: (0, 0), pipeline_mode=pl.Buffered(1))],
        out_specs=[row(D)] + [pl.BlockSpec(blk, imap) for _, blk, imap in attn] + [row(n) for n, _ in outs[1:]],
        out_shape=[jax.ShapeDtypeStruct((T, D), F32)]
                  + [jax.ShapeDtypeStruct(shape, BF16) for shape, _, _ in attn]
                  + [jax.ShapeDtypeStruct((T, n), dt) for n, dt in outs[1:]],
        scratch_shapes=[pltpu.VMEM((1, LANES), F32)],
        compiler_params=_params(("arbitrary",)),
        name="inproj",
    )(*source, w, b, tri)
    return res


def _fox_kernel(qp_ref, kp_ref, vt_ref, o_ref, m_ref, l_ref, acc_ref, s_ref, p_ref, a_ref, *, tq, cb):
    i = pl.program_id(1)
    key_i = lax.broadcasted_iota(I32, (tq, tq), 0)
    qry_i = lax.broadcasted_iota(I32, (tq, tq), 1)
    causal = key_i <= qry_i
    for h in range(FOX_HEADS):
        m_ref[h] = jnp.full((1, tq), MASK_VALUE, F32)
        l_ref[h] = jnp.zeros((1, tq), F32)
        acc_ref[h] = jnp.zeros((LANES, tq), F32)

    def scores(j, par):
        for h in range(FOX_HEADS):
            s_ref[par * FOX_HEADS + h] = _dot_nt(kp_ref[h, j], qp_ref[h])

    def softmax(par, masked):
        for h in range(FOX_HEADS):
            st = s_ref[par * FOX_HEADS + h]
            if masked:
                st = jnp.where(causal, st, MASK_VALUE)
            m = m_ref[h]
            m_new = jnp.maximum(m, jnp.max(st, axis=0, keepdims=True))
            a = jnp.exp2(m - m_new)
            pt = jnp.exp2(st - m_new)
            m_ref[h] = m_new
            l_ref[h] = a * l_ref[h] + jnp.sum(pt, axis=0, keepdims=True)
            p_ref[par * FOX_HEADS + h] = pt.astype(BF16)
            a_ref[par * FOX_HEADS + h] = a

    def values(j, par):
        for h in range(FOX_HEADS):
            acc_ref[h] = a_ref[par * FOX_HEADS + h] * acc_ref[h] + _dot(vt_ref[h, j],
                                                                     p_ref[par * FOX_HEADS + h])

    for h in range(FOX_HEADS):
        p_ref[FOX_HEADS + h] = jnp.zeros((cb, tq), BF16)
        a_ref[FOX_HEADS + h] = jnp.ones((1, tq), F32)
    scores(0, 0)

    def trip(t, par):
        scores(t + 1, 1 - par)
        softmax(par, False)
        values(jnp.maximum(t - 1, 0), 1 - par)

    def two_trips(u, _):
        trip(2 * u, 0)
        trip(2 * u + 1, 1)
        return 0

    lax.fori_loop(0, i // 2, two_trips, 0)

    @pl.when(i % 2 == 1)
    def _():
        trip(i - 1, 0)
        softmax(1, True)
        values(i - 1, 0)
        values(i, 1)

    @pl.when(i % 2 == 0)
    def _():
        softmax(0, True)
        values(jnp.maximum(i - 1, 0), 1)
        values(i, 0)
    for p in range(FOX_PAIRS):
        h0 = p * HEADS_PER_VREG
        out_t = acc_ref[h0] / l_ref[h0]
        for h in range(h0 + 1, h0 + HEADS_PER_VREG):
            out_t = out_t + acc_ref[h] / l_ref[h]
        o_ref[:, p * LANES:(p + 1) * LANES] = out_t.T.astype(BF16)


def _fox_attention(qp, kp, vt, batch, seq, cb):
    H, T, _ = qp.shape
    tq = cb
    nq = seq // tq
    return pl.pallas_call(
        functools.partial(_fox_kernel, tq=tq, cb=cb),
        grid=(batch, nq),
        in_specs=[pl.BlockSpec((H, tq, LANES), lambda b, i: (0, b * nq + i, 0)),
                  pl.BlockSpec((H, nq, cb, LANES), lambda b, i: (0, b, 0, 0)),
                  pl.BlockSpec((H, nq, LANES, cb), lambda b, i: (0, b, 0, 0))],
        out_specs=pl.BlockSpec((tq, FOX_WIDTH), lambda b, i: (b * nq + i, 0)),
        out_shape=jax.ShapeDtypeStruct((T, FOX_WIDTH), BF16),
        scratch_shapes=[pltpu.VMEM((FOX_HEADS, 1, tq), F32),
                        pltpu.VMEM((FOX_HEADS, 1, tq), F32),
                        pltpu.VMEM((FOX_HEADS, LANES, tq), F32),
                        pltpu.VMEM((2 * FOX_HEADS, cb, tq), F32),
                        pltpu.VMEM((2 * FOX_HEADS, cb, tq), BF16),
                        pltpu.VMEM((2 * FOX_HEADS, 1, tq), F32)],
        compiler_params=_params(("parallel", "arbitrary")),
        name="fox_attention",
    )(qp, kp, vt)


def _hgrn_kernel(hq_ref, hf_ref, hi_ref, hg_ref, lb_ref, ng_ref, o_ref, ss_ref, cpad_ref, vpad_ref,
                 term_ref, score_ref, *, seq, rb, wl):
    ch = HGRN_CHUNK
    n_states = wl // LANES
    rr = lax.broadcasted_iota(I32, (rb, rb), 0)
    cc = lax.broadcasted_iota(I32, (rb, rb), 1)
    same_chunk = (rr // ch) == (cc // ch)
    cum_mat = jnp.concatenate([(same_chunk & (cc <= rr)).astype(BF16), same_chunk.astype(BF16)], axis=0)
    hr = lax.broadcasted_iota(I32, (wl, wl), 0) // HEAD_DIM
    hc = lax.broadcasted_iota(I32, (wl, wl), 1) // HEAD_DIM
    head_ones = (hr == hc).astype(BF16)
    head_mask = (hr == hc)[:LANES, :LANES].astype(F32)
    tmod = lax.broadcasted_iota(I32, (rb, wl), 0) % ch
    lb = lb_ref[...]
    ng = ng_ref[...]
    ss_ref[...] = jnp.zeros(ss_ref.shape, F32)
    cpad_ref[0:ch, :] = jnp.zeros((ch, wl), F32)
    vpad_ref[0:ch, :] = jnp.zeros((ch, wl), F32)

    def block(r, _):
        rows = pl.ds(pl.multiple_of(r * rb, rb), rb)
        z = hf_ref[rows, :]
        g = jnp.log(lb + (1.0 - lb) * _sigmoid(z))
        k = (1.0 - lb) * _sigmoid(-z)
        hq = hq_ref[rows, :].astype(F32)
        qs = hq * _sigmoid(hq)
        v = hi_ref[rows, :].astype(F32)
        cums = _dot_exact_lhs(cum_mat, g * LOG2E)
        b = cums[:rb]
        btot = cums[rb:]
        qd = (qs * jnp.exp2(b)).astype(BF16)
        kd = (k * jnp.exp2(btot - b)).astype(BF16)
        dec = jnp.exp2(btot)

        c = b - jnp.log2(k)
        cpad_ref[ch:ch + rb, :] = c
        vpad_ref[ch:ch + rb, :] = v
        term_ref[0:rb, :] = (qs * k).astype(BF16)
        for o in range(1, ch):
            cs = cpad_ref[ch - o:ch - o + rb, :]
            term = jnp.where(tmod >= o, qs * jnp.exp2(b - cs), 0.0)
            term_ref[o * rb:(o + 1) * rb, :] = term.astype(BF16)
        score_ref[...] = _dot(term_ref[...], head_ones)
        acc = score_ref[0:rb, :] * v
        for o in range(1, ch):
            acc = acc + score_ref[o * rb:(o + 1) * rb, :] * vpad_ref[ch - o:ch - o + rb, :]

        vb = v.astype(BF16)
        chunks = [slice(n * ch, (n + 1) * ch) for n in range(rb // ch)]
        inter_cols = []
        for sidx in range(n_states):
            ln = slice(sidx * LANES, (sidx + 1) * LANES)
            upds = [_dot_tn(vb[sl, ln], kd[sl, ln]) * head_mask for sl in chunks]
            ss = ss_ref[sidx]
            states = []
            for n, sl in enumerate(chunks):
                states.append(ss.astype(BF16))
                ss = ss * dec[n * ch:n * ch + 1, ln] + upds[n]
            ss_ref[sidx] = ss
            inter_cols.append(jnp.concatenate(
                [_dot_nt(qd[sl, ln], st) for sl, st in zip(chunks, states)], axis=0))
        o_blk = acc + jnp.concatenate(inter_cols, axis=1)
        ms = _dot_exact_rhs(o_blk * o_blk, head_ones) * (1.0 / HEAD_DIM)
        hg = hg_ref[rows, :].astype(F32)
        o_blk = o_blk * lax.rsqrt(ms + RMS_EPS) * ng * (hg * _sigmoid(hg))
        o_ref[rows, :] = o_blk.astype(BF16)
        return 0

    lax.fori_loop(0, seq // rb, block, 0, unroll=2 if (seq // rb) % 2 == 0 else 1)


def _hgrn(hq, hf, hi, hg, lb, ng, layer, batch, seq):
    T = hq.shape[0]
    rb = _row_tile(seq, 128)
    wl = 2 * LANES
    n_prog = HGRN_WIDTH // wl
    blk = lambda: pl.BlockSpec((seq, wl), lambda b, p: (b, p))
    par = lambda: pl.BlockSpec((None, 1, wl), lambda b, p: (layer * n_prog + p, 0, 0))
    return pl.pallas_call(
        functools.partial(_hgrn_kernel, seq=seq, rb=rb, wl=wl),
        grid=(batch, n_prog),
        in_specs=[blk(), blk(), blk(), blk(), par(), par()],
        out_specs=blk(),
        out_shape=jax.ShapeDtypeStruct((T, HGRN_WIDTH), BF16),
        scratch_shapes=[pltpu.VMEM((wl // LANES, LANES, LANES), F32),
                        pltpu.VMEM((HGRN_CHUNK + rb, wl), F32),
                        pltpu.VMEM((HGRN_CHUNK + rb, wl), F32),
                        pltpu.VMEM((HGRN_CHUNK * rb, wl), BF16),
                        pltpu.VMEM((HGRN_CHUNK * rb, wl), F32)],
        compiler_params=_params(("parallel", "parallel")),
        name="hgrn2",
    )(hq, hf, hi, hg, lb.reshape(-1, 1, wl), ng.reshape(-1, 1, wl))


def _merge_kernel(fo_ref, ho_ref, gates_ref, x_ref, wf_ref, wh_ref, wm_ref, bm_ref, g_ref, b_ref,
                  wrh_ref, wrl_ref, rb_ref, tri_ref, x1r_ref, route_ref, cnt_ref,
                  *, alpha, d_model):
    i = pl.program_id(0)
    tm = x_ref.shape[0]

    @pl.when(i == 0)
    def _():
        cnt_ref[...] = jnp.zeros(cnt_ref.shape, F32)

    y_fox = _dot(fo_ref[...], wf_ref[...])
    y_hgrn = _dot(ho_ref[...], wh_ref[...])
    g_fox = _sigmoid(gates_ref[:, :d_model].astype(F32))
    g_hgrn = _sigmoid(gates_ref[:, d_model:].astype(F32))
    mixed = _dot((g_fox * y_fox + g_hgrn * y_hgrn).astype(BF16), wm_ref[...]) + bm_ref[...]
    x1 = _layer_norm(alpha * x_ref[...] + mixed, g_ref[...], b_ref[...])
    nj = d_model // LANES
    for j in range(nj):
        x1r_ref[pl.ds(j, tm, stride=nj), :] = x1[:, j * LANES:(j + 1) * LANES]

    xh = x1.astype(BF16)
    xl = (x1 - xh.astype(F32)).astype(BF16)
    logits = _dot(xh, wrh_ref[...]) + _dot(xl, wrh_ref[...]) + _dot(xh, wrl_ref[...]) + rb_ref[...]
    lane = lax.broadcasted_iota(I32, (tm, LANES), 1)
    lane_f = lane.astype(F32)
    neg = jnp.float32(-jnp.inf)
    lg = jnp.where(lane < N_EXPERTS, logits, neg)
    m1 = jnp.max(lg, axis=-1, keepdims=True)
    idx1 = jnp.min(jnp.where(lg == m1, lane_f, float(LANES)), axis=-1, keepdims=True).astype(I32)
    in_group = (lane // EXPERTS_PER_GROUP == idx1 // EXPERTS_PER_GROUP) & (lane < N_EXPERTS)
    lg2 = jnp.where(in_group & (lane != idx1), logits, neg)
    m2 = jnp.max(lg2, axis=-1, keepdims=True)
    idx2 = jnp.min(jnp.where(lg2 == m2, lane_f, float(LANES)), axis=-1, keepdims=True).astype(I32)
    e21 = jnp.exp(m2 - m1)
    gate1 = 1.0 / (1.0 + e21)
    gate2 = e21 / (1.0 + e21)

    oh1 = lane == idx1
    oh2 = lane == idx2
    oh = (oh1 | oh2).astype(F32)
    before = _dot(tri_ref[...], oh.astype(BF16)) + cnt_ref[0:1, :]
    rank1 = jnp.sum(jnp.where(oh1, before, 0.0), axis=-1, keepdims=True)
    rank2 = jnp.sum(jnp.where(oh2, before, 0.0), axis=-1, keepdims=True)
    cnt_ref[0:1, :] = cnt_ref[0:1, :] + jnp.sum(oh, axis=0, keepdims=True)

    route = jnp.where(lane == 0, idx1.astype(F32), 0.0)
    route = jnp.where(lane == 1, idx2.astype(F32), route)
    route = jnp.where(lane == 2, gate1, route)
    route = jnp.where(lane == 3, gate2, route)
    route = jnp.where(lane == 4, rank1, route)
    route = jnp.where(lane == 5, rank2, route)
    route_ref[...] = route[:, :ROUTE_COLS]


def _merge(fox_o, hgrn_o, gates, x, wf, wh, wm, bm, g, b, wrh, wrl, rbias, tri, layer, alpha):
    T, D = x.shape
    tm = tri.shape[0]
    nj = D // LANES
    row = lambda n: pl.BlockSpec((tm, n), lambda i: (i, 0))
    lw = lambda r, c: pl.BlockSpec((None, r, c), lambda i: (layer, 0, 0))
    cw = lambda r, c: pl.BlockSpec((r, c), lambda i: (0, 0))
    return pl.pallas_call(
        functools.partial(_merge_kernel, alpha=alpha, d_model=D),
        grid=(T // tm,),
        in_specs=[row(FOX_WIDTH), row(HGRN_WIDTH), row(2 * D), row(D),
                  lw(FOX_WIDTH, D), lw(HGRN_WIDTH, D), lw(D, D), lw(1, D), lw(1, D), lw(1, D),
                  cw(D, LANES), cw(D, LANES), cw(1, LANES), cw(tm, tm)],
        out_specs=[pl.BlockSpec((tm * nj, LANES), lambda i: (i, 0)), row(ROUTE_COLS),
                   pl.BlockSpec((SUBLANES, LANES), lambda i: (0, 0))],
        out_shape=[jax.ShapeDtypeStruct((T * nj, LANES), F32),
                   jax.ShapeDtypeStruct((T, ROUTE_COLS), F32), jax.ShapeDtypeStruct((SUBLANES, LANES), F32)],
        compiler_params=_params(("arbitrary",)),
        name="merge_router",
    )(fox_o, hgrn_o, gates, x, wf, wh, wm, bm, g, b, wrh, wrl, rbias, tri)


def _dispatch_kernel(dest_ref, x_ref, xs_hbm, inv_ref, sem, *, tc, nj):
    i = pl.program_id(0)

    def row_copy(r, d):
        return pltpu.make_async_copy(
            x_ref.at[pl.ds(pl.multiple_of(r * nj, nj), nj), :],
            xs_hbm.at[pl.ds(pl.multiple_of(d * nj, nj), nj), :], sem)

    def issue(r, _):
        d0 = dest_ref[0, 0, 2 * r]
        d1 = dest_ref[0, 0, 2 * r + 1]
        assignment = 2 * (i * tc + r)
        inv_ref[d0] = assignment
        inv_ref[d1] = assignment + 1
        row_copy(r, d0).start(priority=0)
        row_copy(r, d1).start(priority=1)
        return 0

    lax.fori_loop(0, tc, issue, 0, unroll=8)

    def drain(r, _):
        row_copy(0, 0).wait()
        row_copy(0, 0).wait()
        return 0

    lax.fori_loop(0, tc, drain, 0, unroll=8)


def _dispatch(x1r, dest, n_tokens, nj):
    tc = _row_tile(n_tokens, 512)
    n_steps = n_tokens // tc
    return pl.pallas_call(
        functools.partial(_dispatch_kernel, tc=tc, nj=nj),
        grid=(n_steps,),
        in_specs=[pl.BlockSpec((1, 1, 2 * tc), lambda i: (i, 0, 0), memory_space=pltpu.SMEM),
                  pl.BlockSpec((tc * nj, LANES), lambda i: (i, 0))],
        out_specs=[pl.BlockSpec(memory_space=pl.ANY), pl.BlockSpec(memory_space=pltpu.SMEM)],
        out_shape=[jax.ShapeDtypeStruct((2 * n_tokens * nj, LANES), F32),
                   jax.ShapeDtypeStruct((2 * n_tokens,), I32)],
        scratch_shapes=[pltpu.SemaphoreType.DMA(())],
        compiler_params=pltpu.CompilerParams(dimension_semantics=("arbitrary",), has_side_effects=True),
        name="moe_dispatch",
    )(dest.reshape(n_steps, 1, 2 * tc), x1r)


def _experts_kernel(meta_ref, slot_ref, xs_ref, w1_ref, w3_ref, w2_ref, y2_hbm, ybuf_ref, sem,
                    *, tm, nj, n_assign):
    w = pl.program_id(0)
    lo = meta_ref[2, w]
    hi = meta_ref[3, w]
    prev = jnp.maximum(w - 1, 0)
    prev_real = (w >= 1) & (meta_ref[3, prev] > meta_ref[2, prev])
    cur_par = w % 2
    prev_par = 1 - cur_par

    def row_copy(par, r, slot):
        return pltpu.make_async_copy(
            ybuf_ref.at[par, pl.ds(pl.multiple_of(r * nj, nj), nj), :],
            y2_hbm.at[pl.ds(pl.multiple_of(slot * nj, nj), nj), :], sem.at[par])

    def send_previous_rows():
        for r in range(tm):
            row_copy(prev_par, r, slot_ref[0, 0, r]).start(priority=r % 2)

    def wait_rows(par):
        pltpu.make_async_copy(ybuf_ref.at[par], y2_hbm.at[pl.ds(0, tm * nj), :], sem.at[par]).wait()

    @pl.when(w == 0)
    def _():
        ybuf_ref[...] = jnp.zeros(ybuf_ref.shape, F32)
        spare0 = pltpu.make_async_copy(ybuf_ref.at[0], y2_hbm.at[pl.ds(n_assign * nj, tm * nj), :], sem.at[0])
        spare0.start()
        spare0.wait()

    @pl.when(hi > lo)
    def _():
        send_previous_rows()
        x = jnp.concatenate([xs_ref[pl.ds(j, tm, stride=nj), :] for j in range(nj)], axis=-1).astype(BF16)
        h1 = _dot(x, w1_ref[...].astype(BF16))
        h3 = _dot(x, w3_ref[...].astype(BF16))
        h = (h1 * _sigmoid(h1) * h3).astype(BF16)
        y = _dot(h, w2_ref[...].astype(BF16))

        @pl.when(w >= 1)
        def _():
            wait_rows(cur_par)

        for j in range(nj):
            ybuf_ref[cur_par, pl.ds(j, tm, stride=nj), :] = y[:, j * LANES:(j + 1) * LANES]

    @pl.when((hi == lo) & prev_real)
    def _():
        send_previous_rows()
        wait_rows(0)
        wait_rows(1)


def _experts(meta, inv, xs, w1, w3, w2, layer, tm, nj):
    n_items = meta.shape[1]
    D = nj * LANES
    dh = w1.shape[-1]
    n_assign = xs.shape[0] // nj
    tile, lo, hi = meta[0], meta[2], meta[3]
    r = jnp.arange(tm, dtype=I32)[None, :]
    owned = (r >= lo[:, None]) & (r < hi[:, None])
    spare = n_assign + (jnp.arange(n_items, dtype=I32)[:, None] % 2) * tm + r
    slots = jnp.where(owned, inv[tile[:, None] * tm + r], spare)
    slots = jnp.concatenate([n_assign + tm + r, slots[:-1]], axis=0).reshape(n_items, 1, tm)
    grid_spec = pltpu.PrefetchScalarGridSpec(
        num_scalar_prefetch=1,
        grid=(n_items,),
        in_specs=[pl.BlockSpec((1, 1, tm), lambda w, m: (w, 0, 0), memory_space=pltpu.SMEM),
                  pl.BlockSpec((tm * nj, LANES), lambda w, m: (m[0, w], 0)),
                  pl.BlockSpec((None, None, D, dh), lambda w, m: (layer, m[1, w], 0, 0)),
                  pl.BlockSpec((None, None, D, dh), lambda w, m: (layer, m[1, w], 0, 0)),
                  pl.BlockSpec((None, None, dh, D), lambda w, m: (layer, m[1, w], 0, 0))],
        out_specs=pl.BlockSpec(memory_space=pl.ANY),
        scratch_shapes=[pltpu.VMEM((2, tm * nj, LANES), F32), pltpu.SemaphoreType.DMA((2,))],
    )
    return pl.pallas_call(
        functools.partial(_experts_kernel, tm=tm, nj=nj, n_assign=n_assign),
        grid_spec=grid_spec,
        out_shape=jax.ShapeDtypeStruct(((n_assign + 2 * tm) * nj, LANES), F32),
        compiler_params=pltpu.CompilerParams(dimension_semantics=("arbitrary",), vmem_limit_bytes=VMEM_LIMIT,
                                             has_side_effects=True),
        name="moe_experts",
    )(meta, slots, xs, w1, w3, w2)


def _combine_kernel(y2_ref, route_ref, x1r_ref, g_ref, b_ref, o_ref, *, tc, nj, alpha):
    o_ref[...] = _combined_rows(y2_ref, route_ref, x1r_ref, g_ref, b_ref, tc=tc, nj=nj, alpha=alpha)


def _combine(y2, route, x1r, g, b, layer, alpha):
    T = route.shape[0]
    nj = x1r.shape[0] // T
    D = nj * LANES
    tc = _row_tile(T, 512)
    row = lambda n: pl.BlockSpec((tc, n), lambda i: (i, 0))
    lw = lambda r, c: pl.BlockSpec((None, r, c), lambda i: (layer, 0, 0))
    return pl.pallas_call(
        functools.partial(_combine_kernel, tc=tc, nj=nj, alpha=alpha),
        grid=(T // tc,),
        in_specs=[pl.BlockSpec((tc * TOP_K * nj, LANES), lambda i: (i, 0)), row(ROUTE_COLS),
                  pl.BlockSpec((tc * nj, LANES), lambda i: (i, 0)), lw(1, D), lw(1, D)],
        out_specs=row(D),
        out_shape=jax.ShapeDtypeStruct((T, D), F32),
        compiler_params=_params(("parallel",)),
        name="moe_combine",
    )(y2, route, x1r, g, b)


def _routing_tables(route, counts, tm, n_items):
    e = route[:, 0:2].astype(I32)
    rank = route[:, 4:6].astype(I32)
    cnt = counts[0, :N_EXPERTS].astype(I32)
    ends = jnp.cumsum(cnt)
    starts = ends - cnt
    dest = (starts[e] + rank).reshape(-1)

    first_tile = starts // tm
    last_tile = jnp.maximum(ends - 1, 0) // tm
    n_tiles_e = jnp.where(cnt > 0, last_tile - first_tile + 1, 0)
    item_end = jnp.cumsum(n_tiles_e)
    item_start = item_end - n_tiles_e
    n_real = item_end[-1]
    w = jnp.arange(n_items, dtype=I32)
    wc = jnp.minimum(w, n_real - 1)
    ex = jnp.sum((item_end[None, :] <= wc[:, None]).astype(I32), axis=1)
    tile = first_tile[ex] + (wc - item_start[ex])
    lo = jnp.clip(starts[ex] - tile * tm, 0, tm)
    hi = jnp.clip(ends[ex] - tile * tm, 0, tm)
    real = w < n_real
    hi = jnp.where(real, hi, lo)
    meta = jnp.stack([tile, ex, lo, hi]).astype(I32)
    return dest, meta


def kernel(x, ln_in_g, ln_in_b, w_in, b_in, w_fox_branch, hgrn_lb_logits, hgrn_norm_g, w_hgrn_branch,
           w_mix_out, b_mix_out, ln1_g, ln1_b, router_w, router_b, expert_w1, expert_w3, expert_w2,
           ln2_g, ln2_b):
    batch, seq, D = x.shape
    depth = w_in.shape[0]
    T = batch * seq
    nj = D // LANES
    alpha = float((2 * depth) ** 0.25)
    assert D % 512 == 0 and seq % HGRN_CHUNK == 0

    sizes = (FOX_WIDTH, FOX_WIDTH, FOX_WIDTH, FOX_HEADS, HGRN_WIDTH, HGRN_WIDTH, HGRN_WIDTH, HGRN_WIDTH, 2 * D)
    offs = [0]
    for s in sizes:
        offs.append(offs[-1] + s)
    col = lambda a, i: a[..., offs[i]:offs[i + 1]]
    order = (0, 1, 2, 4, 6, 7, 8, 5)
    pad_ff = lambda a: jnp.pad(col(a, 3), [(0, 0)] * (a.ndim - 1) + [(0, LANES - FOX_HEADS)])
    w_all = jnp.concatenate([col(w_in, i) for i in order] + [pad_ff(w_in)], axis=-1).astype(BF16)
    b_all = jnp.concatenate([col(b_in, i) for i in order] + [pad_ff(b_in)], axis=-1).astype(F32)[:, None, :]

    lb_p = jax.nn.softmax(hgrn_lb_logits.astype(F32), axis=0)
    lb_all = (jnp.cumsum(lb_p, axis=0) - lb_p[0]).reshape(depth * HGRN_PAIRS, 1, LANES)
    ng_all = hgrn_norm_g.astype(F32).reshape(depth * HGRN_PAIRS, 1, LANES)

    wf = w_fox_branch.astype(BF16)
    wh = w_hgrn_branch.astype(BF16)
    wm = w_mix_out.astype(BF16)
    r3 = lambda a: a.astype(F32)[:, None, :]
    bm, g1, b1, g2, b2 = r3(b_mix_out), r3(ln1_g), r3(ln1_b), r3(ln2_g), r3(ln2_b)
    rw = jnp.pad(router_w.astype(F32), ((0, 0), (0, LANES - N_EXPERTS)))
    wrh = rw.astype(BF16)
    wrl = (rw - wrh.astype(F32)).astype(BF16)
    rbias = jnp.pad(router_b.astype(F32), (0, LANES - N_EXPERTS)).reshape(1, LANES)

    tm_merge = _row_tile(T, 512)
    tri = jnp.tril(jnp.ones((tm_merge, tm_merge), BF16), k=-1)
    tm_exp = _row_tile(2 * T, 512)
    n_items = (2 * T) // tm_exp + N_EXPERTS
    tm_in = _row_tile(seq, 512)
    fox_block = _row_tile(tm_in, 256)
    tri_in = jnp.tril(jnp.ones((tm_in, tm_in), BF16))

    source = (x.reshape(T, D), ln_in_g.astype(F32).reshape(1, D), ln_in_b.astype(F32).reshape(1, D))
    for l in range(depth):
        xc, qp, kp, vt, hq, hi, hg, gates, hf = _inproj(source, w_all, b_all, tri_in, l, seq, fox_block, alpha)
        fox_o = _fox_attention(qp, kp, vt, batch, seq, fox_block)
        hgrn_o = _hgrn(hq, hf, hi, hg, lb_all, ng_all, l, batch, seq)
        x1r, route, counts = _merge(fox_o, hgrn_o, gates, xc, wf, wh, wm, bm, g1, b1,
                                    wrh, wrl, rbias, tri, l, alpha)
        dest, meta = _routing_tables(route, counts, tm_exp, n_items)
        xs, inv = _dispatch(x1r, dest, T, nj)
        y2 = _experts(meta, inv, xs, expert_w1, expert_w3, expert_w2, l, tm_exp, nj)
        source = (y2, route, x1r, g2, b2)
    out = _combine(*source, depth - 1, alpha)
    return out.reshape(batch, seq, D)
```

```python
import functools

import jax
import jax.numpy as jnp
from jax import lax
from jax.experimental import pallas as pl
from jax.experimental.pallas import tpu as pltpu

F32 = jnp.float32
BF16 = jnp.bfloat16
I32 = jnp.int32

LANES = 128
SUBLANES = 8
HEAD_DIM = 64
FOX_HEADS = 8
HGRN_HEADS = 8
HEADS_PER_VREG = LANES // HEAD_DIM
FOX_PAIRS = FOX_HEADS // HEADS_PER_VREG
HGRN_PAIRS = HGRN_HEADS // HEADS_PER_VREG
FOX_WIDTH = FOX_HEADS * HEAD_DIM
HGRN_WIDTH = HGRN_HEADS * HEAD_DIM
N_EXPERTS = 16
N_GROUPS = 4
EXPERTS_PER_GROUP = N_EXPERTS // N_GROUPS
TOP_K = 2
ROUTE_COLS = 8
HGRN_CHUNK = 16
LN_EPS = 1e-5
RMS_EPS = 1e-6
MASK_VALUE = -1e30
LOG2E = 1.4426950408889634
VMEM_LIMIT = 56 * 1024 * 1024

_C_FQ = 0
_C_FK = _C_FQ + FOX_WIDTH
_C_FV = _C_FK + FOX_WIDTH
_C_HQ = _C_FV + FOX_WIDTH
_C_HI = _C_HQ + HGRN_WIDTH
_C_HG = _C_HI + HGRN_WIDTH
_C_GATES = _C_HG + HGRN_WIDTH


def _params(sem, vmem=VMEM_LIMIT):
    return pltpu.CompilerParams(dimension_semantics=sem, vmem_limit_bytes=vmem)


def _split3(x):
    hi = x.astype(BF16)
    r1 = x - hi.astype(F32)
    mid = r1.astype(BF16)
    lo = (r1 - mid.astype(F32)).astype(BF16)
    return hi, mid, lo


def _dot(a, b):
    return jnp.dot(a, b, preferred_element_type=F32)


def _dot_nt(a, b):
    return lax.dot_general(a, b, (((1,), (1,)), ((), ())), preferred_element_type=F32)


def _dot_tn(a, b):
    return lax.dot_general(a, b, (((0,), (0,)), ((), ())), preferred_element_type=F32)


def _dot_exact_lhs(m, x):
    hi, mid, lo = _split3(x)
    return _dot(m, hi) + _dot(m, mid) + _dot(m, lo)


def _dot_exact_rhs(x, m):
    hi, mid, lo = _split3(x)
    return _dot(hi, m) + _dot(mid, m) + _dot(lo, m)


def _sigmoid(x):
    return 1.0 / (1.0 + jnp.exp(-x))


def _layer_norm(x, g, b):
    mu = jnp.mean(x, axis=-1, keepdims=True)
    xc = x - mu
    var = jnp.mean(xc * xc, axis=-1, keepdims=True)
    return xc * lax.rsqrt(var + LN_EPS) * g + b


def _row_tile(n, want):
    t = min(n, want)
    assert n % t == 0, (n, t)
    return t


def _fox_head_lanes(h):
    lane = lax.broadcasted_iota(I32, (1, LANES), 1)
    hh = h % HEADS_PER_VREG
    own = (lane >= hh * HEAD_DIM) & (lane < (hh + 1) * HEAD_DIM)
    e0 = (1 - hh) * HEAD_DIM
    return lane, own, e0


def _combined_rows(y2_ref, route_ref, x1r_ref, g_ref, b_ref, *, tc, nj, alpha):
    gate1 = route_ref[:, 2:3]
    gate2 = route_ref[:, 3:4]
    resid = jnp.concatenate(
        [alpha * x1r_ref[pl.ds(j, tc, stride=nj), :]
         + gate1 * y2_ref[pl.ds(j, tc, stride=TOP_K * nj), :]
         + gate2 * y2_ref[pl.ds(nj + j, tc, stride=TOP_K * nj), :]
         for j in range(nj)], axis=-1)
    return _layer_norm(resid, g_ref[...], b_ref[...])


def _inproj_kernel(*refs, d_model, tiles_per_seq, cb, alpha, first_layer):
    i = pl.program_id(0)
    fc_ref, xin_ref = refs[-2], refs[-1]
    tile = jnp.maximum(i - 1, 0)

    @pl.when(i == 0)
    def _():
        xin_ref[...] = jnp.zeros(xin_ref.shape, F32)

    @pl.when(tile % tiles_per_seq == 0)
    def _():
        fc_ref[...] = jnp.zeros(fc_ref.shape, F32)

    if first_layer:
        x_ref, g_ref, bl_ref, *refs = refs
        tm = x_ref.shape[0]
    else:
        y2_ref, route_ref, x1r_ref, g_ref, bl_ref, *refs = refs
        tm = route_ref.shape[0]
    (w_ref, b_ref, tri_ref, xc_ref, qp_ref, kp_ref, vt_ref, hq_ref, hi_ref, hg_ref,
     gates_ref, hf_ref, _, _) = refs
    x = xin_ref[(i + 1) % 2]
    xc_ref[...] = x
    xb = x.astype(BF16)

    def proj(c0, n):
        return _dot(xb, w_ref[:, c0:c0 + n]) + b_ref[:, c0:c0 + n]

    c_hf = _C_GATES + 2 * d_model
    ff = proj(c_hf + HGRN_WIDTH, LANES)
    log_f = jnp.minimum(ff, 0.0) - jnp.log(1.0 + jnp.exp(-jnp.abs(ff)))
    f_cum = _dot_exact_lhs(tri_ref[...], log_f * LOG2E) + fc_ref[...]
    fc_ref[...] = f_cum[tm - 1:tm, :]

    q_all = proj(_C_FQ, FOX_WIDTH) * (HEAD_DIM ** -0.5 * LOG2E)
    k_all = proj(_C_FK, FOX_WIDTH)
    v_all = proj(_C_FV, FOX_WIDTH)
    for h in range(FOX_HEADS):
        p = h // HEADS_PER_VREG
        cols = slice(p * LANES, (p + 1) * LANES)
        lane, own, e0 = _fox_head_lanes(h)
        hi, mid, lo = (t.astype(F32) for t in _split3(f_cum[:, h:h + 1]))
        ext_q = jnp.where((lane >= e0 + 3) & (lane < e0 + 6), 1.0, 0.0)
        ext_q = jnp.where(lane == e0, hi, ext_q)
        ext_q = jnp.where(lane == e0 + 1, mid, ext_q)
        ext_q = jnp.where(lane == e0 + 2, lo, ext_q)
        ext_k = jnp.where((lane >= e0) & (lane < e0 + 3), 1.0, 0.0)
        ext_k = jnp.where(lane == e0 + 3, -hi, ext_k)
        ext_k = jnp.where(lane == e0 + 4, -mid, ext_k)
        ext_k = jnp.where(lane == e0 + 5, -lo, ext_k)
        qp_ref[h] = jnp.where(own, q_all[:, cols], ext_q).astype(BF16)
        kp = jnp.where(own, k_all[:, cols], ext_k).astype(BF16)
        vt = jnp.where(own, v_all[:, cols], 0.0).T
        for blk in range(tm // cb):
            kp_ref[h, blk] = kp[blk * cb:(blk + 1) * cb]
            vt_ref[h, blk] = vt[:, blk * cb:(blk + 1) * cb].astype(BF16)

    nj = d_model // LANES
    strip = 64

    def next_input_strip(r0):
        if first_layer:
            x_next = _layer_norm(x_ref[r0:r0 + strip, :], g_ref[...], bl_ref[...])
        else:
            x_next = _combined_rows(y2_ref.at[pl.ds(r0 * TOP_K * nj, strip * TOP_K * nj), :],
                                    route_ref.at[pl.ds(r0, strip), :],
                                    x1r_ref.at[pl.ds(r0 * nj, strip * nj), :],
                                    g_ref, bl_ref, tc=strip, nj=nj, alpha=alpha)
        xin_ref[i % 2, r0:r0 + strip, :] = x_next

    strips = iter(range(0, tm, strip))
    n_gate_chunks = 2 * d_model // 512
    projections = ([(hq_ref, _C_HQ, BF16), (hi_ref, _C_HI, BF16), (hg_ref, _C_HG, BF16)]
                   + [(gates_ref.at[:, pl.ds(c * 512, 512)], _C_GATES + c * 512, BF16) for c in range(n_gate_chunks)]
                   + [(hf_ref, c_hf, F32)])
    for out_ref, c0, dt in projections:
        out_ref[...] = proj(c0, out_ref.shape[1]).astype(dt)
        r0 = next(strips, None)
        if r0 is not None:
            next_input_strip(r0)
    for r0 in strips:
        next_input_strip(r0)


def _inproj(source, w, b, tri, layer, seq, cb, alpha):
    first_layer = len(source) == 3
    n_all = w.shape[-1]
    tm = tri.shape[0]
    H = FOX_HEADS
    T = source[0].shape[0] if first_layer else source[1].shape[0]
    n_tiles = T // tm
    src = lambda i: jnp.minimum(i, n_tiles - 1)
    dst = lambda i: jnp.maximum(i - 1, 0)
    src_row = lambda rows, cols: pl.BlockSpec((rows, cols), lambda i: (src(i), 0))
    out_row = lambda n: pl.BlockSpec((tm, n), lambda i: (dst(i), 0))
    if first_layer:
        D = source[0].shape[1]
        src_specs = [src_row(tm, D), pl.BlockSpec((1, D), lambda i: (0, 0)), pl.BlockSpec((1, D), lambda i: (0, 0))]
    else:
        nj = source[2].shape[0] // T
        D = nj * LANES
        prev = layer - 1
        src_specs = [src_row(tm * TOP_K * nj, LANES), src_row(tm, ROUTE_COLS), src_row(tm * nj, LANES),
                     pl.BlockSpec((None, 1, D), lambda i: (prev, 0, 0)),
                     pl.BlockSpec((None, 1, D), lambda i: (prev, 0, 0))]
    outs = [(D, F32)] + [(HGRN_WIDTH, BF16)] * 3 + [(2 * D, BF16), (HGRN_WIDTH, F32)]
    attn = [((H, T, LANES), (H, tm, LANES), lambda i: (0, dst(i), 0)),
            ((H, T // cb, cb, LANES), (H, tm // cb, cb, LANES), lambda i: (0, dst(i), 0, 0)),
            ((H, T // cb, LANES, cb), (H, tm // cb, LANES, cb), lambda i: (0, dst(i), 0, 0))]
    return pl.pallas_call(
        functools.partial(_inproj_kernel, d_model=D, tiles_per_seq=seq // tm, cb=cb, alpha=alpha,
                          first_layer=first_layer),
        grid=(n_tiles + 1,),
        in_specs=src_specs
                 + [pl.BlockSpec((None, D, n_all), lambda i: (layer, 0, 0), pipeline_mode=pl.Buffered(1)),
                    pl.BlockSpec((None, 1, n_all), lambda i: (layer, 0, 0), pipeline_mode=pl.Buffered(1)),
                    pl.BlockSpec((tm, tm), lambda i: (0, 0), pipeline_mode=pl.Buffered(1))],
        out_specs=[out_row(D)] + [pl.BlockSpec(blk, imap) for _, blk, imap in attn]
                  + [out_row(n) for n, _ in outs[1:]],
        out_shape=[jax.ShapeDtypeStruct((T, D), F32)]
                  + [jax.ShapeDtypeStruct(shape, BF16) for shape, _, _ in attn]
                  + [jax.ShapeDtypeStruct((T, n), dt) for n, dt in outs[1:]],
        scratch_shapes=[pltpu.VMEM((1, LANES), F32), pltpu.VMEM((2, tm, D), F32)],
        compiler_params=_params(("arbitrary",)),
        name="inproj",
    )(*source, w, b, tri)


def _fox_kernel(qp_ref, kp_ref, vt_ref, o_ref, m_ref, l_ref, acc_ref, s_ref, p_ref, a_ref, *, tq, cb):
    i = pl.program_id(1)
    key_i = lax.broadcasted_iota(I32, (tq, tq), 0)
    qry_i = lax.broadcasted_iota(I32, (tq, tq), 1)
    causal = key_i <= qry_i
    for h in range(FOX_HEADS):
        m_ref[h] = jnp.full((1, tq), MASK_VALUE, F32)
        l_ref[h] = jnp.zeros((1, tq), F32)
        acc_ref[h] = jnp.zeros((LANES, tq), F32)

    def scores(j, par):
        for h in range(FOX_HEADS):
            s_ref[par * FOX_HEADS + h] = _dot_nt(kp_ref[h, j], qp_ref[h])

    def softmax(par, masked):
        for h in range(FOX_HEADS):
            st = s_ref[par * FOX_HEADS + h]
            if masked:
                st = jnp.where(causal, st, MASK_VALUE)
            m = m_ref[h]
            m_new = jnp.maximum(m, jnp.max(st, axis=0, keepdims=True))
            a = jnp.exp2(m - m_new)
            pt = jnp.exp2(st - m_new)
            m_ref[h] = m_new
            l_ref[h] = a * l_ref[h] + jnp.sum(pt, axis=0, keepdims=True)
            p_ref[par * FOX_HEADS + h] = pt.astype(BF16)
            a_ref[par * FOX_HEADS + h] = a

    def values(j, par):
        for h in range(FOX_HEADS):
            acc_ref[h] = a_ref[par * FOX_HEADS + h] * acc_ref[h] + _dot(vt_ref[h, j],
                                                                     p_ref[par * FOX_HEADS + h])

    for h in range(FOX_HEADS):
        p_ref[FOX_HEADS + h] = jnp.zeros((cb, tq), BF16)
        a_ref[FOX_HEADS + h] = jnp.ones((1, tq), F32)
    scores(0, 0)

    def trip(t, par):
        scores(t + 1, 1 - par)
        softmax(par, False)
        values(jnp.maximum(t - 1, 0), 1 - par)

    def two_trips(u, _):
        trip(2 * u, 0)
        trip(2 * u + 1, 1)
        return 0

    lax.fori_loop(0, i // 2, two_trips, 0)

    @pl.when(i % 2 == 1)
    def _():
        trip(i - 1, 0)
        softmax(1, True)
        values(i - 1, 0)
        values(i, 1)

    @pl.when(i % 2 == 0)
    def _():
        softmax(0, True)
        values(jnp.maximum(i - 1, 0), 1)
        values(i, 0)
    for p in range(FOX_PAIRS):
        h0 = p * HEADS_PER_VREG
        out_t = acc_ref[h0] / l_ref[h0]
        for h in range(h0 + 1, h0 + HEADS_PER_VREG):
            out_t = out_t + acc_ref[h] / l_ref[h]
        o_ref[:, p * LANES:(p + 1) * LANES] = out_t.T.astype(BF16)


def _fox_attention(qp, kp, vt, batch, seq, cb):
    H, T, _ = qp.shape
    tq = cb
    nq = seq // tq
    return pl.pallas_call(
        functools.partial(_fox_kernel, tq=tq, cb=cb),
        grid=(batch, nq),
        in_specs=[pl.BlockSpec((H, tq, LANES), lambda b, i: (0, b * nq + i, 0)),
                  pl.BlockSpec((H, nq, cb, LANES), lambda b, i: (0, b, 0, 0)),
                  pl.BlockSpec((H, nq, LANES, cb), lambda b, i: (0, b, 0, 0))],
        out_specs=pl.BlockSpec((tq, FOX_WIDTH), lambda b, i: (b * nq + i, 0)),
        out_shape=jax.ShapeDtypeStruct((T, FOX_WIDTH), BF16),
        scratch_shapes=[pltpu.VMEM((FOX_HEADS, 1, tq), F32),
                        pltpu.VMEM((FOX_HEADS, 1, tq), F32),
                        pltpu.VMEM((FOX_HEADS, LANES, tq), F32),
                        pltpu.VMEM((2 * FOX_HEADS, cb, tq), F32),
                        pltpu.VMEM((2 * FOX_HEADS, cb, tq), BF16),
                        pltpu.VMEM((2 * FOX_HEADS, 1, tq), F32)],
        compiler_params=_params(("parallel", "arbitrary")),
        name="fox_attention",
    )(qp, kp, vt)


def _hgrn_kernel(hq_ref, hf_ref, hi_ref, hg_ref, lb_ref, ng_ref, o_ref, ss_ref, cpad_ref, vpad_ref,
                 term_ref, score_ref, *, seq, rb, wl):
    ch = HGRN_CHUNK
    n_states = wl // LANES
    rr = lax.broadcasted_iota(I32, (rb, rb), 0)
    cc = lax.broadcasted_iota(I32, (rb, rb), 1)
    same_chunk = (rr // ch) == (cc // ch)
    cum_mat = jnp.concatenate([(same_chunk & (cc <= rr)).astype(BF16), same_chunk.astype(BF16)], axis=0)
    hr = lax.broadcasted_iota(I32, (wl, wl), 0) // HEAD_DIM
    hc = lax.broadcasted_iota(I32, (wl, wl), 1) // HEAD_DIM
    head_ones = (hr == hc).astype(BF16)
    head_mask = (hr == hc)[:LANES, :LANES].astype(F32)
    tmod = lax.broadcasted_iota(I32, (rb, wl), 0) % ch
    lb = lb_ref[...]
    ng = ng_ref[...]
    ss_ref[...] = jnp.zeros(ss_ref.shape, F32)
    cpad_ref[0:ch, :] = jnp.zeros((ch, wl), F32)
    vpad_ref[0:ch, :] = jnp.zeros((ch, wl), F32)

    def block(r, _):
        rows = pl.ds(pl.multiple_of(r * rb, rb), rb)
        z = hf_ref[rows, :]
        g = jnp.log(lb + (1.0 - lb) * _sigmoid(z))
        k = (1.0 - lb) * _sigmoid(-z)
        hq = hq_ref[rows, :].astype(F32)
        qs = hq * _sigmoid(hq)
        v = hi_ref[rows, :].astype(F32)
        cums = _dot_exact_lhs(cum_mat, g * LOG2E)
        b = cums[:rb]
        btot = cums[rb:]
        qd = (qs * jnp.exp2(b)).astype(BF16)
        kd = (k * jnp.exp2(btot - b)).astype(BF16)
        dec = jnp.exp2(btot)

        c = b - jnp.log2(k)
        cpad_ref[ch:ch + rb, :] = c
        vpad_ref[ch:ch + rb, :] = v
        term_ref[0:rb, :] = (qs * k).astype(BF16)
        for o in range(1, ch):
            cs = cpad_ref[ch - o:ch - o + rb, :]
            term = jnp.where(tmod >= o, qs * jnp.exp2(b - cs), 0.0)
            term_ref[o * rb:(o + 1) * rb, :] = term.astype(BF16)
        score_ref[...] = _dot(term_ref[...], head_ones)
        acc = score_ref[0:rb, :] * v
        for o in range(1, ch):
            acc = acc + score_ref[o * rb:(o + 1) * rb, :] * vpad_ref[ch - o:ch - o + rb, :]

        vb = v.astype(BF16)
        chunks = [slice(n * ch, (n + 1) * ch) for n in range(rb // ch)]
        inter_cols = []
        for sidx in range(n_states):
            ln = slice(sidx * LANES, (sidx + 1) * LANES)
            upds = [_dot_tn(vb[sl, ln], kd[sl, ln]) * head_mask for sl in chunks]
            ss = ss_ref[sidx]
            states = []
            for n, sl in enumerate(chunks):
                states.append(ss.astype(BF16))
                ss = ss * dec[n * ch:n * ch + 1, ln] + upds[n]
            ss_ref[sidx] = ss
            inter_cols.append(jnp.concatenate(
                [_dot_nt(qd[sl, ln], st) for sl, st in zip(chunks, states)], axis=0))
        o_blk = acc + jnp.concatenate(inter_cols, axis=1)
        ms = _dot_exact_rhs(o_blk * o_blk, head_ones) * (1.0 / HEAD_DIM)
        hg = hg_ref[rows, :].astype(F32)
        o_blk = o_blk * lax.rsqrt(ms + RMS_EPS) * ng * (hg * _sigmoid(hg))
        o_ref[rows, :] = o_blk.astype(BF16)
        return 0

    lax.fori_loop(0, seq // rb, block, 0, unroll=2 if (seq // rb) % 2 == 0 else 1)


def _hgrn(hq, hf, hi, hg, lb, ng, layer, batch, seq):
    T = hq.shape[0]
    rb = _row_tile(seq, 128)
    wl = 2 * LANES
    n_prog = HGRN_WIDTH // wl
    blk = lambda: pl.BlockSpec((seq, wl), lambda b, p: (b, p))
    par = lambda: pl.BlockSpec((None, 1, wl), lambda b, p: (layer * n_prog + p, 0, 0))
    return pl.pallas_call(
        functools.partial(_hgrn_kernel, seq=seq, rb=rb, wl=wl),
        grid=(batch, n_prog),
        in_specs=[blk(), blk(), blk(), blk(), par(), par()],
        out_specs=blk(),
        out_shape=jax.ShapeDtypeStruct((T, HGRN_WIDTH), BF16),
        scratch_shapes=[pltpu.VMEM((wl // LANES, LANES, LANES), F32),
                        pltpu.VMEM((HGRN_CHUNK + rb, wl), F32),
                        pltpu.VMEM((HGRN_CHUNK + rb, wl), F32),
                        pltpu.VMEM((HGRN_CHUNK * rb, wl), BF16),
                        pltpu.VMEM((HGRN_CHUNK * rb, wl), F32)],
        compiler_params=_params(("parallel", "parallel")),
        name="hgrn2",
    )(hq, hf, hi, hg, lb.reshape(-1, 1, wl), ng.reshape(-1, 1, wl))


def _merge_kernel(fo_ref, ho_ref, gates_ref, x_ref, wf_ref, wh_ref, wm_ref, bm_ref, g_ref, b_ref,
                  wrh_ref, wrl_ref, rb_ref, tri_ref, x1r_ref, route_ref, cnt_ref,
                  *, alpha, d_model):
    i = pl.program_id(0)
    tm = x_ref.shape[0]

    @pl.when(i == 0)
    def _():
        cnt_ref[...] = jnp.zeros(cnt_ref.shape, F32)

    y_fox = _dot(fo_ref[...], wf_ref[...])
    y_hgrn = _dot(ho_ref[...], wh_ref[...])
    g_fox = _sigmoid(gates_ref[:, :d_model].astype(F32))
    g_hgrn = _sigmoid(gates_ref[:, d_model:].astype(F32))
    mixed = _dot((g_fox * y_fox + g_hgrn * y_hgrn).astype(BF16), wm_ref[...]) + bm_ref[...]
    x1 = _layer_norm(alpha * x_ref[...] + mixed, g_ref[...], b_ref[...])
    nj = d_model // LANES
    for j in range(nj):
        x1r_ref[pl.ds(j, tm, stride=nj), :] = x1[:, j * LANES:(j + 1) * LANES]

    xh = x1.astype(BF16)
    xl = (x1 - xh.astype(F32)).astype(BF16)
    logits = _dot(xh, wrh_ref[...]) + _dot(xl, wrh_ref[...]) + _dot(xh, wrl_ref[...]) + rb_ref[...]
    lane = lax.broadcasted_iota(I32, (tm, LANES), 1)
    lane_f = lane.astype(F32)
    neg = jnp.float32(-jnp.inf)
    lg = jnp.where(lane < N_EXPERTS, logits, neg)
    m1 = jnp.max(lg, axis=-1, keepdims=True)
    idx1 = jnp.min(jnp.where(lg == m1, lane_f, float(LANES)), axis=-1, keepdims=True).astype(I32)
    in_group = (lane // EXPERTS_PER_GROUP == idx1 // EXPERTS_PER_GROUP) & (lane < N_EXPERTS)
    lg2 = jnp.where(in_group & (lane != idx1), logits, neg)
    m2 = jnp.max(lg2, axis=-1, keepdims=True)
    idx2 = jnp.min(jnp.where(lg2 == m2, lane_f, float(LANES)), axis=-1, keepdims=True).astype(I32)
    e21 = jnp.exp(m2 - m1)
    gate1 = 1.0 / (1.0 + e21)
    gate2 = e21 / (1.0 + e21)

    oh1 = lane == idx1
    oh2 = lane == idx2
    oh = (oh1 | oh2).astype(F32)
    before = _dot(tri_ref[...], oh.astype(BF16)) + cnt_ref[0:1, :]
    rank1 = jnp.sum(jnp.where(oh1, before, 0.0), axis=-1, keepdims=True)
    rank2 = jnp.sum(jnp.where(oh2, before, 0.0), axis=-1, keepdims=True)
    cnt_ref[0:1, :] = cnt_ref[0:1, :] + jnp.sum(oh, axis=0, keepdims=True)

    route = jnp.where(lane == 0, idx1.astype(F32), 0.0)
    route = jnp.where(lane == 1, idx2.astype(F32), route)
    route = jnp.where(lane == 2, gate1, route)
    route = jnp.where(lane == 3, gate2, route)
    route = jnp.where(lane == 4, rank1, route)
    route = jnp.where(lane == 5, rank2, route)
    route_ref[...] = route[:, :ROUTE_COLS]


def _merge(fox_o, hgrn_o, gates, x, wf, wh, wm, bm, g, b, wrh, wrl, rbias, tri, layer, alpha):
    T, D = x.shape
    tm = tri.shape[0]
    nj = D // LANES
    row = lambda n: pl.BlockSpec((tm, n), lambda i: (i, 0))
    lw = lambda r, c: pl.BlockSpec((None, r, c), lambda i: (layer, 0, 0))
    cw = lambda r, c: pl.BlockSpec((r, c), lambda i: (0, 0))
    return pl.pallas_call(
        functools.partial(_merge_kernel, alpha=alpha, d_model=D),
        grid=(T // tm,),
        in_specs=[row(FOX_WIDTH), row(HGRN_WIDTH), row(2 * D), row(D),
                  lw(FOX_WIDTH, D), lw(HGRN_WIDTH, D), lw(D, D), lw(1, D), lw(1, D), lw(1, D),
                  cw(D, LANES), cw(D, LANES), cw(1, LANES), cw(tm, tm)],
        out_specs=[pl.BlockSpec((tm * nj, LANES), lambda i: (i, 0)), row(ROUTE_COLS),
                   pl.BlockSpec((SUBLANES, LANES), lambda i: (0, 0))],
        out_shape=[jax.ShapeDtypeStruct((T * nj, LANES), F32),
                   jax.ShapeDtypeStruct((T, ROUTE_COLS), F32), jax.ShapeDtypeStruct((SUBLANES, LANES), F32)],
        compiler_params=_params(("arbitrary",)),
        name="merge_router",
    )(fox_o, hgrn_o, gates, x, wf, wh, wm, bm, g, b, wrh, wrl, rbias, tri)


def _dispatch_kernel(dest_ref, x_ref, xs_hbm, inv_ref, sem, *, tc, nj):
    i = pl.program_id(0)

    def row_copy(r, d):
        return pltpu.make_async_copy(
            x_ref.at[pl.ds(pl.multiple_of(r * nj, nj), nj), :],
            xs_hbm.at[pl.ds(pl.multiple_of(d * nj, nj), nj), :], sem)

    def issue(r, _):
        d0 = dest_ref[0, 0, 2 * r]
        d1 = dest_ref[0, 0, 2 * r + 1]
        assignment = 2 * (i * tc + r)
        inv_ref[d0] = assignment
        inv_ref[d1] = assignment + 1
        row_copy(r, d0).start(priority=0)
        row_copy(r, d1).start(priority=1)
        return 0

    lax.fori_loop(0, tc, issue, 0, unroll=8)

    def drain(r, _):
        row_copy(0, 0).wait()
        row_copy(0, 0).wait()
        return 0

    lax.fori_loop(0, tc, drain, 0, unroll=8)


def _dispatch(x1r, dest, n_tokens, nj):
    tc = _row_tile(n_tokens, 512)
    n_steps = n_tokens // tc
    return pl.pallas_call(
        functools.partial(_dispatch_kernel, tc=tc, nj=nj),
        grid=(n_steps,),
        in_specs=[pl.BlockSpec((1, 1, 2 * tc), lambda i: (i, 0, 0), memory_space=pltpu.SMEM),
                  pl.BlockSpec((tc * nj, LANES), lambda i: (i, 0))],
        out_specs=[pl.BlockSpec(memory_space=pl.ANY), pl.BlockSpec(memory_space=pltpu.SMEM)],
        out_shape=[jax.ShapeDtypeStruct((2 * n_tokens * nj, LANES), F32),
                   jax.ShapeDtypeStruct((2 * n_tokens,), I32)],
        scratch_shapes=[pltpu.SemaphoreType.DMA(())],
        compiler_params=pltpu.CompilerParams(dimension_semantics=("arbitrary",), has_side_effects=True),
        name="moe_dispatch",
    )(dest.reshape(n_steps, 1, 2 * tc), x1r)


def _experts_kernel(meta_ref, slot_ref, xs_ref, w1_ref, w3_ref, w2_ref, y2_hbm, ybuf_ref, sem,
                    *, tm, nj, n_assign):
    w = pl.program_id(0)
    lo = meta_ref[2, w]
    hi = meta_ref[3, w]
    prev = jnp.maximum(w - 1, 0)
    prev_real = (w >= 1) & (meta_ref[3, prev] > meta_ref[2, prev])
    cur_par = w % 2
    prev_par = 1 - cur_par

    def row_copy(par, r, slot):
        return pltpu.make_async_copy(
            ybuf_ref.at[par, pl.ds(pl.multiple_of(r * nj, nj), nj), :],
            y2_hbm.at[pl.ds(pl.multiple_of(slot * nj, nj), nj), :], sem.at[par])

    def send_previous_rows():
        for r in range(tm):
            row_copy(prev_par, r, slot_ref[0, 0, r]).start(priority=r % 2)

    def wait_rows(par):
        pltpu.make_async_copy(ybuf_ref.at[par], y2_hbm.at[pl.ds(0, tm * nj), :], sem.at[par]).wait()

    @pl.when(w == 0)
    def _():
        ybuf_ref[...] = jnp.zeros(ybuf_ref.shape, F32)
        spare0 = pltpu.make_async_copy(ybuf_ref.at[0], y2_hbm.at[pl.ds(n_assign * nj, tm * nj), :], sem.at[0])
        spare0.start()
        spare0.wait()

    @pl.when(hi > lo)
    def _():
        send_previous_rows()
        x = jnp.concatenate([xs_ref[pl.ds(j, tm, stride=nj), :] for j in range(nj)], axis=-1).astype(BF16)
        h1 = _dot(x, w1_ref[...].astype(BF16))
        h3 = _dot(x, w3_ref[...].astype(BF16))
        h = (h1 * _sigmoid(h1) * h3).astype(BF16)
        y = _dot(h, w2_ref[...].astype(BF16))

        @pl.when(w >= 1)
        def _():
            wait_rows(cur_par)

        for j in range(nj):
            ybuf_ref[cur_par, pl.ds(j, tm, stride=nj), :] = y[:, j * LANES:(j + 1) * LANES]

    @pl.when((hi == lo) & prev_real)
    def _():
        send_previous_rows()
        wait_rows(0)
        wait_rows(1)


def _experts(meta, inv, xs, w1, w3, w2, layer, tm, nj):
    n_items = meta.shape[1]
    D = nj * LANES
    dh = w1.shape[-1]
    n_assign = xs.shape[0] // nj
    tile, lo, hi = meta[0], meta[2], meta[3]
    r = jnp.arange(tm, dtype=I32)[None, :]
    owned = (r >= lo[:, None]) & (r < hi[:, None])
    spare = n_assign + (jnp.arange(n_items, dtype=I32)[:, None] % 2) * tm + r
    slots = jnp.where(owned, inv[tile[:, None] * tm + r], spare)
    slots = jnp.concatenate([n_assign + tm + r, slots[:-1]], axis=0).reshape(n_items, 1, tm)
    grid_spec = pltpu.PrefetchScalarGridSpec(
        num_scalar_prefetch=1,
        grid=(n_items,),
        in_specs=[pl.BlockSpec((1, 1, tm), lambda w, m: (w, 0, 0), memory_space=pltpu.SMEM),
                  pl.BlockSpec((tm * nj, LANES), lambda w, m: (m[0, w], 0)),
                  pl.BlockSpec((None, None, D, dh), lambda w, m: (layer, m[1, w], 0, 0)),
                  pl.BlockSpec((None, None, D, dh), lambda w, m: (layer, m[1, w], 0, 0)),
                  pl.BlockSpec((None, None, dh, D), lambda w, m: (layer, m[1, w], 0, 0))],
        out_specs=pl.BlockSpec(memory_space=pl.ANY),
        scratch_shapes=[pltpu.VMEM((2, tm * nj, LANES), F32), pltpu.SemaphoreType.DMA((2,))],
    )
    return pl.pallas_call(
        functools.partial(_experts_kernel, tm=tm, nj=nj, n_assign=n_assign),
        grid_spec=grid_spec,
        out_shape=jax.ShapeDtypeStruct(((n_assign + 2 * tm) * nj, LANES), F32),
        compiler_params=pltpu.CompilerParams(dimension_semantics=("arbitrary",), vmem_limit_bytes=VMEM_LIMIT,
                                             has_side_effects=True),
        name="moe_experts",
    )(meta, slots, xs, w1, w3, w2)


def _combine_kernel(y2_ref, route_ref, x1r_ref, g_ref, b_ref, o_ref, *, tc, nj, alpha):
    o_ref[...] = _combined_rows(y2_ref, route_ref, x1r_ref, g_ref, b_ref, tc=tc, nj=nj, alpha=alpha)


def _combine(y2, route, x1r, g, b, layer, alpha):
    T = route.shape[0]
    nj = x1r.shape[0] // T
    D = nj * LANES
    tc = _row_tile(T, 512)
    row = lambda n: pl.BlockSpec((tc, n), lambda i: (i, 0))
    lw = lambda r, c: pl.BlockSpec((None, r, c), lambda i: (layer, 0, 0))
    return pl.pallas_call(
        functools.partial(_combine_kernel, tc=tc, nj=nj, alpha=alpha),
        grid=(T // tc,),
        in_specs=[pl.BlockSpec((tc * TOP_K * nj, LANES), lambda i: (i, 0)), row(ROUTE_COLS),
                  pl.BlockSpec((tc * nj, LANES), lambda i: (i, 0)), lw(1, D), lw(1, D)],
        out_specs=row(D),
        out_shape=jax.ShapeDtypeStruct((T, D), F32),
        compiler_params=_params(("parallel",)),
        name="moe_combine",
    )(y2, route, x1r, g, b)


def _routing_tables(route, counts, tm, n_items):
    e = route[:, 0:2].astype(I32)
    rank = route[:, 4:6].astype(I32)
    cnt = counts[0, :N_EXPERTS].astype(I32)
    ends = jnp.cumsum(cnt)
    starts = ends - cnt
    dest = (starts[e] + rank).reshape(-1)

    first_tile = starts // tm
    last_tile = jnp.maximum(ends - 1, 0) // tm
    n_tiles_e = jnp.where(cnt > 0, last_tile - first_tile + 1, 0)
    item_end = jnp.cumsum(n_tiles_e)
    item_start = item_end - n_tiles_e
    n_real = item_end[-1]
    w = jnp.arange(n_items, dtype=I32)
    wc = jnp.minimum(w, n_real - 1)
    ex = jnp.sum((item_end[None, :] <= wc[:, None]).astype(I32), axis=1)
    tile = first_tile[ex] + (wc - item_start[ex])
    lo = jnp.clip(starts[ex] - tile * tm, 0, tm)
    hi = jnp.clip(ends[ex] - tile * tm, 0, tm)
    real = w < n_real
    hi = jnp.where(real, hi, lo)
    meta = jnp.stack([tile, ex, lo, hi]).astype(I32)
    return dest, meta


def kernel(x, ln_in_g, ln_in_b, w_in, b_in, w_fox_branch, hgrn_lb_logits, hgrn_norm_g, w_hgrn_branch,
           w_mix_out, b_mix_out, ln1_g, ln1_b, router_w, router_b, expert_w1, expert_w3, expert_w2,
           ln2_g, ln2_b):
    batch, seq, D = x.shape
    depth = w_in.shape[0]
    T = batch * seq
    nj = D // LANES
    alpha = float((2 * depth) ** 0.25)
    assert D % 512 == 0 and seq % HGRN_CHUNK == 0

    sizes = (FOX_WIDTH, FOX_WIDTH, FOX_WIDTH, FOX_HEADS, HGRN_WIDTH, HGRN_WIDTH, HGRN_WIDTH, HGRN_WIDTH, 2 * D)
    offs = [0]
    for s in sizes:
        offs.append(offs[-1] + s)
    col = lambda a, i: a[..., offs[i]:offs[i + 1]]
    order = (0, 1, 2, 4, 6, 7, 8, 5)
    pad_ff = lambda a: jnp.pad(col(a, 3), [(0, 0)] * (a.ndim - 1) + [(0, LANES - FOX_HEADS)])
    w_all = jnp.concatenate([col(w_in, i) for i in order] + [pad_ff(w_in)], axis=-1).astype(BF16)
    b_all = jnp.concatenate([col(b_in, i) for i in order] + [pad_ff(b_in)], axis=-1).astype(F32)[:, None, :]

    lb_p = jax.nn.softmax(hgrn_lb_logits.astype(F32), axis=0)
    lb_all = (jnp.cumsum(lb_p, axis=0) - lb_p[0]).reshape(depth * HGRN_PAIRS, 1, LANES)
    ng_all = hgrn_norm_g.astype(F32).reshape(depth * HGRN_PAIRS, 1, LANES)

    wf = w_fox_branch.astype(BF16)
    wh = w_hgrn_branch.astype(BF16)
    wm = w_mix_out.astype(BF16)
    r3 = lambda a: a.astype(F32)[:, None, :]
    bm, g1, b1, g2, b2 = r3(b_mix_out), r3(ln1_g), r3(ln1_b), r3(ln2_g), r3(ln2_b)
    rw = jnp.pad(router_w.astype(F32), ((0, 0), (0, LANES - N_EXPERTS)))
    wrh = rw.astype(BF16)
    wrl = (rw - wrh.astype(F32)).astype(BF16)
    rbias = jnp.pad(router_b.astype(F32), (0, LANES - N_EXPERTS)).reshape(1, LANES)

    tm_merge = _row_tile(T, 512)
    tri = jnp.tril(jnp.ones((tm_merge, tm_merge), BF16), k=-1)
    tm_exp = _row_tile(2 * T, 512)
    n_items = (2 * T) // tm_exp + N_EXPERTS
    tm_in = _row_tile(seq, 512)
    fox_block = _row_tile(tm_in, 256)
    tri_in = jnp.tril(jnp.ones((tm_in, tm_in), BF16))

    source = (x.reshape(T, D), ln_in_g.astype(F32).reshape(1, D), ln_in_b.astype(F32).reshape(1, D))
    for l in range(depth):
        xc, qp, kp, vt, hq, hi, hg, gates, hf = _inproj(source, w_all, b_all, tri_in, l, seq, fox_block, alpha)
        fox_o = _fox_attention(qp, kp, vt, batch, seq, fox_block)
        hgrn_o = _hgrn(hq, hf, hi, hg, lb_all, ng_all, l, batch, seq)
        x1r, route, counts = _merge(fox_o, hgrn_o, gates, xc, wf, wh, wm, bm, g1, b1,
                                    wrh, wrl, rbias, tri, l, alpha)
        dest, meta = _routing_tables(route, counts, tm_exp, n_items)
        xs, inv = _dispatch(x1r, dest, T, nj)
        y2 = _experts(meta, inv, xs, expert_w1, expert_w3, expert_w2, l, tm_exp, nj)
        source = (y2, route, x1r, g2, b2)
    out = _combine(*source, depth - 1, alpha)
    return out.reshape(batch, seq, D)
```

```python
import functools

import jax
import jax.numpy as jnp
from jax import lax
from jax.experimental import pallas as pl
from jax.experimental.pallas import tpu as pltpu

F32 = jnp.float32
BF16 = jnp.bfloat16
I32 = jnp.int32

LANES = 128
SUBLANES = 8
HEAD_DIM = 64
FOX_HEADS = 8
HGRN_HEADS = 8
HEADS_PER_VREG = LANES // HEAD_DIM
FOX_PAIRS = FOX_HEADS // HEADS_PER_VREG
HGRN_PAIRS = HGRN_HEADS // HEADS_PER_VREG
FOX_WIDTH = FOX_HEADS * HEAD_DIM
HGRN_WIDTH = HGRN_HEADS * HEAD_DIM
N_EXPERTS = 16
N_GROUPS = 4
EXPERTS_PER_GROUP = N_EXPERTS // N_GROUPS
TOP_K = 2
ROUTE_COLS = 8
HGRN_CHUNK = 16
LN_EPS = 1e-5
RMS_EPS = 1e-6
MASK_VALUE = -1e30
LOG2E = 1.4426950408889634
VMEM_LIMIT = 56 * 1024 * 1024

_C_FQ = 0
_C_FK = _C_FQ + FOX_WIDTH
_C_FV = _C_FK + FOX_WIDTH
_C_HQ = _C_FV + FOX_WIDTH
_C_HI = _C_HQ + HGRN_WIDTH
_C_HG = _C_HI + HGRN_WIDTH
_C_GATES = _C_HG + HGRN_WIDTH


def _params(sem, vmem=VMEM_LIMIT):
    return pltpu.CompilerParams(dimension_semantics=sem, vmem_limit_bytes=vmem)


def _split3(x):
    hi = x.astype(BF16)
    r1 = x - hi.astype(F32)
    mid = r1.astype(BF16)
    lo = (r1 - mid.astype(F32)).astype(BF16)
    return hi, mid, lo


def _dot(a, b):
    return jnp.dot(a, b, preferred_element_type=F32)


def _dot_nt(a, b):
    return lax.dot_general(a, b, (((1,), (1,)), ((), ())), preferred_element_type=F32)


def _dot_tn(a, b):
    return lax.dot_general(a, b, (((0,), (0,)), ((), ())), preferred_element_type=F32)


def _dot_exact_lhs(m, x):
    hi, mid, lo = _split3(x)
    return _dot(m, hi) + _dot(m, mid) + _dot(m, lo)


def _dot_exact_rhs(x, m):
    hi, mid, lo = _split3(x)
    return _dot(hi, m) + _dot(mid, m) + _dot(lo, m)


def _sigmoid(x):
    return 1.0 / (1.0 + jnp.exp(-x))


def _layer_norm(x, g, b):
    mu = jnp.mean(x, axis=-1, keepdims=True)
    xc = x - mu
    var = jnp.mean(xc * xc, axis=-1, keepdims=True)
    return xc * lax.rsqrt(var + LN_EPS) * g + b


def _row_tile(n, want):
    t = min(n, want)
    assert n % t == 0, (n, t)
    return t


def _fox_head_lanes(h):
    lane = lax.broadcasted_iota(I32, (1, LANES), 1)
    hh = h % HEADS_PER_VREG
    own = (lane >= hh * HEAD_DIM) & (lane < (hh + 1) * HEAD_DIM)
    e0 = (1 - hh) * HEAD_DIM
    return lane, own, e0


def _combined_rows(y2_ref, route_ref, x1r_ref, g_ref, b_ref, *, tc, nj, alpha):
    gate1 = route_ref[:, 2:3]
    gate2 = route_ref[:, 3:4]
    resid = jnp.concatenate(
        [alpha * x1r_ref[pl.ds(j, tc, stride=nj), :]
         + gate1 * y2_ref[pl.ds(j, tc, stride=TOP_K * nj), :]
         + gate2 * y2_ref[pl.ds(nj + j, tc, stride=TOP_K * nj), :]
         for j in range(nj)], axis=-1)
    return _layer_norm(resid, g_ref[...], b_ref[...])


def _inproj_kernel(*refs, d_model, tiles_per_seq, cb, alpha, first_layer):
    i = pl.program_id(0)
    fc_ref, xin_ref = refs[-2], refs[-1]
    tile = jnp.maximum(i - 1, 0)

    @pl.when(i == 0)
    def _():
        xin_ref[...] = jnp.zeros(xin_ref.shape, F32)

    @pl.when(tile % tiles_per_seq == 0)
    def _():
        fc_ref[...] = jnp.zeros(fc_ref.shape, F32)

    if first_layer:
        x_ref, g_ref, bl_ref, *refs = refs
        tm = x_ref.shape[0]
    else:
        y2_ref, route_ref, x1r_ref, g_ref, bl_ref, *refs = refs
        tm = route_ref.shape[0]
    (w_ref, b_ref, tri_ref, xc_ref, qp_ref, kp_ref, vt_ref, hq_ref, hi_ref, hg_ref,
     gates_ref, hf_ref, _, _) = refs
    x = xin_ref[(i + 1) % 2]
    xc_ref[...] = x
    xb = x.astype(BF16)

    def proj(c0, n):
        return _dot(xb, w_ref[:, c0:c0 + n]) + b_ref[:, c0:c0 + n]

    c_hf = _C_GATES + 2 * d_model
    ff = proj(c_hf + HGRN_WIDTH, LANES)
    log_f = jnp.minimum(ff, 0.0) - jnp.log(1.0 + jnp.exp(-jnp.abs(ff)))
    f_cum = _dot_exact_lhs(tri_ref[...], log_f * LOG2E) + fc_ref[...]
    fc_ref[...] = f_cum[tm - 1:tm, :]

    q_all = proj(_C_FQ, FOX_WIDTH) * (HEAD_DIM ** -0.5 * LOG2E)
    k_all = proj(_C_FK, FOX_WIDTH)
    v_all = proj(_C_FV, FOX_WIDTH)
    for h in range(FOX_HEADS):
        p = h // HEADS_PER_VREG
        cols = slice(p * LANES, (p + 1) * LANES)
        lane, own, e0 = _fox_head_lanes(h)
        hi, mid, lo = (t.astype(F32) for t in _split3(f_cum[:, h:h + 1]))
        ext_q = jnp.where((lane >= e0 + 3) & (lane < e0 + 6), 1.0, 0.0)
        ext_q = jnp.where(lane == e0, hi, ext_q)
        ext_q = jnp.where(lane == e0 + 1, mid, ext_q)
        ext_q = jnp.where(lane == e0 + 2, lo, ext_q)
        ext_k = jnp.where((lane >= e0) & (lane < e0 + 3), 1.0, 0.0)
        ext_k = jnp.where(lane == e0 + 3, -hi, ext_k)
        ext_k = jnp.where(lane == e0 + 4, -mid, ext_k)
        ext_k = jnp.where(lane == e0 + 5, -lo, ext_k)
        qp_ref[h] = jnp.where(own, q_all[:, cols], ext_q).astype(BF16)
        kp = jnp.where(own, k_all[:, cols], ext_k).astype(BF16)
        vt = jnp.where(own, v_all[:, cols], 0.0).T
        for blk in range(tm // cb):
            kp_ref[h, blk] = kp[blk * cb:(blk + 1) * cb]
            vt_ref[h, blk] = vt[:, blk * cb:(blk + 1) * cb].astype(BF16)

    nj = d_model // LANES
    strip = 32

    def next_input_strip(r0):
        if first_layer:
            x_next = _layer_norm(x_ref[r0:r0 + strip, :], g_ref[...], bl_ref[...])
        else:
            x_next = _combined_rows(y2_ref.at[pl.ds(r0 * TOP_K * nj, strip * TOP_K * nj), :],
                                    route_ref.at[pl.ds(r0, strip), :],
                                    x1r_ref.at[pl.ds(r0 * nj, strip * nj), :],
                                    g_ref, bl_ref, tc=strip, nj=nj, alpha=alpha)
        xin_ref[i % 2, r0:r0 + strip, :] = x_next

    strips = iter(range(0, tm, strip))
    n_gate_chunks = 2 * d_model // 512
    projections = ([(hq_ref, _C_HQ, BF16), (hi_ref, _C_HI, BF16), (hg_ref, _C_HG, BF16)]
                   + [(gates_ref.at[:, pl.ds(c * 512, 512)], _C_GATES + c * 512, BF16) for c in range(n_gate_chunks)]
                   + [(hf_ref, c_hf, F32)])
    for out_ref, c0, dt in projections:
        out_ref[...] = proj(c0, out_ref.shape[1]).astype(dt)
        r0 = next(strips, None)
        if r0 is not None:
            next_input_strip(r0)
    for r0 in strips:
        next_input_strip(r0)


def _inproj(source, w, b, tri, layer, seq, cb, alpha):
    first_layer = len(source) == 3
    n_all = w.shape[-1]
    tm = tri.shape[0]
    H = FOX_HEADS
    T = source[0].shape[0] if first_layer else source[1].shape[0]
    n_tiles = T // tm
    src = lambda i: jnp.minimum(i, n_tiles - 1)
    dst = lambda i: jnp.maximum(i - 1, 0)
    src_row = lambda rows, cols: pl.BlockSpec((rows, cols), lambda i: (src(i), 0))
    out_row = lambda n: pl.BlockSpec((tm, n), lambda i: (dst(i), 0))
    if first_layer:
        D = source[0].shape[1]
        src_specs = [src_row(tm, D), pl.BlockSpec((1, D), lambda i: (0, 0)), pl.BlockSpec((1, D), lambda i: (0, 0))]
    else:
        nj = source[2].shape[0] // T
        D = nj * LANES
        prev = layer - 1
        src_specs = [src_row(tm * TOP_K * nj, LANES), src_row(tm, ROUTE_COLS), src_row(tm * nj, LANES),
                     pl.BlockSpec((None, 1, D), lambda i: (prev, 0, 0)),
                     pl.BlockSpec((None, 1, D), lambda i: (prev, 0, 0))]
    outs = [(D, F32)] + [(HGRN_WIDTH, BF16)] * 3 + [(2 * D, BF16), (HGRN_WIDTH, F32)]
    attn = [((H, T, LANES), (H, tm, LANES), lambda i: (0, dst(i), 0)),
            ((H, T // cb, cb, LANES), (H, tm // cb, cb, LANES), lambda i: (0, dst(i), 0, 0)),
            ((H, T // cb, LANES, cb), (H, tm // cb, LANES, cb), lambda i: (0, dst(i), 0, 0))]
    return pl.pallas_call(
        functools.partial(_inproj_kernel, d_model=D, tiles_per_seq=seq // tm, cb=cb, alpha=alpha,
                          first_layer=first_layer),
        grid=(n_tiles + 1,),
        in_specs=src_specs
                 + [pl.BlockSpec((None, D, n_all), lambda i: (layer, 0, 0), pipeline_mode=pl.Buffered(1)),
                    pl.BlockSpec((None, 1, n_all), lambda i: (layer, 0, 0), pipeline_mode=pl.Buffered(1)),
                    pl.BlockSpec((tm, tm), lambda i: (0, 0), pipeline_mode=pl.Buffered(1))],
        out_specs=[out_row(D)] + [pl.BlockSpec(blk, imap) for _, blk, imap in attn]
                  + [out_row(n) for n, _ in outs[1:]],
        out_shape=[jax.ShapeDtypeStruct((T, D), F32)]
                  + [jax.ShapeDtypeStruct(shape, BF16) for shape, _, _ in attn]
                  + [jax.ShapeDtypeStruct((T, n), dt) for n, dt in outs[1:]],
        scratch_shapes=[pltpu.VMEM((1, LANES), F32), pltpu.VMEM((2, tm, D), F32)],
        compiler_params=_params(("arbitrary",)),
        name="inproj",
    )(*source, w, b, tri)


def _fox_kernel(qp_ref, kp_ref, vt_ref, o_ref, m_ref, l_ref, acc_ref, s_ref, p_ref, a_ref, *, tq, cb):
    i = pl.program_id(1)
    key_i = lax.broadcasted_iota(I32, (tq, tq), 0)
    qry_i = lax.broadcasted_iota(I32, (tq, tq), 1)
    causal = key_i <= qry_i
    for h in range(FOX_HEADS):
        m_ref[h] = jnp.full((1, tq), MASK_VALUE, F32)
        l_ref[h] = jnp.zeros((1, tq), F32)
        acc_ref[h] = jnp.zeros((LANES, tq), F32)

    def scores(j, par):
        for h in range(FOX_HEADS):
            s_ref[par * FOX_HEADS + h] = _dot_nt(kp_ref[h, j], qp_ref[h])

    def softmax(par, masked):
        for h in range(FOX_HEADS):
            st = s_ref[par * FOX_HEADS + h]
            if masked:
                st = jnp.where(causal, st, MASK_VALUE)
            m = m_ref[h]
            m_new = jnp.maximum(m, jnp.max(st, axis=0, keepdims=True))
            a = jnp.exp2(m - m_new)
            pt = jnp.exp2(st - m_new)
            m_ref[h] = m_new
            l_ref[h] = a * l_ref[h] + jnp.sum(pt, axis=0, keepdims=True)
            p_ref[par * FOX_HEADS + h] = pt.astype(BF16)
            a_ref[par * FOX_HEADS + h] = a

    def values(j, par):
        for h in range(FOX_HEADS):
            acc_ref[h] = a_ref[par * FOX_HEADS + h] * acc_ref[h] + _dot(vt_ref[h, j],
                                                                     p_ref[par * FOX_HEADS + h])

    for h in range(FOX_HEADS):
        p_ref[FOX_HEADS + h] = jnp.zeros((cb, tq), BF16)
        a_ref[FOX_HEADS + h] = jnp.ones((1, tq), F32)
    scores(0, 0)

    def trip(t, par):
        scores(t + 1, 1 - par)
        softmax(par, False)
        values(jnp.maximum(t - 1, 0), 1 - par)

    def two_trips(u, _):
        trip(2 * u, 0)
        trip(2 * u + 1, 1)
        return 0

    lax.fori_loop(0, i // 2, two_trips, 0)

    @pl.when(i % 2 == 1)
    def _():
        trip(i - 1, 0)
        softmax(1, True)
        values(i - 1, 0)
        values(i, 1)

    @pl.when(i % 2 == 0)
    def _():
        softmax(0, True)
        values(jnp.maximum(i - 1, 0), 1)
        values(i, 0)
    for p in range(FOX_PAIRS):
        h0 = p * HEADS_PER_VREG
        out_t = acc_ref[h0] / l_ref[h0]
        for h in range(h0 + 1, h0 + HEADS_PER_VREG):
            out_t = out_t + acc_ref[h] / l_ref[h]
        o_ref[:, p * LANES:(p + 1) * LANES] = out_t.T.astype(BF16)


def _fox_attention(qp, kp, vt, batch, seq, cb):
    H, T, _ = qp.shape
    tq = cb
    nq = seq // tq
    return pl.pallas_call(
        functools.partial(_fox_kernel, tq=tq, cb=cb),
        grid=(batch, nq),
        in_specs=[pl.BlockSpec((H, tq, LANES), lambda b, i: (0, b * nq + i, 0)),
                  pl.BlockSpec((H, nq, cb, LANES), lambda b, i: (0, b, 0, 0)),
                  pl.BlockSpec((H, nq, LANES, cb), lambda b, i: (0, b, 0, 0))],
        out_specs=pl.BlockSpec((tq, FOX_WIDTH), lambda b, i: (b * nq + i, 0)),
        out_shape=jax.ShapeDtypeStruct((T, FOX_WIDTH), BF16),
        scratch_shapes=[pltpu.VMEM((FOX_HEADS, 1, tq), F32),
                        pltpu.VMEM((FOX_HEADS, 1, tq), F32),
                        pltpu.VMEM((FOX_HEADS, LANES, tq), F32),
                        pltpu.VMEM((2 * FOX_HEADS, cb, tq), F32),
                        pltpu.VMEM((2 * FOX_HEADS, cb, tq), BF16),
                        pltpu.VMEM((2 * FOX_HEADS, 1, tq), F32)],
        compiler_params=_params(("parallel", "arbitrary")),
        name="fox_attention",
    )(qp, kp, vt)


def _hgrn_kernel(hq_ref, hf_ref, hi_ref, hg_ref, lb_ref, ng_ref, o_ref, ss_ref, cpad_ref, vpad_ref,
                 term_ref, score_ref, *, seq, rb, wl):
    ch = HGRN_CHUNK
    n_states = wl // LANES
    rr = lax.broadcasted_iota(I32, (rb, rb), 0)
    cc = lax.broadcasted_iota(I32, (rb, rb), 1)
    same_chunk = (rr // ch) == (cc // ch)
    cum_mat = jnp.concatenate([(same_chunk & (cc <= rr)).astype(BF16), same_chunk.astype(BF16)], axis=0)
    hr = lax.broadcasted_iota(I32, (wl, wl), 0) // HEAD_DIM
    hc = lax.broadcasted_iota(I32, (wl, wl), 1) // HEAD_DIM
    head_ones = (hr == hc).astype(BF16)
    head_mask = (hr == hc)[:LANES, :LANES].astype(F32)
    tmod = lax.broadcasted_iota(I32, (rb, wl), 0) % ch
    lb = lb_ref[...]
    ng = ng_ref[...]
    ss_ref[...] = jnp.zeros(ss_ref.shape, F32)
    cpad_ref[0:ch, :] = jnp.zeros((ch, wl), F32)
    vpad_ref[0:ch, :] = jnp.zeros((ch, wl), F32)

    def block(r, _):
        rows = pl.ds(pl.multiple_of(r * rb, rb), rb)
        z = hf_ref[rows, :]
        g = jnp.log(lb + (1.0 - lb) * _sigmoid(z))
        k = (1.0 - lb) * _sigmoid(-z)
        hq = hq_ref[rows, :].astype(F32)
        qs = hq * _sigmoid(hq)
        v = hi_ref[rows, :].astype(F32)
        cums = _dot_exact_lhs(cum_mat, g * LOG2E)
        b = cums[:rb]
        btot = cums[rb:]
        qd = (qs * jnp.exp2(b)).astype(BF16)
        kd = (k * jnp.exp2(btot - b)).astype(BF16)
        dec = jnp.exp2(btot)

        c = b - jnp.log2(k)
        cpad_ref[ch:ch + rb, :] = c
        vpad_ref[ch:ch + rb, :] = v
        term_ref[0:rb, :] = (qs * k).astype(BF16)
        for o in range(1, ch):
            cs = cpad_ref[ch - o:ch - o + rb, :]
            term = jnp.where(tmod >= o, qs * jnp.exp2(b - cs), 0.0)
            term_ref[o * rb:(o + 1) * rb, :] = term.astype(BF16)
        score_ref[...] = _dot(term_ref[...], head_ones)
        acc = score_ref[0:rb, :] * v
        for o in range(1, ch):
            acc = acc + score_ref[o * rb:(o + 1) * rb, :] * vpad_ref[ch - o:ch - o + rb, :]

        vb = v.astype(BF16)
        chunks = [slice(n * ch, (n + 1) * ch) for n in range(rb // ch)]
        inter_cols = []
        for sidx in range(n_states):
            ln = slice(sidx * LANES, (sidx + 1) * LANES)
            upds = [_dot_tn(vb[sl, ln], kd[sl, ln]) * head_mask for sl in chunks]
            ss = ss_ref[sidx]
            states = []
            for n, sl in enumerate(chunks):
                states.append(ss.astype(BF16))
                ss = ss * dec[n * ch:n * ch + 1, ln] + upds[n]
            ss_ref[sidx] = ss
            inter_cols.append(jnp.concatenate(
                [_dot_nt(qd[sl, ln], st) for sl, st in zip(chunks, states)], axis=0))
        o_blk = acc + jnp.concatenate(inter_cols, axis=1)
        ms = _dot_exact_rhs(o_blk * o_blk, head_ones) * (1.0 / HEAD_DIM)
        hg = hg_ref[rows, :].astype(F32)
        o_blk = o_blk * lax.rsqrt(ms + RMS_EPS) * ng * (hg * _sigmoid(hg))
        o_ref[rows, :] = o_blk.astype(BF16)
        return 0

    lax.fori_loop(0, seq // rb, block, 0, unroll=2 if (seq // rb) % 2 == 0 else 1)


def _hgrn(hq, hf, hi, hg, lb, ng, layer, batch, seq):
    T = hq.shape[0]
    rb = _row_tile(seq, 128)
    wl = 2 * LANES
    n_prog = HGRN_WIDTH // wl
    blk = lambda: pl.BlockSpec((seq, wl), lambda b, p: (b, p))
    par = lambda: pl.BlockSpec((None, 1, wl), lambda b, p: (layer * n_prog + p, 0, 0))
    return pl.pallas_call(
        functools.partial(_hgrn_kernel, seq=seq, rb=rb, wl=wl),
        grid=(batch, n_prog),
        in_specs=[blk(), blk(), blk(), blk(), par(), par()],
        out_specs=blk(),
        out_shape=jax.ShapeDtypeStruct((T, HGRN_WIDTH), BF16),
        scratch_shapes=[pltpu.VMEM((wl // LANES, LANES, LANES), F32),
                        pltpu.VMEM((HGRN_CHUNK + rb, wl), F32),
                        pltpu.VMEM((HGRN_CHUNK + rb, wl), F32),
                        pltpu.VMEM((HGRN_CHUNK * rb, wl), BF16),
                        pltpu.VMEM((HGRN_CHUNK * rb, wl), F32)],
        compiler_params=_params(("parallel", "parallel")),
        name="hgrn2",
    )(hq, hf, hi, hg, lb.reshape(-1, 1, wl), ng.reshape(-1, 1, wl))


def _merge_kernel(fo_ref, ho_ref, gates_ref, x_ref, wf_ref, wh_ref, wm_ref, bm_ref, g_ref, b_ref,
                  wrh_ref, wrl_ref, rb_ref, tri_ref, x1r_ref, route_ref, cnt_ref,
                  *, alpha, d_model):
    i = pl.program_id(0)
    tm = x_ref.shape[0]

    @pl.when(i == 0)
    def _():
        cnt_ref[...] = jnp.zeros(cnt_ref.shape, F32)

    y_fox = _dot(fo_ref[...], wf_ref[...])
    y_hgrn = _dot(ho_ref[...], wh_ref[...])
    g_fox = _sigmoid(gates_ref[:, :d_model].astype(F32))
    g_hgrn = _sigmoid(gates_ref[:, d_model:].astype(F32))
    mixed = _dot((g_fox * y_fox + g_hgrn * y_hgrn).astype(BF16), wm_ref[...]) + bm_ref[...]
    x1 = _layer_norm(alpha * x_ref[...] + mixed, g_ref[...], b_ref[...])
    nj = d_model // LANES
    for j in range(nj):
        x1r_ref[pl.ds(j, tm, stride=nj), :] = x1[:, j * LANES:(j + 1) * LANES]

    xh = x1.astype(BF16)
    xl = (x1 - xh.astype(F32)).astype(BF16)
    logits = _dot(xh, wrh_ref[...]) + _dot(xl, wrh_ref[...]) + _dot(xh, wrl_ref[...]) + rb_ref[...]
    lane = lax.broadcasted_iota(I32, (tm, LANES), 1)
    lane_f = lane.astype(F32)
    neg = jnp.float32(-jnp.inf)
    lg = jnp.where(lane < N_EXPERTS, logits, neg)
    m1 = jnp.max(lg, axis=-1, keepdims=True)
    idx1 = jnp.min(jnp.where(lg == m1, lane_f, float(LANES)), axis=-1, keepdims=True).astype(I32)
    in_group = (lane // EXPERTS_PER_GROUP == idx1 // EXPERTS_PER_GROUP) & (lane < N_EXPERTS)
    lg2 = jnp.where(in_group & (lane != idx1), logits, neg)
    m2 = jnp.max(lg2, axis=-1, keepdims=True)
    idx2 = jnp.min(jnp.where(lg2 == m2, lane_f, float(LANES)), axis=-1, keepdims=True).astype(I32)
    e21 = jnp.exp(m2 - m1)
    gate1 = 1.0 / (1.0 + e21)
    gate2 = e21 / (1.0 + e21)

    oh1 = lane == idx1
    oh2 = lane == idx2
    oh = (oh1 | oh2).astype(F32)
    before = _dot(tri_ref[...], oh.astype(BF16)) + cnt_ref[0:1, :]
    rank1 = jnp.sum(jnp.where(oh1, before, 0.0), axis=-1, keepdims=True)
    rank2 = jnp.sum(jnp.where(oh2, before, 0.0), axis=-1, keepdims=True)
    cnt_ref[0:1, :] = cnt_ref[0:1, :] + jnp.sum(oh, axis=0, keepdims=True)

    route = jnp.where(lane == 0, idx1.astype(F32), 0.0)
    route = jnp.where(lane == 1, idx2.astype(F32), route)
    route = jnp.where(lane == 2, gate1, route)
    route = jnp.where(lane == 3, gate2, route)
    route = jnp.where(lane == 4, rank1, route)
    route = jnp.where(lane == 5, rank2, route)
    route_ref[...] = route[:, :ROUTE_COLS]


def _merge(fox_o, hgrn_o, gates, x, wf, wh, wm, bm, g, b, wrh, wrl, rbias, tri, layer, alpha):
    T, D = x.shape
    tm = tri.shape[0]
    nj = D // LANES
    row = lambda n: pl.BlockSpec((tm, n), lambda i: (i, 0))
    lw = lambda r, c: pl.BlockSpec((None, r, c), lambda i: (layer, 0, 0))
    cw = lambda r, c: pl.BlockSpec((r, c), lambda i: (0, 0))
    return pl.pallas_call(
        functools.partial(_merge_kernel, alpha=alpha, d_model=D),
        grid=(T // tm,),
        in_specs=[row(FOX_WIDTH), row(HGRN_WIDTH), row(2 * D), row(D),
                  lw(FOX_WIDTH, D), lw(HGRN_WIDTH, D), lw(D, D), lw(1, D), lw(1, D), lw(1, D),
                  cw(D, LANES), cw(D, LANES), cw(1, LANES), cw(tm, tm)],
        out_specs=[pl.BlockSpec((tm * nj, LANES), lambda i: (i, 0)), row(ROUTE_COLS),
                   pl.BlockSpec((SUBLANES, LANES), lambda i: (0, 0))],
        out_shape=[jax.ShapeDtypeStruct((T * nj, LANES), F32),
                   jax.ShapeDtypeStruct((T, ROUTE_COLS), F32), jax.ShapeDtypeStruct((SUBLANES, LANES), F32)],
        compiler_params=_params(("arbitrary",)),
        name="merge_router",
    )(fox_o, hgrn_o, gates, x, wf, wh, wm, bm, g, b, wrh, wrl, rbias, tri)


def _dispatch_kernel(dest_ref, x_ref, xs_hbm, inv_ref, sem, *, tc, nj):
    i = pl.program_id(0)

    def row_copy(r, d):
        return pltpu.make_async_copy(
            x_ref.at[pl.ds(pl.multiple_of(r * nj, nj), nj), :],
            xs_hbm.at[pl.ds(pl.multiple_of(d * nj, nj), nj), :], sem)

    def issue(r, _):
        d0 = dest_ref[0, 0, 2 * r]
        d1 = dest_ref[0, 0, 2 * r + 1]
        assignment = 2 * (i * tc + r)
        inv_ref[d0] = assignment
        inv_ref[d1] = assignment + 1
        row_copy(r, d0).start(priority=0)
        row_copy(r, d1).start(priority=1)
        return 0

    lax.fori_loop(0, tc, issue, 0, unroll=8)

    def drain(r, _):
        row_copy(0, 0).wait()
        row_copy(0, 0).wait()
        return 0

    lax.fori_loop(0, tc, drain, 0, unroll=8)


def _dispatch(x1r, dest, n_tokens, nj):
    tc = _row_tile(n_tokens, 512)
    n_steps = n_tokens // tc
    return pl.pallas_call(
        functools.partial(_dispatch_kernel, tc=tc, nj=nj),
        grid=(n_steps,),
        in_specs=[pl.BlockSpec((1, 1, 2 * tc), lambda i: (i, 0, 0), memory_space=pltpu.SMEM),
                  pl.BlockSpec((tc * nj, LANES), lambda i: (i, 0))],
        out_specs=[pl.BlockSpec(memory_space=pl.ANY), pl.BlockSpec(memory_space=pltpu.SMEM)],
        out_shape=[jax.ShapeDtypeStruct((2 * n_tokens * nj, LANES), F32),
                   jax.ShapeDtypeStruct((2 * n_tokens,), I32)],
        scratch_shapes=[pltpu.SemaphoreType.DMA(())],
        compiler_params=pltpu.CompilerParams(dimension_semantics=("arbitrary",), has_side_effects=True),
        name="moe_dispatch",
    )(dest.reshape(n_steps, 1, 2 * tc), x1r)


def _experts_kernel(meta_ref, slot_ref, xs_ref, w1_ref, w3_ref, w2_ref, y2_hbm, ybuf_ref, sem,
                    *, tm, nj, n_assign):
    w = pl.program_id(0)
    lo = meta_ref[2, w]
    hi = meta_ref[3, w]
    prev = jnp.maximum(w - 1, 0)
    prev_real = (w >= 1) & (meta_ref[3, prev] > meta_ref[2, prev])
    cur_par = w % 2
    prev_par = 1 - cur_par

    def row_copy(par, r, slot):
        return pltpu.make_async_copy(
            ybuf_ref.at[par, pl.ds(pl.multiple_of(r * nj, nj), nj), :],
            y2_hbm.at[pl.ds(pl.multiple_of(slot * nj, nj), nj), :], sem.at[par])

    def send_previous_rows():
        for r in range(tm):
            row_copy(prev_par, r, slot_ref[0, 0, r]).start(priority=r % 2)

    def wait_rows(par):
        pltpu.make_async_copy(ybuf_ref.at[par], y2_hbm.at[pl.ds(0, tm * nj), :], sem.at[par]).wait()

    @pl.when(w == 0)
    def _():
        ybuf_ref[...] = jnp.zeros(ybuf_ref.shape, F32)
        spare0 = pltpu.make_async_copy(ybuf_ref.at[0], y2_hbm.at[pl.ds(n_assign * nj, tm * nj), :], sem.at[0])
        spare0.start()
        spare0.wait()

    @pl.when(hi > lo)
    def _():
        send_previous_rows()
        x = jnp.concatenate([xs_ref[pl.ds(j, tm, stride=nj), :] for j in range(nj)], axis=-1).astype(BF16)
        h1 = _dot(x, w1_ref[...].astype(BF16))
        h3 = _dot(x, w3_ref[...].astype(BF16))
        h = (h1 * _sigmoid(h1) * h3).astype(BF16)
        y = _dot(h, w2_ref[...].astype(BF16))

        @pl.when(w >= 1)
        def _():
            wait_rows(cur_par)

        for j in range(nj):
            ybuf_ref[cur_par, pl.ds(j, tm, stride=nj), :] = y[:, j * LANES:(j + 1) * LANES]

    @pl.when((hi == lo) & prev_real)
    def _():
        send_previous_rows()
        wait_rows(0)
        wait_rows(1)


def _experts(meta, inv, xs, w1, w3, w2, layer, tm, nj):
    n_items = meta.shape[1]
    D = nj * LANES
    dh = w1.shape[-1]
    n_assign = xs.shape[0] // nj
    tile, lo, hi = meta[0], meta[2], meta[3]
    r = jnp.arange(tm, dtype=I32)[None, :]
    owned = (r >= lo[:, None]) & (r < hi[:, None])
    spare = n_assign + (jnp.arange(n_items, dtype=I32)[:, None] % 2) * tm + r
    slots = jnp.where(owned, inv[tile[:, None] * tm + r], spare)
    slots = jnp.concatenate([n_assign + tm + r, slots[:-1]], axis=0).reshape(n_items, 1, tm)
    grid_spec = pltpu.PrefetchScalarGridSpec(
        num_scalar_prefetch=1,
        grid=(n_items,),
        in_specs=[pl.BlockSpec((1, 1, tm), lambda w, m: (w, 0, 0), memory_space=pltpu.SMEM),
                  pl.BlockSpec((tm * nj, LANES), lambda w, m: (m[0, w], 0)),
                  pl.BlockSpec((None, None, D, dh), lambda w, m: (layer, m[1, w], 0, 0)),
                  pl.BlockSpec((None, None, D, dh), lambda w, m: (layer, m[1, w], 0, 0)),
                  pl.BlockSpec((None, None, dh, D), lambda w, m: (layer, m[1, w], 0, 0))],
        out_specs=pl.BlockSpec(memory_space=pl.ANY),
        scratch_shapes=[pltpu.VMEM((2, tm * nj, LANES), F32), pltpu.SemaphoreType.DMA((2,))],
    )
    return pl.pallas_call(
        functools.partial(_experts_kernel, tm=tm, nj=nj, n_assign=n_assign),
        grid_spec=grid_spec,
        out_shape=jax.ShapeDtypeStruct(((n_assign + 2 * tm) * nj, LANES), F32),
        compiler_params=pltpu.CompilerParams(dimension_semantics=("arbitrary",), vmem_limit_bytes=VMEM_LIMIT,
                                             has_side_effects=True),
        name="moe_experts",
    )(meta, slots, xs, w1, w3, w2)


def _combine_kernel(y2_ref, route_ref, x1r_ref, g_ref, b_ref, o_ref, *, tc, nj, alpha):
    o_ref[...] = _combined_rows(y2_ref, route_ref, x1r_ref, g_ref, b_ref, tc=tc, nj=nj, alpha=alpha)


def _combine(y2, route, x1r, g, b, layer, alpha):
    T = route.shape[0]
    nj = x1r.shape[0] // T
    D = nj * LANES
    tc = _row_tile(T, 512)
    row = lambda n: pl.BlockSpec((tc, n), lambda i: (i, 0))
    lw = lambda r, c: pl.BlockSpec((None, r, c), lambda i: (layer, 0, 0))
    return pl.pallas_call(
        functools.partial(_combine_kernel, tc=tc, nj=nj, alpha=alpha),
        grid=(T // tc,),
        in_specs=[pl.BlockSpec((tc * TOP_K * nj, LANES), lambda i: (i, 0)), row(ROUTE_COLS),
                  pl.BlockSpec((tc * nj, LANES), lambda i: (i, 0)), lw(1, D), lw(1, D)],
        out_specs=row(D),
        out_shape=jax.ShapeDtypeStruct((T, D), F32),
        compiler_params=_params(("parallel",)),
        name="moe_combine",
    )(y2, route, x1r, g, b)


def _routing_tables(route, counts, tm, n_items):
    e = route[:, 0:2].astype(I32)
    rank = route[:, 4:6].astype(I32)
    cnt = counts[0, :N_EXPERTS].astype(I32)
    ends = jnp.cumsum(cnt)
    starts = ends - cnt
    dest = (starts[e] + rank).reshape(-1)

    first_tile = starts // tm
    last_tile = jnp.maximum(ends - 1, 0) // tm
    n_tiles_e = jnp.where(cnt > 0, last_tile - first_tile + 1, 0)
    item_end = jnp.cumsum(n_tiles_e)
    item_start = item_end - n_tiles_e
    n_real = item_end[-1]
    w = jnp.arange(n_items, dtype=I32)
    wc = jnp.minimum(w, n_real - 1)
    ex = jnp.sum((item_end[None, :] <= wc[:, None]).astype(I32), axis=1)
    tile = first_tile[ex] + (wc - item_start[ex])
    lo = jnp.clip(starts[ex] - tile * tm, 0, tm)
    hi = jnp.clip(ends[ex] - tile * tm, 0, tm)
    real = w < n_real
    hi = jnp.where(real, hi, lo)
    meta = jnp.stack([tile, ex, lo, hi]).astype(I32)
    return dest, meta


def kernel(x, ln_in_g, ln_in_b, w_in, b_in, w_fox_branch, hgrn_lb_logits, hgrn_norm_g, w_hgrn_branch,
           w_mix_out, b_mix_out, ln1_g, ln1_b, router_w, router_b, expert_w1, expert_w3, expert_w2,
           ln2_g, ln2_b):
    batch, seq, D = x.shape
    depth = w_in.shape[0]
    T = batch * seq
    nj = D // LANES
    alpha = float((2 * depth) ** 0.25)
    assert D % 512 == 0 and seq % HGRN_CHUNK == 0

    sizes = (FOX_WIDTH, FOX_WIDTH, FOX_WIDTH, FOX_HEADS, HGRN_WIDTH, HGRN_WIDTH, HGRN_WIDTH, HGRN_WIDTH, 2 * D)
    offs = [0]
    for s in sizes:
        offs.append(offs[-1] + s)
    col = lambda a, i: a[..., offs[i]:offs[i + 1]]
    order = (0, 1, 2, 4, 6, 7, 8, 5)
    pad_ff = lambda a: jnp.pad(col(a, 3), [(0, 0)] * (a.ndim - 1) + [(0, LANES - FOX_HEADS)])
    w_all = jnp.concatenate([col(w_in, i) for i in order] + [pad_ff(w_in)], axis=-1).astype(BF16)
    b_all = jnp.concatenate([col(b_in, i) for i in order] + [pad_ff(b_in)], axis=-1).astype(F32)[:, None, :]

    lb_p = jax.nn.softmax(hgrn_lb_logits.astype(F32), axis=0)
    lb_all = (jnp.cumsum(lb_p, axis=0) - lb_p[0]).reshape(depth * HGRN_PAIRS, 1, LANES)
    ng_all = hgrn_norm_g.astype(F32).reshape(depth * HGRN_PAIRS, 1, LANES)

    wf = w_fox_branch.astype(BF16)
    wh = w_hgrn_branch.astype(BF16)
    wm = w_mix_out.astype(BF16)
    r3 = lambda a: a.astype(F32)[:, None, :]
    bm, g1, b1, g2, b2 = r3(b_mix_out), r3(ln1_g), r3(ln1_b), r3(ln2_g), r3(ln2_b)
    rw = jnp.pad(router_w.astype(F32), ((0, 0), (0, LANES - N_EXPERTS)))
    wrh = rw.astype(BF16)
    wrl = (rw - wrh.astype(F32)).astype(BF16)
    rbias = jnp.pad(router_b.astype(F32), (0, LANES - N_EXPERTS)).reshape(1, LANES)

    tm_merge = _row_tile(T, 512)
    tri = jnp.tril(jnp.ones((tm_merge, tm_merge), BF16), k=-1)
    tm_exp = _row_tile(2 * T, 512)
    n_items = (2 * T) // tm_exp + N_EXPERTS
    tm_in = _row_tile(seq, 512)
    fox_block = _row_tile(tm_in, 256)
    tri_in = jnp.tril(jnp.ones((tm_in, tm_in), BF16))

    source = (x.reshape(T, D), ln_in_g.astype(F32).reshape(1, D), ln_in_b.astype(F32).reshape(1, D))
    for l in range(depth):
        xc, qp, kp, vt, hq, hi, hg, gates, hf = _inproj(source, w_all, b_all, tri_in, l, seq, fox_block, alpha)
        fox_o = _fox_attention(qp, kp, vt, batch, seq, fox_block)
        hgrn_o = _hgrn(hq, hf, hi, hg, lb_all, ng_all, l, batch, seq)
        x1r, route, counts = _merge(fox_o, hgrn_o, gates, xc, wf, wh, wm, bm, g1, b1,
                                    wrh, wrl, rbias, tri, l, alpha)
        dest, meta = _routing_tables(route, counts, tm_exp, n_items)
        xs, inv = _dispatch(x1r, dest, T, nj)
        y2 = _experts(meta, inv, xs, expert_w1, expert_w3, expert_w2, l, tm_exp, nj)
        source = (y2, route, x1r, g2, b2)
    out = _combine(*source, depth - 1, alpha)
    return out.reshape(batch, seq, D)
```
